```python
import math
import jax, jax.numpy as jnp
from jax import lax
import numpy as np

D_MODEL = 1024
BATCH = 32
SEQ = 256
DEPTH = 4
DEC_BATCH = 8
DEC_SEQ = 2048
PAST_LEN = 512

GRID_W = 64
N_BRANCH = 3
BR_W = 512
HY_W = 512
HY_ORDER = 2
HY_SHORT = 3
HY_BANDS = 16
HY_POS_DIM = 1 + 2 * HY_BANDS
HY_FFN = 64
DA_HEADS = 4
DA_HEAD_DIM = 64
DA_V_DIM = 2 * DA_HEAD_DIM
MLA_HEADS = 8
MLA_NOPE = 64
MLA_ROPE = 32
MLA_V = 64
MLA_Q_LORA = 384
MLA_KV_LORA = 256
ROPE_BASE = 10000.0
Q_BLOCK = 128
ALPHA = (2 * DEPTH) ** 0.25
BETA = (8 * DEPTH) ** -0.25
LN_EPS = 1e-5
RMS_EPS = 1e-6

IN_SIZES = (3 * HY_W,
            2 * DA_HEADS * DA_HEAD_DIM,
            2 * DA_HEADS * DA_HEAD_DIM,
            DA_HEADS * DA_V_DIM,
            MLA_Q_LORA,
            MLA_KV_LORA,
            MLA_ROPE,
            N_BRANCH * BR_W,
            N_BRANCH * D_MODEL)
IN_SPLITS = tuple(int(v) for v in np.cumsum(IN_SIZES)[:-1])
D_IN = int(sum(IN_SIZES))

kernel_name = 'hybrid_hyena_diffattn_mla_dit_step'


def layer_norm(x, g, b):
    xf = x.astype(jnp.float32)
    mu = jnp.mean(xf, axis=-1, keepdims=True)
    var = jnp.mean(jnp.square(xf - mu), axis=-1, keepdims=True)
    y = (xf - mu) * lax.rsqrt(var + LN_EPS) * g.astype(jnp.float32) + b.astype(jnp.float32)
    return y.astype(x.dtype)


def rms_norm(x, g):
    xf = x.astype(jnp.float32)
    y = xf * lax.rsqrt(jnp.mean(jnp.square(xf), axis=-1, keepdims=True) + RMS_EPS)
    return (y * g.astype(jnp.float32)).astype(x.dtype)


def grid_positions(n_tok):
    rows = n_tok // GRID_W
    row = jnp.repeat(jnp.arange(rows), GRID_W)
    col = jnp.tile(jnp.arange(GRID_W), rows)
    return row, col


def rope_1d(x, pos, n_mid):
    half = x.shape[-1] // 2
    freqs = 1.0 / (ROPE_BASE ** (jnp.arange(half, dtype=jnp.float32) / half))
    ang = pos.astype(jnp.float32)[:, None] * freqs
    ang = ang.reshape((ang.shape[0],) + (1,) * n_mid + (half,))
    cos = jnp.cos(ang).astype(x.dtype)
    sin = jnp.sin(ang).astype(x.dtype)
    x1, x2 = x[..., :half], x[..., half:]
    return jnp.concatenate([x1 * cos - x2 * sin, x1 * sin + x2 * cos], axis=-1)


def axial_rope(x, row, col, n_mid):
    d = x.shape[-1]
    return jnp.concatenate([rope_1d(x[..., :d // 2], row, n_mid),
                            rope_1d(x[..., d // 2:], col, n_mid)], axis=-1)


def short_conv(u, w, b):
    up = jnp.pad(u, ((0, 0), (1, 1), (0, 0)))
    return up[:, :-2] * w[0] + up[:, 1:-1] * w[1] + up[:, 2:] * w[2] + b


def hyena_filters(n_tok, w1, b1, w2, b2, w3, sin_freq, log_decay):
    t = jnp.arange(n_tok, dtype=jnp.float32)
    t_lin = t / (n_tok - 1)
    bands = jnp.arange(1, HY_BANDS + 1, dtype=jnp.float32)
    ang = (2.0 * math.pi / n_tok) * t[:, None] * bands
    feats = jnp.concatenate([t_lin[:, None], jnp.cos(ang), -jnp.sin(ang)], axis=-1)
    f32 = lambda a: a.astype(jnp.float32)
    sf = f32(sin_freq)
    hid = jnp.sin(sf[0] * (feats @ f32(w1) + f32(b1)))
    hid = jnp.sin(sf[1] * (hid @ f32(w2) + f32(b2)))
    h = hid @ f32(w3)
    h = h * jnp.exp(-t_lin[:, None] * jnp.exp(f32(log_decay)))
    return h.reshape(n_tok, 2, HY_ORDER, HY_W)


def bidir_long_conv(z, h_fwd, h_bwd, skip):
    n_tok, ch = z.shape[1], z.shape[2]
    circ = jnp.concatenate([h_fwd, jnp.zeros((1, ch), jnp.float32),
                            jnp.flip(h_bwd[1:], axis=0)], axis=0)
    zf = jnp.fft.rfft(z.astype(jnp.float32), n=2 * n_tok, axis=1)
    cf = jnp.fft.rfft(circ, axis=0)
    y = jnp.fft.irfft(zf * cf[None], n=2 * n_tok, axis=1)[:, :n_tok]
    return (y + z.astype(jnp.float32) * skip.astype(jnp.float32)).astype(z.dtype)


def hyena(u, conv_w, conv_b, filt, skip):
    u = short_conv(u, conv_w, conv_b)
    v, x1, x2 = jnp.split(u, 3, axis=-1)
    z = v
    for n, gate in enumerate((x1, x2)):
        z = gate * bidir_long_conv(z, filt[:, 0, n], filt[:, 1, n], skip[n])
    return z


def _to_blocks(a):
    b, nb = a.shape[0], a.shape[1] // Q_BLOCK
    return jnp.moveaxis(a.reshape((b, nb, Q_BLOCK) + a.shape[2:]), 1, 0)


def _from_blocks(o):
    nb, b = o.shape[0], o.shape[1]
    return jnp.moveaxis(o, 0, 1).reshape((b, nb * o.shape[2]) + o.shape[3:])


def diff_attention(q, k, v, lam):
    scale = DA_HEAD_DIM ** -0.5

    def block(qb):
        s = jnp.einsum('bqjhd,bkjhd->bjhqk', qb, k, preferred_element_type=jnp.float32) * scale
        pr = jax.nn.softmax(s, axis=-1)
        pr = pr[:, 0] - lam * pr[:, 1]
        return jnp.einsum('bhqk,bkhe->bqhe', pr.astype(v.dtype), v)

    return _from_blocks(lax.map(block, _to_blocks(q)))


def mla_attention(q_nope, q_rope, k_nope, k_rope, v):
    scale = (MLA_NOPE + MLA_ROPE) ** -0.5

    def block(qs):
        qn, qr = qs
        s = (jnp.einsum('bqhd,bkhd->bhqk', qn, k_nope, preferred_element_type=jnp.float32)
             + jnp.einsum('bqhr,bkr->bhqk', qr, k_rope, preferred_element_type=jnp.float32)) * scale
        pr = jax.nn.softmax(s, axis=-1)
        return jnp.einsum('bhqk,bkhe->bqhe', pr.astype(v.dtype), v)

    return _from_blocks(lax.map(block, (_to_blocks(q_nope), _to_blocks(q_rope))))


def trunk_layer(x, mod, l, p, pos=None, ctx=None):
    b, n, _ = x.shape
    shift, scale, gate = jnp.split(mod, 3, axis=-1)
    h = x * (1 + scale) + shift
    proj = h @ p['w_in'][l]
    hy_in, da_q, da_k, da_v, mla_dq, mla_dkv, mla_kr, paths, merge = jnp.split(proj, IN_SPLITS, axis=-1)

    filt = hyena_filters(n, p['hy_ffn_w1'][l], p['hy_ffn_b1'][l], p['hy_ffn_w2'][l],
                         p['hy_ffn_b2'][l], p['hy_ffn_w3'][l], p['hy_sin_freq'][l],
                         p['hy_log_decay'][l])
    y_hy = hyena(hy_in, p['hy_conv_w'][l], p['hy_conv_b'][l], filt, p['hy_skip'][l])

    q = da_q.reshape(b, n, 2, DA_HEADS, DA_HEAD_DIM)
    k = da_k.reshape(b, n, 2, DA_HEADS, DA_HEAD_DIM)
    v = da_v.reshape(b, n, DA_HEADS, DA_V_DIM)
    c_q = rms_norm(mla_dq, p['mla_q_norm'][l])
    qm = (c_q @ p['mla_w_uq'][l]).reshape(b, n, MLA_HEADS, MLA_NOPE + MLA_ROPE)
    q_nope, q_rope = qm[..., :MLA_NOPE], qm[..., MLA_NOPE:]
    c_kv = rms_norm(mla_dkv, p['mla_kv_norm'][l])

    if ctx is None:
        new_ctx = (k, v, c_kv, mla_kr)
        k_all, v_all, ckv_all, kr_all = k, v, c_kv, mla_kr
    else:
        row, col = pos
        q = axial_rope(q, row, col, 2)
        q_rope = axial_rope(q_rope, row, col, 1)
        ctx_k, ctx_v, ctx_ckv, ctx_kr = ctx
        k_all = jnp.concatenate([ctx_k, axial_rope(k, row, col, 2)], axis=1)
        v_all = jnp.concatenate([ctx_v, v], axis=1)
        ckv_all = jnp.concatenate([ctx_ckv, c_kv], axis=1)
        kr_all = jnp.concatenate([ctx_kr, axial_rope(mla_kr, row, col, 0)], axis=1)
        new_ctx = None

    lam_init = 0.8 - 0.6 * math.exp(-0.3 * l)
    lp = p['da_lambda'][l].astype(jnp.float32)
    lam = jnp.exp(jnp.sum(lp[0] * lp[1])) - jnp.exp(jnp.sum(lp[2] * lp[3])) + lam_init
    o = diff_attention(q, k_all, v_all, lam)
    y_da = (rms_norm(o, p['da_subln'][l]) * (1 - lam_init)).reshape(b, n, DA_HEADS * DA_V_DIM)

    kv = (ckv_all @ p['mla_w_ukv'][l]).reshape(b, -1, MLA_HEADS, MLA_NOPE + MLA_V)
    y_mla = mla_attention(q_nope, q_rope, kv[..., :MLA_NOPE], kr_all,
                          kv[..., MLA_NOPE:]).reshape(b, n, MLA_HEADS * MLA_V)

    br = jnp.stack([y_hy, y_da, y_mla], axis=2) * jax.nn.silu(paths.reshape(b, n, N_BRANCH, BR_W))
    y_br = jnp.einsum('blni,nid->blnd', br, p['w_branch'][l])
    merged = jnp.sum(jax.nn.sigmoid(merge.reshape(b, n, N_BRANCH, D_MODEL)) * y_br, axis=2)
    out = merged @ p['w_out'][l]
    x = layer_norm(ALPHA * x + gate * out, p['ln_g'][l], p['ln_b'][l])
    return x, new_ctx


def setup_inputs(seed: int = 0) -> dict:
    key = jax.random.key(seed)
    ks = jax.random.split(key, 40)
    f32 = jnp.float32
    nrm = lambda k, shape, s: jax.random.normal(k, shape, f32) * s
    decay0 = jnp.log(jnp.linspace(3.07, 15.35, 2 * HY_ORDER * HY_W, dtype=f32))
    return {
        'x_prompt': nrm(ks[0], (BATCH, SEQ, D_MODEL), 1.0),
        'x_sample': nrm(ks[1], (DEC_BATCH, DEC_SEQ, D_MODEL), 1.0),
        'c': nrm(ks[2], (DEC_BATCH, D_MODEL), 1.0),
        'cache_diff_k': nrm(ks[3], (DEC_BATCH, DEPTH, PAST_LEN, 2, DA_HEADS, DA_HEAD_DIM), 1.0),
        'cache_diff_v': nrm(ks[4], (DEC_BATCH, DEPTH, PAST_LEN, DA_HEADS, DA_V_DIM), 1.0),
        'cache_mla_ckv': nrm(ks[5], (DEC_BATCH, DEPTH, PAST_LEN, MLA_KV_LORA), 1.0),
        'cache_mla_krope': nrm(ks[6], (DEC_BATCH, DEPTH, PAST_LEN, MLA_ROPE), 1.0),
        'c_ctx': nrm(ks[7], (D_MODEL,), 1.0),
        'w_mod': nrm(ks[8], (DEPTH, D_MODEL, 3 * D_MODEL), D_MODEL ** -0.5),
        'b_mod': nrm(ks[9], (DEPTH, 3 * D_MODEL), 0.02),
        'w_in': nrm(ks[10], (DEPTH, D_MODEL, D_IN), D_MODEL ** -0.5),
        'hy_conv_w': nrm(ks[11], (DEPTH, HY_SHORT, 3 * HY_W), HY_SHORT ** -0.5),
        'hy_conv_b': nrm(ks[12], (DEPTH, 3 * HY_W), 0.02),
        'hy_ffn_w1': nrm(ks[13], (DEPTH, HY_POS_DIM, HY_FFN), HY_POS_DIM ** -0.5),
        'hy_ffn_b1': nrm(ks[14], (DEPTH, HY_FFN), 0.1),
        'hy_ffn_w2': nrm(ks[15], (DEPTH, HY_FFN, HY_FFN), HY_FFN ** -0.5),
        'hy_ffn_b2': nrm(ks[16], (DEPTH, HY_FFN), 0.1),
        'hy_ffn_w3': nrm(ks[17], (DEPTH, HY_FFN, 2 * HY_ORDER * HY_W), 0.1 * HY_FFN ** -0.5),
        'hy_sin_freq': 1.0 + nrm(ks[18], (DEPTH, 2, HY_FFN), 0.05),
        'hy_log_decay': decay0[None, :] + nrm(ks[19], (DEPTH, 2 * HY_ORDER * HY_W), 0.05),
        'hy_skip': nrm(ks[20], (DEPTH, HY_ORDER, HY_W), 1.0),
        'da_lambda': nrm(ks[21], (DEPTH, 4, DA_HEAD_DIM), 0.1),
        'da_subln': 1.0 + nrm(ks[22], (DEPTH, DA_V_DIM), 0.02),
        'mla_q_norm': 1.0 + nrm(ks[23], (DEPTH, MLA_Q_LORA), 0.02),
        'mla_w_uq': nrm(ks[24], (DEPTH, MLA_Q_LORA, MLA_HEADS * (MLA_NOPE + MLA_ROPE)), MLA_Q_LORA ** -0.5),
        'mla_kv_norm': 1.0 + nrm(ks[25], (DEPTH, MLA_KV_LORA), 0.02),
        'mla_w_ukv': nrm(ks[26], (DEPTH, MLA_KV_LORA, MLA_HEADS * (MLA_NOPE + MLA_V)), MLA_KV_LORA ** -0.5),
        'w_branch': nrm(ks[27], (DEPTH, N_BRANCH, BR_W, D_MODEL), BETA * BR_W ** -0.5),
        'w_out': nrm(ks[28], (DEPTH, D_MODEL, D_MODEL), BETA * D_MODEL ** -0.5),
        'ln_g': 1.0 + nrm(ks[29], (DEPTH, D_MODEL), 0.02),
        'ln_b': nrm(ks[30], (DEPTH, D_MODEL), 0.02),
    }


def reference(x_prompt, x_sample, c, cache_diff_k, cache_diff_v, cache_mla_ckv, cache_mla_krope,
              c_ctx, w_mod, b_mod, w_in, hy_conv_w, hy_conv_b, hy_ffn_w1, hy_ffn_b1, hy_ffn_w2,
              hy_ffn_b2, hy_ffn_w3, hy_sin_freq, hy_log_decay, hy_skip, da_lambda, da_subln,
              mla_q_norm, mla_w_uq, mla_kv_norm, mla_w_ukv, w_branch, w_out, ln_g, ln_b):
    p = {'w_in': w_in, 'hy_conv_w': hy_conv_w, 'hy_conv_b': hy_conv_b,
         'hy_ffn_w1': hy_ffn_w1, 'hy_ffn_b1': hy_ffn_b1, 'hy_ffn_w2': hy_ffn_w2,
         'hy_ffn_b2': hy_ffn_b2, 'hy_ffn_w3': hy_ffn_w3, 'hy_sin_freq': hy_sin_freq,
         'hy_log_decay': hy_log_decay, 'hy_skip': hy_skip, 'da_lambda': da_lambda,
         'da_subln': da_subln, 'mla_q_norm': mla_q_norm, 'mla_w_uq': mla_w_uq,
         'mla_kv_norm': mla_kv_norm, 'mla_w_ukv': mla_w_ukv, 'w_branch': w_branch,
         'w_out': w_out, 'ln_g': ln_g, 'ln_b': ln_b}

    xp = x_prompt
    ks_l, vs_l, ckv_l, kr_l = [], [], [], []
    for l in range(DEPTH):
        mod = (jax.nn.silu(c_ctx) @ w_mod[l] + b_mod[l])[None, None, :]
        xp, (k_c, v_c, ckv_c, kr_c) = trunk_layer(xp, mod, l, p)
        ks_l.append(k_c)
        vs_l.append(v_c)
        ckv_l.append(ckv_c)
        kr_l.append(kr_c)
    state_diff_k = jnp.stack(ks_l, axis=1)
    state_diff_v = jnp.stack(vs_l, axis=1)
    state_mla_ckv = jnp.stack(ckv_l, axis=1)
    state_mla_krope = jnp.stack(kr_l, axis=1)

    pos = grid_positions(x_sample.shape[1])
    xs = x_sample
    for l in range(DEPTH):
        mod = (jax.nn.silu(c) @ w_mod[l] + b_mod[l])[:, None, :]
        ctx = (cache_diff_k[:, l], cache_diff_v[:, l], cache_mla_ckv[:, l], cache_mla_krope[:, l])
        xs, _ = trunk_layer(xs, mod, l, p, pos=pos, ctx=ctx)

    return (xp, xs, state_diff_k, state_diff_v, state_mla_ckv, state_mla_krope)
```

```python
import functools
import math

import numpy as np
import jax
import jax.numpy as jnp
from jax import lax
from jax.experimental import pallas as pl
from jax.experimental.pallas import tpu as pltpu

F32 = jnp.float32
BF16 = jnp.bfloat16

D_MODEL = 1024
N_BRANCH = 3
BR_W = 512
HY_W = 512
HY_ORDER = 2
HY_BANDS = 16
HY_FFN = 64
DA_HEADS = 4
DA_HEAD_DIM = 64
DA_V_DIM = 2 * DA_HEAD_DIM
MLA_HEADS = 8
MLA_NOPE = 64
MLA_ROPE = 32
MLA_V = 64
MLA_Q_LORA = 384
MLA_KV_LORA = 256
GRID_W = 64
ROPE_BASE = 10000.0
LN_EPS = 1e-5
RMS_EPS = 1e-6

LANES = 128
MLA_HEAD_PAD = LANES
VMEM_LIMIT = 56 * 1024 * 1024

COL_HY = 0
COL_Q = 3 * HY_W
COL_K = COL_Q + 2 * DA_HEADS * DA_HEAD_DIM
COL_V = COL_K + 2 * DA_HEADS * DA_HEAD_DIM
COL_SMALL = COL_V + DA_HEADS * DA_V_DIM
SMALL_W = 1024
SMALL_USED = MLA_Q_LORA + MLA_KV_LORA + MLA_ROPE
COL_MERGE = COL_SMALL + SMALL_W
COL_PATHS = COL_MERGE + N_BRANCH * D_MODEL
D_IN_P = COL_PATHS + N_BRANCH * BR_W
ORIG_PATHS = COL_SMALL + SMALL_USED
ORIG_MERGE = ORIG_PATHS + N_BRANCH * BR_W
ORIG_END = ORIG_MERGE + N_BRANCH * D_MODEL


def _cparams(sem):
    return pltpu.CompilerParams(dimension_semantics=sem, vmem_limit_bytes=VMEM_LIMIT)


def _split_bf16(x):
    hi = x.astype(BF16)
    lo = (x - hi.astype(F32)).astype(BF16)
    return hi, lo


def _dot(a, b):
    return jnp.dot(a, b, preferred_element_type=F32)


def _dot_nt(a, b):
    return lax.dot_general(a, b, (((1,), (1,)), ((), ())), preferred_element_type=F32)


def _dot3(a, b):
    ah, al = _split_bf16(a)
    bh, bl = _split_bf16(b)
    return _dot(ah, bh) + _dot(ah, bl) + _dot(al, bh)


def _dot2(m, x):
    xh, xl = _split_bf16(x)
    return _dot(m, xh) + _dot(m, xl)


def _sigmoid(x):
    return 1.0 / (1.0 + jnp.exp(-x))


@functools.lru_cache(maxsize=None)
def _dft_tables(n_tok):
    k = np.arange(n_tok, dtype=np.int64)
    ang = np.pi * ((np.outer(k, k) % (2 * n_tok)).astype(np.float64)) / n_tok
    return (jnp.asarray(np.cos(ang), dtype=BF16), jnp.asarray(np.sin(ang), dtype=BF16))


@functools.lru_cache(maxsize=None)
def _hyena_feats(n_tok):
    t = np.arange(n_tok, dtype=np.float64)
    t_lin = t / (n_tok - 1)
    bands = np.arange(1, HY_BANDS + 1, dtype=np.float64)
    ang = (2.0 * np.pi / n_tok) * t[:, None] * bands
    feats = np.concatenate([t_lin[:, None], np.cos(ang), -np.sin(ang)], axis=-1)
    out = np.zeros((n_tok, LANES), np.float64)
    out[:, :feats.shape[1]] = feats
    return jnp.asarray(out, dtype=F32)


def _rope_pattern(pos_row, pos_col, dd):
    q = dd // 4
    freqs = 1.0 / (ROPE_BASE ** (np.arange(q, dtype=np.float64) / q))
    n = pos_row.shape[0]
    cos = np.zeros((n, dd)); sa = np.zeros((n, dd)); sb = np.zeros((n, dd))
    for g, pos in enumerate((pos_row, pos_col)):
        ang = pos[:, None].astype(np.float64) * freqs
        base = g * 2 * q
        cos[:, base:base + q] = np.cos(ang); cos[:, base + q:base + 2 * q] = np.cos(ang)
        sa[:, base:base + q] = -np.sin(ang)
        sb[:, base + q:base + 2 * q] = np.sin(ang)
    return cos, sa, sb


@functools.lru_cache(maxsize=None)
def _rope_tables(n_tok):
    t = np.arange(n_tok)
    row, col = t // GRID_W, t % GRID_W
    ones = np.ones((n_tok, LANES)); zeros = np.zeros((n_tok, LANES))
    c, a, b = _rope_pattern(row, col, DA_HEAD_DIM)
    da = [np.tile(c, (1, 2)), np.tile(a, (1, 2)), np.tile(b, (1, 2))]
    c, a, b = _rope_pattern(row, col, MLA_ROPE)
    mq = [ones.copy(), zeros.copy(), zeros.copy()]
    kr = [ones.copy(), zeros.copy(), zeros.copy()]
    for dst, src in zip(mq, (c, a, b)):
        dst[:, MLA_NOPE:MLA_NOPE + MLA_ROPE] = src
    for dst, src in zip(kr, (c, a, b)):
        dst[:, :MLA_ROPE] = src
    tab = np.stack([np.stack(da), np.stack(mq), np.stack(kr)])
    return jnp.asarray(tab, dtype=F32)


def _rope_apply(x, tab_ref, kind, q):
    cos = tab_ref[kind, 0]
    sa = tab_ref[kind, 1]
    sb = tab_ref[kind, 2]
    outs = []
    for c in range(x.shape[1] // LANES):
        xc = x[:, c * LANES:(c + 1) * LANES]
        outs.append(xc * cos + pltpu.roll(xc, LANES - q, 1) * sa + pltpu.roll(xc, q, 1) * sb)
    return outs[0] if len(outs) == 1 else jnp.concatenate(outs, axis=1)


def _mod_kernel(c_ref, w_ref, b_ref, o_ref):
    c = c_ref[...]
    o_ref[0] = _dot3(c * _sigmoid(c), w_ref[0]) + b_ref[0]


def _modulation(cvec, w_mod, b_mod):
    depth = w_mod.shape[0]
    rows = cvec.shape[0]
    return pl.pallas_call(
        _mod_kernel,
        grid=(depth, 3),
        in_specs=[pl.BlockSpec((rows, D_MODEL), lambda l, j: (0, 0)),
                  pl.BlockSpec((1, D_MODEL, D_MODEL), lambda l, j: (l, 0, j)),
                  pl.BlockSpec((1, 1, D_MODEL), lambda l, j: (l, 0, j))],
        out_specs=pl.BlockSpec((1, rows, D_MODEL), lambda l, j: (l, 0, j)),
        out_shape=jax.ShapeDtypeStruct((depth, rows, 3 * D_MODEL), F32),
        compiler_params=_cparams(("arbitrary", "arbitrary")),
        name="modulation",
    )(cvec, w_mod, b_mod.reshape(depth, 1, 3 * D_MODEL))


def _filt_kernel(feats_ref, w1_ref, b1_ref, w2_ref, b2_ref, w3f_ref, w3b_ref, sf_ref,
                 ldf_ref, ldb_ref, c_ref, s_ref, kr_ref, ki_ref, kny_ref):
    n_tok = feats_ref.shape[0]
    inv_n = 1.0 / (2 * n_tok)
    feats = feats_ref[...]
    sf = sf_ref[0]
    hid = jnp.sin(sf[0:1] * (_dot3(feats, w1_ref[0]) + b1_ref[0]))
    hid = jnp.sin(sf[1:2] * (_dot3(hid, w2_ref[0]) + b2_ref[0]))
    t_lin = feats[:, 0:1]
    hf = _dot3(hid, w3f_ref[0]) * jnp.exp(-t_lin * jnp.exp(ldf_ref[0]))
    hb = _dot3(hid, w3b_ref[0]) * jnp.exp(-t_lin * jnp.exp(ldb_ref[0]))
    row = lax.broadcasted_iota(jnp.int32, hf.shape, 0)
    hb = jnp.where(row == 0, 0.0, hb)
    a = hf + hb
    d = hf - hb
    wk = jnp.where(row == 0, inv_n, 2.0 * inv_n)
    kr_ref[0, 0] = _dot2(c_ref[...], a) * wk
    ki_ref[0, 0] = -_dot2(s_ref[...], d) * wk
    sign = jnp.where((row & 1) == 0, 1.0, -1.0)
    kny_ref[0, 0] = jnp.sum(a * sign, axis=0, keepdims=True) * inv_n


def _hyena_spectra(n_tok, w1p, b1, w2, b2, w3, sf, log_decay):
    depth = w3.shape[0]
    ct = 256
    nc = HY_W // ct
    cmat, smat = _dft_tables(n_tok)
    feats = _hyena_feats(n_tok)
    const = lambda shape: pl.BlockSpec(shape, lambda l, n, c: (0,) * len(shape),
                                       pipeline_mode=pl.Buffered(1))
    per_l = lambda shape: pl.BlockSpec((1,) + shape, lambda l, n, c: (l, 0, 0))
    fcol = lambda l, n, c: (l, 0, n * nc + c)
    bcol = lambda l, n, c: (l, 0, HY_ORDER * nc + n * nc + c)
    out_spec = pl.BlockSpec((1, 1, n_tok, ct), lambda l, n, c: (l, n, 0, c))
    return pl.pallas_call(
        _filt_kernel,
        grid=(depth, HY_ORDER, nc),
        in_specs=[const((n_tok, LANES)),
                  per_l((LANES, HY_FFN)), per_l((1, HY_FFN)),
                  per_l((HY_FFN, HY_FFN)), per_l((1, HY_FFN)),
                  pl.BlockSpec((1, HY_FFN, ct), fcol), pl.BlockSpec((1, HY_FFN, ct), bcol),
                  per_l((2, HY_FFN)),
                  pl.BlockSpec((1, 1, ct), fcol), pl.BlockSpec((1, 1, ct), bcol),
                  const((n_tok, n_tok)), const((n_tok, n_tok))],
        out_specs=[out_spec, out_spec,
                   pl.BlockSpec((1, 1, 1, ct), lambda l, n, c: (l, n, 0, c))],
        out_shape=[jax.ShapeDtypeStruct((depth, HY_ORDER, n_tok, HY_W), F32),
                   jax.ShapeDtypeStruct((depth, HY_ORDER, n_tok, HY_W), F32),
                   jax.ShapeDtypeStruct((depth, HY_ORDER, 1, HY_W), F32)],
        compiler_params=_cparams(("arbitrary",) * 3),
        name="hyena_spectra",
    )(feats, w1p, b1.reshape(depth, 1, HY_FFN), w2, b2.reshape(depth, 1, HY_FFN), w3, w3, sf,
      log_decay.reshape(depth, 1, -1), log_decay.reshape(depth, 1, -1), cmat, smat)


def _inproj_kernel(x_ref, mod_ref, w_ref, o_ref):
    shift = mod_ref[0, 0:1, :]
    scale = mod_ref[0, 1:2, :]
    h = (x_ref[...] * (1.0 + scale) + shift).astype(BF16)
    o_ref[...] = _dot(h, w_ref[0])


def _inproj(x2d, mod, w_in_p, l, seq_len):
    t = x2d.shape[0]
    per_batch = mod.shape[0] > 1
    tm = min(512, seq_len if per_batch else t)
    tn = D_IN_P // 4
    tiles_per_batch = seq_len // tm
    mod_idx = (lambda j, i: (i // tiles_per_batch, 0, 0)) if per_batch else (lambda j, i: (0, 0, 0))
    return pl.pallas_call(
        _inproj_kernel,
        grid=(D_IN_P // tn, t // tm),
        in_specs=[pl.BlockSpec((tm, D_MODEL), lambda j, i: (i, 0)),
                  pl.BlockSpec((1, 3, D_MODEL), mod_idx),
                  pl.BlockSpec((1, D_MODEL, tn), lambda j, i: (l, 0, j))],
        out_specs=pl.BlockSpec((tm, tn), lambda j, i: (i, j)),
        out_shape=jax.ShapeDtypeStruct((t, D_IN_P), F32),
        compiler_params=_cparams(("arbitrary", "arbitrary")),
        name="inproj",
    )(x2d, mod, w_in_p)


def _short_conv(x, w_ref, b_ref):
    n_tok = x.shape[0]
    row = lax.broadcasted_iota(jnp.int32, x.shape, 0)
    prev = jnp.where(row == 0, 0.0, pltpu.roll(x, 1, 0))
    nxt = jnp.where(row == n_tok - 1, 0.0, pltpu.roll(x, n_tok - 1, 0))
    return prev * w_ref[0:1, :] + x * w_ref[1:2, :] + nxt * w_ref[2:3, :] + b_ref[...]


def _hyena_kernel(*refs, conv_z, tk):
    if conv_z:
        (zin_ref, gin_ref, wz_ref, bz_ref, wg_ref, bg_ref, kr_ref, ki_ref, kny_ref, skip_ref,
         c_ref, s_ref, o_ref, zb_ref, acc_ref) = refs
    else:
        (zin_ref, gin_ref, wg_ref, bg_ref, kr_ref, ki_ref, kny_ref, skip_ref,
         c_ref, s_ref, o_ref, zb_ref, acc_ref) = refs
    nb, n_tok, ct = zin_ref.shape
    for i in range(nb):
        z = zin_ref[i]
        if conv_z:
            z = _short_conv(z, wz_ref, bz_ref)
        zb_ref[...] = z.astype(BF16)
        for j in range(n_tok // tk):
            rows = slice(j * tk, (j + 1) * tk)
            zr = _dot(c_ref[rows, :], zb_ref[...])
            zs = _dot(s_ref[rows, :], zb_ref[...])
            kr = kr_ref[0, 0, rows, :]
            ki = ki_ref[0, 0, rows, :]
            yr = (zr * kr + zs * ki).astype(BF16)
            ym = (zs * kr - zr * ki).astype(BF16)
            part = _dot(c_ref[:, rows], yr) + _dot(s_ref[:, rows], ym)
            if j == 0:
                acc_ref[...] = part
            else:
                acc_ref[...] += part
        row = lax.broadcasted_iota(jnp.int32, z.shape, 0)
        sign = jnp.where((row & 1) == 0, 1.0, -1.0)
        zny = jnp.sum(z * sign, axis=0, keepdims=True)
        y = acc_ref[...] + sign * (zny * kny_ref[0, 0]) + z * skip_ref[0, 0]
        o_ref[i] = _short_conv(gin_ref[i], wg_ref, bg_ref) * y


def _hyena_order(zin, zblk, gin, gblk, conv_w, conv_b, kr, ki, kny, skip, l, order, conv_z):
    b, n_tok = zin.shape[0], zin.shape[1]
    cmat, smat = _dft_tables(n_tok)
    if n_tok >= 2048:
        ct, nb = 256, 1
    else:
        ct, nb = 512, min(8, b)
    nc = HY_W // ct
    tk = min(256, n_tok)
    const = lambda shape: pl.BlockSpec(shape, lambda c, i: (0,) * len(shape),
                                       pipeline_mode=pl.Buffered(1))
    zspec = pl.BlockSpec((nb, n_tok, ct), lambda c, i: (i, 0, zblk * nc + c))
    gspec = pl.BlockSpec((nb, n_tok, ct), lambda c, i: (i, 0, gblk * nc + c))
    wspec = lambda blk: pl.BlockSpec((1, 3, ct), lambda c, i: (l, 0, blk * nc + c))
    bspec = lambda blk: pl.BlockSpec((1, 1, ct), lambda c, i: (l, 0, blk * nc + c))
    kspec = pl.BlockSpec((1, 1, n_tok, ct), lambda c, i: (l, order, 0, c))
    in_specs = [zspec, gspec]
    args = [zin, gin]
    if conv_z:
        in_specs += [wspec(0), bspec(0)]
        args += [conv_w, conv_b]
    in_specs += [wspec(order + 1), bspec(order + 1), kspec, kspec,
                 pl.BlockSpec((1, 1, 1, ct), lambda c, i: (l, order, 0, c)),
                 pl.BlockSpec((1, 1, 1, ct), lambda c, i: (l, order, 0, c)),
                 const((n_tok, n_tok)), const((n_tok, n_tok))]
    args += [conv_w, conv_b, kr, ki, kny, skip, cmat, smat]

    def body(*refs):
        refs = list(refs)
        lo = 2
        n_small = 4 if conv_z else 2
        for k in range(lo, lo + n_small):
            refs[k] = refs[k].at[0]
        _hyena_kernel(*refs, conv_z=conv_z, tk=tk)

    return pl.pallas_call(
        body,
        grid=(nc, b // nb),
        in_specs=in_specs,
        out_specs=pl.BlockSpec((nb, n_tok, ct), lambda c, i: (i, 0, c)),
        out_shape=jax.ShapeDtypeStruct((b, n_tok, HY_W), F32),
        scratch_shapes=[pltpu.VMEM((n_tok, ct), BF16), pltpu.VMEM((n_tok, ct), F32)],
        compiler_params=_cparams(("arbitrary", "arbitrary")),
        name=f"hyena_order{order}",
    )(*args)


def _rms(x, g, n):
    ms = jnp.sum(x * x, axis=-1, keepdims=True) * (1.0 / n)
    return x * lax.rsqrt(ms + RMS_EPS) * g


def _prep_kernel(*refs, rope):
    if rope:
        (q_ref, k_ref, v_ref, sm_ref, qn_ref, wuq_ref, kvn_ref, wk_ref, wv_ref, e_ref, tab_ref,
         qd_ref, kd_ref, vd_ref, qm_ref, km_ref, vm_ref, ckv_ref) = refs
    else:
        (q_ref, k_ref, v_ref, sm_ref, qn_ref, wuq_ref, kvn_ref, wk_ref, wv_ref, e_ref,
         qd_ref, kd_ref, vd_ref, qm_ref, km_ref, vm_ref, ckv_ref) = refs
    q = q_ref[...] * (DA_HEAD_DIM ** -0.5)
    k = k_ref[...]
    if rope:
        q = _rope_apply(q, tab_ref, 0, DA_HEAD_DIM // 4)
        k = _rope_apply(k, tab_ref, 0, DA_HEAD_DIM // 4)
    qd_ref[...] = q.astype(BF16)
    kd_ref[...] = k.astype(BF16)
    vd_ref[...] = v_ref[...].astype(BF16)

    dq = sm_ref[:, 0:MLA_Q_LORA]
    dkv = sm_ref[:, MLA_Q_LORA:MLA_Q_LORA + MLA_KV_LORA]
    kr = sm_ref[:, 5 * LANES:6 * LANES]
    c_q = _rms(dq, qn_ref[0], MLA_Q_LORA)
    qm = _dot(c_q.astype(BF16), wuq_ref[0]) * ((MLA_NOPE + MLA_ROPE) ** -0.5)
    c_kv = _rms(dkv, kvn_ref[0], MLA_KV_LORA)
    ckv_ref[...] = c_kv
    ckv_b = c_kv.astype(BF16)
    if rope:
        qm = _rope_apply(qm, tab_ref, 1, MLA_ROPE // 4)
        kr = _rope_apply(kr, tab_ref, 2, MLA_ROPE // 4)
    qm_ref[...] = qm.astype(BF16)
    km_ref[...] = (_dot(ckv_b, wk_ref[0]) + _dot(kr.astype(BF16), e_ref[...])).astype(BF16)
    vm_ref[...] = _dot(ckv_b, wv_ref[0]).astype(BF16)


def _prep(proj, wts, l, seq_len, rope):
    t = proj.shape[0]
    tm = min(512, seq_len if rope else t)
    tiles_per_seq = seq_len // tm
    blk = lambda w, idx: pl.BlockSpec((tm, w), lambda i: (i, idx))
    wl = lambda shape: pl.BlockSpec((1,) + shape, lambda i: (l, 0, 0))
    in_specs = [blk(512, COL_Q // 512), blk(512, COL_K // 512), blk(512, COL_V // 512),
                blk(SMALL_W, COL_SMALL // SMALL_W),
                wl((1, MLA_Q_LORA)), wl((MLA_Q_LORA, MLA_HEADS * MLA_HEAD_PAD)),
                wl((1, MLA_KV_LORA)), wl((MLA_KV_LORA, MLA_HEADS * MLA_HEAD_PAD)),
                wl((MLA_KV_LORA, MLA_HEADS * MLA_V)),
                pl.BlockSpec((LANES, MLA_HEADS * MLA_HEAD_PAD), lambda i: (0, 0))]
    args = [proj, proj, proj, proj, wts["q_norm"], wts["w_uq"], wts["kv_norm"], wts["w_uk"],
            wts["w_uv"], wts["e_place"]]
    if rope:
        in_specs.append(pl.BlockSpec((3, 3, tm, LANES), lambda i: (0, 0, i % tiles_per_seq, 0)))
        args.append(_rope_tables(seq_len))
    out = lambda w: pl.BlockSpec((tm, w), lambda i: (i, 0))
    return pl.pallas_call(
        functools.partial(_prep_kernel, rope=rope),
        grid=(t // tm,),
        in_specs=in_specs,
        out_specs=[out(512), out(512), out(512), out(1024), out(1024), out(512), out(MLA_KV_LORA)],
        out_shape=[jax.ShapeDtypeStruct((t, 512), BF16), jax.ShapeDtypeStruct((t, 512), BF16),
                   jax.ShapeDtypeStruct((t, 512), BF16), jax.ShapeDtypeStruct((t, 1024), BF16),
                   jax.ShapeDtypeStruct((t, 1024), BF16), jax.ShapeDtypeStruct((t, 512), BF16),
                   jax.ShapeDtypeStruct((t, MLA_KV_LORA), F32)],
        compiler_params=_cparams(("arbitrary",)),
        name="attn_prep",
    )(*args)


def _prep_cache_kernel(k_ref, v_ref, ckv_ref, kr_ref, wk_ref, wv_ref, e_ref,
                       kd_ref, vd_ref, km_ref, vm_ref):
    kd_ref[0] = k_ref[0, 0].astype(BF16)
    vd_ref[0] = v_ref[0, 0].astype(BF16)
    ckv_b = ckv_ref[0, 0].astype(BF16)
    km_ref[0] = (_dot(ckv_b, wk_ref[0]) + _dot(kr_ref[0, 0].astype(BF16), e_ref[...])).astype(BF16)
    vm_ref[0] = _dot(ckv_b, wv_ref[0]).astype(BF16)


def _prep_cache(cache_k, cache_v, cache_ckv, cache_kr, wts, l):
    b, _, p = cache_k.shape[:3]
    ck = cache_k.reshape(b, -1, p, 512)
    cv = cache_v.reshape(b, -1, p, 512)
    cin = lambda w: pl.BlockSpec((1, 1, p, w), lambda i: (i, l, 0, 0))
    wl = lambda shape: pl.BlockSpec((1,) + shape, lambda i: (l, 0, 0))
    out = lambda w: pl.BlockSpec((1, p, w), lambda i: (i, 0, 0))
    return pl.pallas_call(
        _prep_cache_kernel,
        grid=(b,),
        in_specs=[cin(512), cin(512), cin(MLA_KV_LORA), cin(MLA_ROPE),
                  wl((MLA_KV_LORA, MLA_HEADS * MLA_HEAD_PAD)), wl((MLA_KV_LORA, MLA_HEADS * MLA_V)),
                  pl.BlockSpec((MLA_ROPE, MLA_HEADS * MLA_HEAD_PAD), lambda i: (0, 0))],
        out_specs=[out(512), out(512), out(1024), out(512)],
        out_shape=[jax.ShapeDtypeStruct((b, p, 512), BF16), jax.ShapeDtypeStruct((b, p, 512), BF16),
                   jax.ShapeDtypeStruct((b, p, 1024), BF16), jax.ShapeDtypeStruct((b, p, 512), BF16)],
        compiler_params=_cparams(("arbitrary",)),
        name="attn_prep_cache",
    )(ck, cv, cache_ckv, cache_kr, wts["w_uk"], wts["w_uv"], wts["e_place"][:MLA_ROPE])


def _softmax_pv(q, k_parts, v_parts):
    s = [_dot_nt(q, k) for k in k_parts]
    m = s[0].max(axis=-1, keepdims=True)
    for sp in s[1:]:
        m = jnp.maximum(m, sp.max(axis=-1, keepdims=True))
    o = None
    den = None
    for sp, v in zip(s, v_parts):
        p = jnp.exp(sp - m)
        den = p.sum(axis=-1, keepdims=True) if den is None else den + p.sum(axis=-1, keepdims=True)
        pv = _dot(p.astype(BF16), v)
        o = pv if o is None else o + pv
    return o / den


def _da_kernel(*refs, lam_init, has_ctx):
    if has_ctx:
        q_ref, kn_ref, vn_ref, kc_ref, vc_ref, lam_ref, g_ref, o_ref = refs
    else:
        q_ref, kn_ref, vn_ref, lam_ref, g_ref, o_ref = refs
    lp = lam_ref[0]
    lam = (jnp.exp(jnp.sum(lp[0:1] * lp[1:2], axis=1, keepdims=True))
           - jnp.exp(jnp.sum(lp[2:3] * lp[3:4], axis=1, keepdims=True)) + lam_init)
    for h in range(DA_HEADS):
        vcols = slice(h * DA_V_DIM, (h + 1) * DA_V_DIM)
        outs = []
        for j in range(2):
            cols = slice((j * DA_HEADS + h) * DA_HEAD_DIM, (j * DA_HEADS + h + 1) * DA_HEAD_DIM)
            k_parts = [kn_ref[0, :, cols]]
            v_parts = [vn_ref[0, :, vcols]]
            if has_ctx:
                k_parts.insert(0, kc_ref[0, :, cols])
                v_parts.insert(0, vc_ref[0, :, vcols])
            outs.append(_softmax_pv(q_ref[0, :, cols], k_parts, v_parts))
        o = outs[0] - lam * outs[1]
        o_ref[0, :, vcols] = _rms(o, g_ref[0], DA_V_DIM) * (1.0 - lam_init)


def _da_attention(qd, kd, vd, ctx, da_lambda, da_subln, l, tq):
    b, n, _ = qd.shape
    has_ctx = ctx is not None
    lam_init = 0.8 - 0.6 * math.exp(-0.3 * l)
    full = lambda a: pl.BlockSpec((1,) + a.shape[1:], lambda i, j: (i, 0, 0))
    in_specs = [pl.BlockSpec((1, tq, 512), lambda i, j: (i, j, 0)), full(kd), full(vd)]
    args = [qd, kd, vd]
    if has_ctx:
        in_specs += [full(ctx[0]), full(ctx[1])]
        args += [ctx[0], ctx[1]]
    in_specs += [pl.BlockSpec((1, 4, DA_HEAD_DIM), lambda i, j: (l, 0, 0)),
                 pl.BlockSpec((1, 1, DA_V_DIM), lambda i, j: (l, 0, 0))]
    args += [da_lambda, da_subln.reshape(-1, 1, DA_V_DIM)]
    return pl.pallas_call(
        functools.partial(_da_kernel, lam_init=lam_init, has_ctx=has_ctx),
        grid=(b, n // tq),
        in_specs=in_specs,
        out_specs=pl.BlockSpec((1, tq, 512), lambda i, j: (i, j, 0)),
        out_shape=jax.ShapeDtypeStruct((b, n, 512), F32),
        compiler_params=_cparams(("arbitrary", "arbitrary")),
        name="diff_attention",
    )(*args)


def _mla_kernel(*refs, has_ctx):
    if has_ctx:
        q_ref, kn_ref, vn_ref, kc_ref, vc_ref, o_ref = refs
    else:
        q_ref, kn_ref, vn_ref, o_ref = refs
    for h in range(MLA_HEADS):
        cols = slice(h * MLA_HEAD_PAD, (h + 1) * MLA_HEAD_PAD)
        vcols = slice(h * MLA_V, (h + 1) * MLA_V)
        k_parts = [kn_ref[0, :, cols]]
        v_parts = [vn_ref[0, :, vcols]]
        if has_ctx:
            k_parts.insert(0, kc_ref[0, :, cols])
            v_parts.insert(0, vc_ref[0, :, vcols])
        o_ref[0, :, vcols] = _softmax_pv(q_ref[0, :, cols], k_parts, v_parts)


def _mla_attention(qm, km, vm, ctx, tq):
    b, n, _ = qm.shape
    has_ctx = ctx is not None
    full = lambda a: pl.BlockSpec((1,) + a.shape[1:], lambda i, j: (i, 0, 0))
    in_specs = [pl.BlockSpec((1, tq, 1024), lambda i, j: (i, j, 0)), full(km), full(vm)]
    args = [qm, km, vm]
    if has_ctx:
        in_specs += [full(ctx[0]), full(ctx[1])]
        args += [ctx[0], ctx[1]]
    return pl.pallas_call(
        functools.partial(_mla_kernel, has_ctx=has_ctx),
        grid=(b, n // tq),
        in_specs=in_specs,
        out_specs=pl.BlockSpec((1, tq, 512), lambda i, j: (i, j, 0)),
        out_shape=jax.ShapeDtypeStruct((b, n, 512), F32),
        compiler_params=_cparams(("arbitrary", "arbitrary")),
        name="latent_attention",
    )(*args)


def _outproj_kernel(x_ref, mod_ref, yh_ref, yd_ref, ym_ref, p0_ref, p1_ref, p2_ref,
                    m0_ref, m1_ref, m2_ref, wb_ref, wo_ref, g_ref, b_ref, o_ref, *, alpha):
    merged = None
    for y_ref, p_ref, m_ref, n in ((yh_ref, p0_ref, m0_ref, 0), (yd_ref, p1_ref, m1_ref, 1),
                                   (ym_ref, p2_ref, m2_ref, 2)):
        p = p_ref[...]
        br = (y_ref[...] * (p * _sigmoid(p))).astype(BF16)
        term = _sigmoid(m_ref[...]) * _dot(br, wb_ref[0, n])
        merged = term if merged is None else merged + term
    out = _dot(merged.astype(BF16), wo_ref[0])
    gate = mod_ref[0, 2:3, :]
    y = alpha * x_ref[...] + gate * out
    mu = jnp.mean(y, axis=-1, keepdims=True)
    yc = y - mu
    var = jnp.mean(yc * yc, axis=-1, keepdims=True)
    o_ref[...] = yc * lax.rsqrt(var + LN_EPS) * g_ref[0] + b_ref[0]


def _outproj(x2d, mod, y_hy, y_da, y_mla, proj, w_branch, w_out, ln_g, ln_b, l, seq_len, alpha):
    t = x2d.shape[0]
    per_batch = mod.shape[0] > 1
    tm = min(256, seq_len if per_batch else t)
    tiles_per_batch = seq_len // tm
    mod_idx = (lambda i: (i // tiles_per_batch, 0, 0)) if per_batch else (lambda i: (0, 0, 0))
    row = lambda w, idx=0: pl.BlockSpec((tm, w), lambda i: (i, idx))
    depth = ln_g.shape[0]
    return pl.pallas_call(
        functools.partial(_outproj_kernel, alpha=alpha),
        grid=(t // tm,),
        in_specs=[row(D_MODEL), pl.BlockSpec((1, 3, D_MODEL), mod_idx),
                  row(BR_W), row(BR_W), row(BR_W),
                  row(BR_W, COL_PATHS // BR_W), row(BR_W, COL_PATHS // BR_W + 1),
                  row(BR_W, COL_PATHS // BR_W + 2),
                  row(D_MODEL, COL_MERGE // D_MODEL), row(D_MODEL, COL_MERGE // D_MODEL + 1),
                  row(D_MODEL, COL_MERGE // D_MODEL + 2),
                  pl.BlockSpec((1, N_BRANCH, BR_W, D_MODEL), lambda i: (l, 0, 0, 0)),
                  pl.BlockSpec((1, D_MODEL, D_MODEL), lambda i: (l, 0, 0)),
                  pl.BlockSpec((1, 1, D_MODEL), lambda i: (l, 0, 0)),
                  pl.BlockSpec((1, 1, D_MODEL), lambda i: (l, 0, 0))],
        out_specs=row(D_MODEL),
        out_shape=jax.ShapeDtypeStruct((t, D_MODEL), F32),
        compiler_params=_cparams(("arbitrary",)),
        name="outproj",
    )(x2d, mod, y_hy, y_da, y_mla, proj, proj, proj, proj, proj, proj, w_branch, w_out,
      ln_g.reshape(depth, 1, D_MODEL), ln_b.reshape(depth, 1, D_MODEL))


def _layer(x, mod, l, wts, spectra, ctx=None):
    b, n, _ = x.shape
    depth = wts["ln_g"].shape[0]
    alpha = (2 * depth) ** 0.25
    x2d = x.reshape(b * n, D_MODEL)
    proj = _inproj(x2d, mod, wts["w_in"], l, n)
    proj3 = proj.reshape(b, n, D_IN_P)

    kr_s, ki_s, kny_s = spectra
    z2 = _hyena_order(proj3, 0, proj3, 1, wts["conv_w"], wts["conv_b"], kr_s, ki_s, kny_s,
                      wts["skip"], l, 0, True)
    y_hy = _hyena_order(z2, 0, proj3, 2, wts["conv_w"], wts["conv_b"], kr_s, ki_s, kny_s,
                        wts["skip"], l, 1, False)

    qd, kd, vd, qm, km, vm, ckv = _prep(proj, wts, l, n, rope=ctx is not None)
    r3 = lambda a: a.reshape(b, n, a.shape[-1])
    qd, kd, vd, qm, km, vm = map(r3, (qd, kd, vd, qm, km, vm))
    if ctx is not None:
        kd_c, vd_c, km_c, vm_c = _prep_cache(*ctx, wts, l)
        da_ctx, mla_ctx = (kd_c, vd_c), (km_c, vm_c)
    else:
        da_ctx = mla_ctx = None
    tq = min(256, n)
    y_da = _da_attention(qd, kd, vd, da_ctx, wts["da_lambda"], wts["da_subln"], l, tq)
    y_mla = _mla_attention(qm, km, vm, mla_ctx, tq)

    x_new = _outproj(x2d, mod, y_hy.reshape(b * n, HY_W), y_da.reshape(b * n, BR_W),
                     y_mla.reshape(b * n, BR_W), proj, wts["w_branch"], wts["w_out"],
                     wts["ln_g"], wts["ln_b"], l, n, alpha)
    new_ctx = (proj3[:, :, COL_K:COL_V], proj3[:, :, COL_V:COL_SMALL], ckv.reshape(b, n, MLA_KV_LORA),
               proj3[:, :, COL_SMALL + MLA_Q_LORA + MLA_KV_LORA:COL_SMALL + SMALL_USED])
    return x_new.reshape(b, n, D_MODEL), new_ctx


def _prepare_weights(w_in, hy_conv_w, hy_conv_b, hy_ffn_w1, hy_skip, da_lambda, da_subln,
                     mla_q_norm, mla_w_uq, mla_kv_norm, mla_w_ukv, w_branch, w_out, ln_g, ln_b):
    depth = w_in.shape[0]
    zpad = jnp.zeros((depth, D_MODEL, SMALL_W - SMALL_USED), w_in.dtype)
    w_in_p = jnp.concatenate([w_in[..., :ORIG_PATHS], zpad, w_in[..., ORIG_MERGE:ORIG_END],
                              w_in[..., ORIG_PATHS:ORIG_MERGE]], axis=-1).astype(BF16)
    uq = mla_w_uq.reshape(depth, MLA_Q_LORA, MLA_HEADS, MLA_NOPE + MLA_ROPE)
    uq = jnp.pad(uq, ((0, 0), (0, 0), (0, 0), (0, MLA_HEAD_PAD - MLA_NOPE - MLA_ROPE)))
    w_uq = uq.reshape(depth, MLA_Q_LORA, MLA_HEADS * MLA_HEAD_PAD).astype(BF16)
    ukv = mla_w_ukv.reshape(depth, MLA_KV_LORA, MLA_HEADS, MLA_NOPE + MLA_V)
    uk = jnp.pad(ukv[..., :MLA_NOPE], ((0, 0), (0, 0), (0, 0), (0, MLA_HEAD_PAD - MLA_NOPE)))
    w_uk = uk.reshape(depth, MLA_KV_LORA, MLA_HEADS * MLA_HEAD_PAD).astype(BF16)
    w_uv = ukv[..., MLA_NOPE:].reshape(depth, MLA_KV_LORA, MLA_HEADS * MLA_V).astype(BF16)
    e = np.zeros((LANES, MLA_HEADS * MLA_HEAD_PAD), np.float32)
    for h in range(MLA_HEADS):
        for i in range(MLA_ROPE):
            e[i, h * MLA_HEAD_PAD + MLA_NOPE + i] = 1.0
    w1p = jnp.pad(hy_ffn_w1, ((0, 0), (0, LANES - hy_ffn_w1.shape[1]), (0, 0)))
    return {
        "w_in": w_in_p, "conv_w": hy_conv_w, "conv_b": hy_conv_b.reshape(depth, 1, -1),
        "skip": hy_skip.reshape(depth, HY_ORDER, 1, HY_W), "w1p": w1p,
        "da_lambda": da_lambda, "da_subln": da_subln,
        "q_norm": mla_q_norm.reshape(depth, 1, -1), "kv_norm": mla_kv_norm.reshape(depth, 1, -1),
        "w_uq": w_uq, "w_uk": w_uk, "w_uv": w_uv, "e_place": jnp.asarray(e, dtype=BF16),
        "w_branch": w_branch.astype(BF16), "w_out": w_out.astype(BF16), "ln_g": ln_g, "ln_b": ln_b,
    }


def kernel(x_prompt, x_sample, c, cache_diff_k, cache_diff_v, cache_mla_ckv, cache_mla_krope, c_ctx, w_mod, b_mod, w_in, hy_conv_w, hy_conv_b, hy_ffn_w1, hy_ffn_b1, hy_ffn_w2, hy_ffn_b2, hy_ffn_w3, hy_sin_freq, hy_log_decay, hy_skip, da_lambda, da_subln, mla_q_norm, mla_w_uq, mla_kv_norm, mla_w_ukv, w_branch, w_out, ln_g, ln_b):
    depth = w_in.shape[0]
    b_ctx, n_ctx, _ = x_prompt.shape
    b_lat, n_lat, _ = x_sample.shape
    wts = _prepare_weights(w_in, hy_conv_w, hy_conv_b, hy_ffn_w1, hy_skip, da_lambda, da_subln,
                           mla_q_norm, mla_w_uq, mla_kv_norm, mla_w_ukv, w_branch, w_out, ln_g, ln_b)

    rows = -(-(1 + b_lat) // 8) * 8
    cvec = jnp.zeros((rows, D_MODEL), F32).at[0].set(c_ctx).at[1:1 + b_lat].set(c)
    mods = _modulation(cvec, w_mod, b_mod).reshape(depth, rows, 3, D_MODEL)

    spec_args = (wts["w1p"], hy_ffn_b1, hy_ffn_w2, hy_ffn_b2, hy_ffn_w3, hy_sin_freq, hy_log_decay)
    spectra_ctx = _hyena_spectra(n_ctx, *spec_args)
    spectra_lat = spectra_ctx if n_lat == n_ctx else _hyena_spectra(n_lat, *spec_args)

    xp = x_prompt
    states = []
    for l in range(depth):
        xp, new_ctx = _layer(xp, mods[l, 0:1], l, wts, spectra_ctx)
        states.append(new_ctx)
    state_k = jnp.stack([s[0] for s in states], axis=1).reshape(
        b_ctx, depth, n_ctx, 2, DA_HEADS, DA_HEAD_DIM)
    state_v = jnp.stack([s[1] for s in states], axis=1).reshape(
        b_ctx, depth, n_ctx, DA_HEADS, DA_V_DIM)
    state_ckv = jnp.stack([s[2] for s in states], axis=1)
    state_kr = jnp.stack([s[3] for s in states], axis=1)

    xs = x_sample
    ctx = (cache_diff_k, cache_diff_v, cache_mla_ckv, cache_mla_krope)
    for l in range(depth):
        xs, _ = _layer(xs, mods[l, 1:1 + b_lat], l, wts, spectra_lat, ctx=ctx)

    return (xp, xs, state_k, state_v, state_ckv, state_kr)
```

```python
import functools
import math

import numpy as np
import jax
import jax.numpy as jnp
from jax import lax
from jax.experimental import pallas as pl
from jax.experimental.pallas import tpu as pltpu

F32 = jnp.float32
BF16 = jnp.bfloat16

D_MODEL = 1024
N_BRANCH = 3
BR_W = 512
HY_W = 512
HY_ORDER = 2
HY_BANDS = 16
HY_FFN = 64
DA_HEADS = 4
DA_HEAD_DIM = 64
DA_V_DIM = 2 * DA_HEAD_DIM
MLA_HEADS = 8
MLA_NOPE = 64
MLA_ROPE = 32
MLA_V = 64
MLA_Q_LORA = 384
MLA_KV_LORA = 256
GRID_W = 64
ROPE_BASE = 10000.0
LN_EPS = 1e-5
RMS_EPS = 1e-6

LANES = 128
MLA_HEAD_PAD = LANES
VMEM_LIMIT = 56 * 1024 * 1024

COL_HY = 0
COL_Q = 3 * HY_W
COL_K = COL_Q + 2 * DA_HEADS * DA_HEAD_DIM
COL_V = COL_K + 2 * DA_HEADS * DA_HEAD_DIM
COL_SMALL = COL_V + DA_HEADS * DA_V_DIM
SMALL_W = 1024
SMALL_USED = MLA_Q_LORA + MLA_KV_LORA + MLA_ROPE
COL_MERGE = COL_SMALL + SMALL_W
COL_PATHS = COL_MERGE + N_BRANCH * D_MODEL
D_IN_P = COL_PATHS + N_BRANCH * BR_W
ORIG_PATHS = COL_SMALL + SMALL_USED
ORIG_MERGE = ORIG_PATHS + N_BRANCH * BR_W
ORIG_END = ORIG_MERGE + N_BRANCH * D_MODEL


def _cparams(sem):
    return pltpu.CompilerParams(dimension_semantics=sem, vmem_limit_bytes=VMEM_LIMIT)


def _split_bf16(x):
    hi = x.astype(BF16)
    lo = (x - hi.astype(F32)).astype(BF16)
    return hi, lo


def _dot(a, b):
    return jnp.dot(a, b, preferred_element_type=F32)


def _dot_nt(a, b):
    return lax.dot_general(a, b, (((1,), (1,)), ((), ())), preferred_element_type=F32)


def _dot3(a, b):
    ah, al = _split_bf16(a)
    bh, bl = _split_bf16(b)
    return _dot(ah, bh) + _dot(ah, bl) + _dot(al, bh)


def _dot2(m, x):
    xh, xl = _split_bf16(x)
    return _dot(m, xh) + _dot(m, xl)


def _sigmoid(x):
    return 1.0 / (1.0 + jnp.exp(-x))


@functools.lru_cache(maxsize=None)
def _dft_tables(n_tok):
    k = np.arange(n_tok, dtype=np.int64)
    ang = np.pi * ((np.outer(k, k) % (2 * n_tok)).astype(np.float64)) / n_tok
    return (jnp.asarray(np.cos(ang), dtype=BF16), jnp.asarray(np.sin(ang), dtype=BF16))


@functools.lru_cache(maxsize=None)
def _hyena_feats(n_tok):
    t = np.arange(n_tok, dtype=np.float64)
    t_lin = t / (n_tok - 1)
    bands = np.arange(1, HY_BANDS + 1, dtype=np.float64)
    ang = (2.0 * np.pi / n_tok) * t[:, None] * bands
    feats = np.concatenate([t_lin[:, None], np.cos(ang), -np.sin(ang)], axis=-1)
    out = np.zeros((n_tok, LANES), np.float64)
    out[:, :feats.shape[1]] = feats
    return jnp.asarray(out, dtype=F32)


def _rope_pattern(pos_row, pos_col, dd):
    q = dd // 4
    freqs = 1.0 / (ROPE_BASE ** (np.arange(q, dtype=np.float64) / q))
    n = pos_row.shape[0]
    cos = np.zeros((n, dd)); sa = np.zeros((n, dd)); sb = np.zeros((n, dd))
    for g, pos in enumerate((pos_row, pos_col)):
        ang = pos[:, None].astype(np.float64) * freqs
        base = g * 2 * q
        cos[:, base:base + q] = np.cos(ang); cos[:, base + q:base + 2 * q] = np.cos(ang)
        sa[:, base:base + q] = -np.sin(ang)
        sb[:, base + q:base + 2 * q] = np.sin(ang)
    return cos, sa, sb


@functools.lru_cache(maxsize=None)
def _rope_tables(n_tok):
    t = np.arange(n_tok)
    row, col = t // GRID_W, t % GRID_W
    ones = np.ones((n_tok, LANES)); zeros = np.zeros((n_tok, LANES))
    c, a, b = _rope_pattern(row, col, DA_HEAD_DIM)
    da = [np.tile(c, (1, 2)), np.tile(a, (1, 2)), np.tile(b, (1, 2))]
    c, a, b = _rope_pattern(row, col, MLA_ROPE)
    mq = [ones.copy(), zeros.copy(), zeros.copy()]
    kr = [ones.copy(), zeros.copy(), zeros.copy()]
    for dst, src in zip(mq, (c, a, b)):
        dst[:, MLA_NOPE:MLA_NOPE + MLA_ROPE] = src
    for dst, src in zip(kr, (c, a, b)):
        dst[:, :MLA_ROPE] = src
    tab = np.stack([np.stack(da), np.stack(mq), np.stack(kr)])
    return jnp.asarray(tab, dtype=F32)


def _rope_apply(x, tab_ref, kind, q):
    cos = tab_ref[kind, 0]
    sa = tab_ref[kind, 1]
    sb = tab_ref[kind, 2]
    outs = []
    for c in range(x.shape[1] // LANES):
        xc = x[:, c * LANES:(c + 1) * LANES]
        outs.append(xc * cos + pltpu.roll(xc, LANES - q, 1) * sa + pltpu.roll(xc, q, 1) * sb)
    return outs[0] if len(outs) == 1 else jnp.concatenate(outs, axis=1)


def _mod_kernel(c_ref, w_ref, b_ref, o_ref):
    c = c_ref[...]
    o_ref[0] = _dot3(c * _sigmoid(c), w_ref[0]) + b_ref[0]


def _modulation(cvec, w_mod, b_mod):
    depth = w_mod.shape[0]
    rows = cvec.shape[0]
    return pl.pallas_call(
        _mod_kernel,
        grid=(depth, 3),
        in_specs=[pl.BlockSpec((rows, D_MODEL), lambda l, j: (0, 0)),
                  pl.BlockSpec((1, D_MODEL, D_MODEL), lambda l, j: (l, 0, j)),
                  pl.BlockSpec((1, 1, D_MODEL), lambda l, j: (l, 0, j))],
        out_specs=pl.BlockSpec((1, rows, D_MODEL), lambda l, j: (l, 0, j)),
        out_shape=jax.ShapeDtypeStruct((depth, rows, 3 * D_MODEL), F32),
        compiler_params=_cparams(("arbitrary", "arbitrary")),
        name="modulation",
    )(cvec, w_mod, b_mod.reshape(depth, 1, 3 * D_MODEL))


def _filt_kernel(feats_ref, w1_ref, b1_ref, w2_ref, b2_ref, w3f_ref, w3b_ref, sf_ref,
                 ldf_ref, ldb_ref, c_ref, s_ref, kr_ref, ki_ref, kny_ref):
    n_tok = feats_ref.shape[0]
    inv_n = 1.0 / (2 * n_tok)
    feats = feats_ref[...]
    sf = sf_ref[0]
    hid = jnp.sin(sf[0:1] * (_dot3(feats, w1_ref[0]) + b1_ref[0]))
    hid = jnp.sin(sf[1:2] * (_dot3(hid, w2_ref[0]) + b2_ref[0]))
    t_lin = feats[:, 0:1]
    hf = _dot3(hid, w3f_ref[0]) * jnp.exp(-t_lin * jnp.exp(ldf_ref[0]))
    hb = _dot3(hid, w3b_ref[0]) * jnp.exp(-t_lin * jnp.exp(ldb_ref[0]))
    row = lax.broadcasted_iota(jnp.int32, hf.shape, 0)
    hb = jnp.where(row == 0, 0.0, hb)
    a = hf + hb
    d = hf - hb
    wk = jnp.where(row == 0, inv_n, 2.0 * inv_n)
    kr_ref[0, 0] = _dot2(c_ref[...], a) * wk
    ki_ref[0, 0] = -_dot2(s_ref[...], d) * wk
    sign = jnp.where((row & 1) == 0, 1.0, -1.0)
    kny_ref[0, 0] = jnp.sum(a * sign, axis=0, keepdims=True) * inv_n


def _hyena_spectra(n_tok, w1p, b1, w2, b2, w3, sf, log_decay):
    depth = w3.shape[0]
    ct = 256
    nc = HY_W // ct
    cmat, smat = _dft_tables(n_tok)
    feats = _hyena_feats(n_tok)
    const = lambda shape: pl.BlockSpec(shape, lambda l, n, c: (0,) * len(shape),
                                       pipeline_mode=pl.Buffered(1))
    per_l = lambda shape: pl.BlockSpec((1,) + shape, lambda l, n, c: (l, 0, 0))
    fcol = lambda l, n, c: (l, 0, n * nc + c)
    bcol = lambda l, n, c: (l, 0, HY_ORDER * nc + n * nc + c)
    out_spec = pl.BlockSpec((1, 1, n_tok, ct), lambda l, n, c: (l, n, 0, c))
    return pl.pallas_call(
        _filt_kernel,
        grid=(depth, HY_ORDER, nc),
        in_specs=[const((n_tok, LANES)),
                  per_l((LANES, HY_FFN)), per_l((1, HY_FFN)),
                  per_l((HY_FFN, HY_FFN)), per_l((1, HY_FFN)),
                  pl.BlockSpec((1, HY_FFN, ct), fcol), pl.BlockSpec((1, HY_FFN, ct), bcol),
                  per_l((2, HY_FFN)),
                  pl.BlockSpec((1, 1, ct), fcol), pl.BlockSpec((1, 1, ct), bcol),
                  const((n_tok, n_tok)), const((n_tok, n_tok))],
        out_specs=[out_spec, out_spec,
                   pl.BlockSpec((1, 1, 1, ct), lambda l, n, c: (l, n, 0, c))],
        out_shape=[jax.ShapeDtypeStruct((depth, HY_ORDER, n_tok, HY_W), F32),
                   jax.ShapeDtypeStruct((depth, HY_ORDER, n_tok, HY_W), F32),
                   jax.ShapeDtypeStruct((depth, HY_ORDER, 1, HY_W), F32)],
        compiler_params=_cparams(("arbitrary",) * 3),
        name="hyena_spectra",
    )(feats, w1p, b1.reshape(depth, 1, HY_FFN), w2, b2.reshape(depth, 1, HY_FFN), w3, w3, sf,
      log_decay.reshape(depth, 1, -1), log_decay.reshape(depth, 1, -1), cmat, smat)


def _inproj_kernel(x_ref, mod_ref, w_ref, o_ref):
    shift = mod_ref[0, 0:1, :]
    scale = mod_ref[0, 1:2, :]
    h = (x_ref[...] * (1.0 + scale) + shift).astype(BF16)
    o_ref[...] = _dot(h, w_ref[0])


def _inproj(x2d, mod, w_in_p, l, seq_len):
    t = x2d.shape[0]
    per_batch = mod.shape[0] > 1
    tm = min(512, seq_len if per_batch else t)
    tn = D_IN_P // 4
    tiles_per_batch = seq_len // tm
    mod_idx = (lambda j, i: (i // tiles_per_batch, 0, 0)) if per_batch else (lambda j, i: (0, 0, 0))
    return pl.pallas_call(
        _inproj_kernel,
        grid=(D_IN_P // tn, t // tm),
        in_specs=[pl.BlockSpec((tm, D_MODEL), lambda j, i: (i, 0)),
                  pl.BlockSpec((1, 3, D_MODEL), mod_idx),
                  pl.BlockSpec((1, D_MODEL, tn), lambda j, i: (l, 0, j))],
        out_specs=pl.BlockSpec((tm, tn), lambda j, i: (i, j)),
        out_shape=jax.ShapeDtypeStruct((t, D_IN_P), F32),
        compiler_params=_cparams(("arbitrary", "arbitrary")),
        name="inproj",
    )(x2d, mod, w_in_p)


def _short_conv(x, w_ref, b_ref):
    n_tok = x.shape[0]
    row = lax.broadcasted_iota(jnp.int32, x.shape, 0)
    prev = jnp.where(row == 0, 0.0, pltpu.roll(x, 1, 0))
    nxt = jnp.where(row == n_tok - 1, 0.0, pltpu.roll(x, n_tok - 1, 0))
    return prev * w_ref[0:1, :] + x * w_ref[1:2, :] + nxt * w_ref[2:3, :] + b_ref[...]


def _hyena_kernel(*refs, conv_z, tk):
    if conv_z:
        (zin_ref, gin_ref, wz_ref, bz_ref, wg_ref, bg_ref, kr_ref, ki_ref, kny_ref, skip_ref,
         c_ref, s_ref, o_ref, zb_ref, acc_ref) = refs
    else:
        (zin_ref, gin_ref, wg_ref, bg_ref, kr_ref, ki_ref, kny_ref, skip_ref,
         c_ref, s_ref, o_ref, zb_ref, acc_ref) = refs
    nb, n_tok, ct = zin_ref.shape
    for i in range(nb):
        z = zin_ref[i]
        if conv_z:
            z = _short_conv(z, wz_ref, bz_ref)
        zb_ref[...] = z.astype(BF16)
        for j in range(n_tok // tk):
            rows = slice(j * tk, (j + 1) * tk)
            zr = _dot(c_ref[rows, :], zb_ref[...])
            zs = _dot(s_ref[rows, :], zb_ref[...])
            kr = kr_ref[0, 0, rows, :]
            ki = ki_ref[0, 0, rows, :]
            yr = (zr * kr + zs * ki).astype(BF16)
            ym = (zs * kr - zr * ki).astype(BF16)
            part = _dot(c_ref[:, rows], yr) + _dot(s_ref[:, rows], ym)
            if j == 0:
                acc_ref[...] = part
            else:
                acc_ref[...] += part
        row = lax.broadcasted_iota(jnp.int32, z.shape, 0)
        sign = jnp.where((row & 1) == 0, 1.0, -1.0)
        zny = jnp.sum(z * sign, axis=0, keepdims=True)
        y = acc_ref[...] + sign * (zny * kny_ref[0, 0]) + z * skip_ref[0, 0]
        o_ref[i] = _short_conv(gin_ref[i], wg_ref, bg_ref) * y


def _hyena_order(zin, zblk, gin, gblk, conv_w, conv_b, kr, ki, kny, skip, l, order, conv_z):
    b, n_tok = zin.shape[0], zin.shape[1]
    cmat, smat = _dft_tables(n_tok)
    if n_tok >= 2048:
        ct, nb = 256, 1
    else:
        ct, nb = 512, min(8, b)
    nc = HY_W // ct
    tk = min(256, n_tok)
    const = lambda shape: pl.BlockSpec(shape, lambda c, i: (0,) * len(shape),
                                       pipeline_mode=pl.Buffered(1))
    zspec = pl.BlockSpec((nb, n_tok, ct), lambda c, i: (i, 0, zblk * nc + c))
    gspec = pl.BlockSpec((nb, n_tok, ct), lambda c, i: (i, 0, gblk * nc + c))
    wspec = lambda blk: pl.BlockSpec((1, 3, ct), lambda c, i: (l, 0, blk * nc + c))
    bspec = lambda blk: pl.BlockSpec((1, 1, ct), lambda c, i: (l, 0, blk * nc + c))
    kspec = pl.BlockSpec((1, 1, n_tok, ct), lambda c, i: (l, order, 0, c))
    in_specs = [zspec, gspec]
    args = [zin, gin]
    if conv_z:
        in_specs += [wspec(0), bspec(0)]
        args += [conv_w, conv_b]
    in_specs += [wspec(order + 1), bspec(order + 1), kspec, kspec,
                 pl.BlockSpec((1, 1, 1, ct), lambda c, i: (l, order, 0, c)),
                 pl.BlockSpec((1, 1, 1, ct), lambda c, i: (l, order, 0, c)),
                 const((n_tok, n_tok)), const((n_tok, n_tok))]
    args += [conv_w, conv_b, kr, ki, kny, skip, cmat, smat]

    def body(*refs):
        refs = list(refs)
        lo = 2
        n_small = 4 if conv_z else 2
        for k in range(lo, lo + n_small):
            refs[k] = refs[k].at[0]
        _hyena_kernel(*refs, conv_z=conv_z, tk=tk)

    return pl.pallas_call(
        body,
        grid=(nc, b // nb),
        in_specs=in_specs,
        out_specs=pl.BlockSpec((nb, n_tok, ct), lambda c, i: (i, 0, c)),
        out_shape=jax.ShapeDtypeStruct((b, n_tok, HY_W), F32),
        scratch_shapes=[pltpu.VMEM((n_tok, ct), BF16), pltpu.VMEM((n_tok, ct), F32)],
        compiler_params=_cparams(("arbitrary", "arbitrary")),
        name=f"hyena_order{order}",
    )(*args)


def _rms(x, g, n):
    ms = jnp.sum(x * x, axis=-1, keepdims=True) * (1.0 / n)
    return x * lax.rsqrt(ms + RMS_EPS) * g


def _da_values_ext(v):
    lane = lax.broadcasted_iota(jnp.int32, (v.shape[0], LANES), 1)
    ones_col = jnp.where(lane == 0, 1.0, 0.0).astype(BF16)
    pieces = []
    for h in range(DA_HEADS):
        pieces += [v[:, h * DA_V_DIM:(h + 1) * DA_V_DIM], ones_col]
    return jnp.concatenate(pieces, axis=1)


def _mla_values_ext(ckv_b, wv):
    v = _dot(ckv_b, wv)
    lane = lax.broadcasted_iota(jnp.int32, v.shape, 1)
    return jnp.where((lane & (MLA_HEAD_PAD - 1)) == MLA_V, 1.0, v).astype(BF16)


def _prep_kernel(*refs, rope):
    if rope:
        (q_ref, k_ref, v_ref, sm_ref, qn_ref, wuq_ref, kvn_ref, wk_ref, wv_ref, e_ref, tab_ref,
         qd_ref, kd_ref, vd_ref, qm_ref, km_ref, vm_ref, ckv_ref) = refs
    else:
        (q_ref, k_ref, v_ref, sm_ref, qn_ref, wuq_ref, kvn_ref, wk_ref, wv_ref, e_ref,
         qd_ref, kd_ref, vd_ref, qm_ref, km_ref, vm_ref, ckv_ref) = refs
    q = q_ref[...] * (DA_HEAD_DIM ** -0.5)
    k = k_ref[...]
    if rope:
        q = _rope_apply(q, tab_ref, 0, DA_HEAD_DIM // 4)
        k = _rope_apply(k, tab_ref, 0, DA_HEAD_DIM // 4)
    qd_ref[...] = q.astype(BF16)
    kd_ref[...] = k.astype(BF16)
    vd_ref[...] = _da_values_ext(v_ref[...].astype(BF16))

    dq = sm_ref[:, 0:MLA_Q_LORA]
    dkv = sm_ref[:, MLA_Q_LORA:MLA_Q_LORA + MLA_KV_LORA]
    kr = sm_ref[:, 5 * LANES:6 * LANES]
    c_q = _rms(dq, qn_ref[0], MLA_Q_LORA)
    qm = _dot(c_q.astype(BF16), wuq_ref[0]) * ((MLA_NOPE + MLA_ROPE) ** -0.5)
    c_kv = _rms(dkv, kvn_ref[0], MLA_KV_LORA)
    ckv_ref[...] = c_kv
    ckv_b = c_kv.astype(BF16)
    if rope:
        qm = _rope_apply(qm, tab_ref, 1, MLA_ROPE // 4)
        kr = _rope_apply(kr, tab_ref, 2, MLA_ROPE // 4)
    qm_ref[...] = qm.astype(BF16)
    km_ref[...] = (_dot(ckv_b, wk_ref[0]) + _dot(kr.astype(BF16), e_ref[...])).astype(BF16)
    vm_ref[...] = _mla_values_ext(ckv_b, wv_ref[0])


def _prep(proj, wts, l, seq_len, rope):
    t = proj.shape[0]
    tm = min(512, seq_len if rope else t)
    tiles_per_seq = seq_len // tm
    blk = lambda w, idx: pl.BlockSpec((tm, w), lambda i: (i, idx))
    wl = lambda shape: pl.BlockSpec((1,) + shape, lambda i: (l, 0, 0))
    in_specs = [blk(512, COL_Q // 512), blk(512, COL_K // 512), blk(512, COL_V // 512),
                blk(SMALL_W, COL_SMALL // SMALL_W),
                wl((1, MLA_Q_LORA)), wl((MLA_Q_LORA, MLA_HEADS * MLA_HEAD_PAD)),
                wl((1, MLA_KV_LORA)), wl((MLA_KV_LORA, MLA_HEADS * MLA_HEAD_PAD)),
                wl((MLA_KV_LORA, MLA_HEADS * MLA_HEAD_PAD)),
                pl.BlockSpec((LANES, MLA_HEADS * MLA_HEAD_PAD), lambda i: (0, 0))]
    args = [proj, proj, proj, proj, wts["q_norm"], wts["w_uq"], wts["kv_norm"], wts["w_uk"],
            wts["w_uv"], wts["e_place"]]
    if rope:
        in_specs.append(pl.BlockSpec((3, 3, tm, LANES), lambda i: (0, 0, i % tiles_per_seq, 0)))
        args.append(_rope_tables(seq_len))
    out = lambda w: pl.BlockSpec((tm, w), lambda i: (i, 0))
    return pl.pallas_call(
        functools.partial(_prep_kernel, rope=rope),
        grid=(t // tm,),
        in_specs=in_specs,
        out_specs=[out(512), out(512), out(1024), out(1024), out(1024), out(1024), out(MLA_KV_LORA)],
        out_shape=[jax.ShapeDtypeStruct((t, 512), BF16), jax.ShapeDtypeStruct((t, 512), BF16),
                   jax.ShapeDtypeStruct((t, 1024), BF16), jax.ShapeDtypeStruct((t, 1024), BF16),
                   jax.ShapeDtypeStruct((t, 1024), BF16), jax.ShapeDtypeStruct((t, 1024), BF16),
                   jax.ShapeDtypeStruct((t, MLA_KV_LORA), F32)],
        compiler_params=_cparams(("arbitrary",)),
        name="attn_prep",
    )(*args)


def _prep_cache_kernel(k_ref, v_ref, ckv_ref, kr_ref, wk_ref, wv_ref, e_ref,
                       kd_ref, vd_ref, km_ref, vm_ref):
    kd_ref[0] = k_ref[0, 0].astype(BF16)
    vd_ref[0] = _da_values_ext(v_ref[0, 0].astype(BF16))
    ckv_b = ckv_ref[0, 0].astype(BF16)
    km_ref[0] = (_dot(ckv_b, wk_ref[0]) + _dot(kr_ref[0, 0].astype(BF16), e_ref[...])).astype(BF16)
    vm_ref[0] = _mla_values_ext(ckv_b, wv_ref[0])


def _prep_cache(cache_k, cache_v, cache_ckv, cache_kr, wts, l):
    b, _, p = cache_k.shape[:3]
    ck = cache_k.reshape(b, -1, p, 512)
    cv = cache_v.reshape(b, -1, p, 512)
    cin = lambda w: pl.BlockSpec((1, 1, p, w), lambda i: (i, l, 0, 0))
    wl = lambda shape: pl.BlockSpec((1,) + shape, lambda i: (l, 0, 0))
    out = lambda w: pl.BlockSpec((1, p, w), lambda i: (i, 0, 0))
    return pl.pallas_call(
        _prep_cache_kernel,
        grid=(b,),
        in_specs=[cin(512), cin(512), cin(MLA_KV_LORA), cin(MLA_ROPE),
                  wl((MLA_KV_LORA, MLA_HEADS * MLA_HEAD_PAD)), wl((MLA_KV_LORA, MLA_HEADS * MLA_HEAD_PAD)),
                  pl.BlockSpec((MLA_ROPE, MLA_HEADS * MLA_HEAD_PAD), lambda i: (0, 0))],
        out_specs=[out(512), out(1024), out(1024), out(1024)],
        out_shape=[jax.ShapeDtypeStruct((b, p, 512), BF16), jax.ShapeDtypeStruct((b, p, 1024), BF16),
                   jax.ShapeDtypeStruct((b, p, 1024), BF16), jax.ShapeDtypeStruct((b, p, 1024), BF16)],
        compiler_params=_cparams(("arbitrary",)),
        name="attn_prep_cache",
    )(ck, cv, cache_ckv, cache_kr, wts["w_uk"], wts["w_uv"], wts["e_place"][:MLA_ROPE])


def _softmax_pv(q, k_parts, v_parts, dv):
    s = [_dot_nt(q, k) for k in k_parts]
    m = s[0].max(axis=-1, keepdims=True)
    for sp in s[1:]:
        m = jnp.maximum(m, sp.max(axis=-1, keepdims=True))
    o = None
    for sp, v in zip(s, v_parts):
        pv = _dot(jnp.exp((sp - m).astype(BF16)), v)
        o = pv if o is None else o + pv
    return o[:, :dv] / o[:, dv:dv + 1]


def _da_kernel(*refs, lam_init, has_ctx):
    if has_ctx:
        q_ref, kn_ref, vn_ref, kc_ref, vc_ref, lam_ref, g_ref, o_ref = refs
    else:
        q_ref, kn_ref, vn_ref, lam_ref, g_ref, o_ref = refs
    lp = lam_ref[0]
    lam = (jnp.exp(jnp.sum(lp[0:1] * lp[1:2], axis=1, keepdims=True))
           - jnp.exp(jnp.sum(lp[2:3] * lp[3:4], axis=1, keepdims=True)) + lam_init)
    for h in range(DA_HEADS):
        vcols = slice(h * 2 * DA_V_DIM, (h + 1) * 2 * DA_V_DIM)
        outs = []
        for j in range(2):
            cols = slice((j * DA_HEADS + h) * DA_HEAD_DIM, (j * DA_HEADS + h + 1) * DA_HEAD_DIM)
            k_parts = [kn_ref[0, :, cols]]
            v_parts = [vn_ref[0, :, vcols]]
            if has_ctx:
                k_parts.insert(0, kc_ref[0, :, cols])
                v_parts.insert(0, vc_ref[0, :, vcols])
            outs.append(_softmax_pv(q_ref[0, :, cols], k_parts, v_parts, DA_V_DIM))
        o = outs[0] - lam * outs[1]
        o_ref[0, :, h * DA_V_DIM:(h + 1) * DA_V_DIM] = _rms(o, g_ref[0], DA_V_DIM) * (1.0 - lam_init)


def _da_attention(qd, kd, vd, ctx, da_lambda, da_subln, l, tq):
    b, n, _ = qd.shape
    has_ctx = ctx is not None
    lam_init = 0.8 - 0.6 * math.exp(-0.3 * l)
    full = lambda a: pl.BlockSpec((1,) + a.shape[1:], lambda i, j: (i, 0, 0))
    in_specs = [pl.BlockSpec((1, tq, 512), lambda i, j: (i, j, 0)), full(kd), full(vd)]
    args = [qd, kd, vd]
    if has_ctx:
        in_specs += [full(ctx[0]), full(ctx[1])]
        args += [ctx[0], ctx[1]]
    in_specs += [pl.BlockSpec((1, 4, DA_HEAD_DIM), lambda i, j: (l, 0, 0)),
                 pl.BlockSpec((1, 1, DA_V_DIM), lambda i, j: (l, 0, 0))]
    args += [da_lambda, da_subln.reshape(-1, 1, DA_V_DIM)]
    return pl.pallas_call(
        functools.partial(_da_kernel, lam_init=lam_init, has_ctx=has_ctx),
        grid=(b, n // tq),
        in_specs=in_specs,
        out_specs=pl.BlockSpec((1, tq, 512), lambda i, j: (i, j, 0)),
        out_shape=jax.ShapeDtypeStruct((b, n, 512), F32),
        compiler_params=_cparams(("arbitrary", "arbitrary")),
        name="diff_attention",
    )(*args)


def _mla_kernel(*refs, has_ctx):
    if has_ctx:
        q_ref, kn_ref, vn_ref, kc_ref, vc_ref, o_ref = refs
    else:
        q_ref, kn_ref, vn_ref, o_ref = refs
    for h in range(MLA_HEADS):
        cols = slice(h * MLA_HEAD_PAD, (h + 1) * MLA_HEAD_PAD)
        k_parts = [kn_ref[0, :, cols]]
        v_parts = [vn_ref[0, :, cols]]
        if has_ctx:
            k_parts.insert(0, kc_ref[0, :, cols])
            v_parts.insert(0, vc_ref[0, :, cols])
        o_ref[0, :, h * MLA_V:(h + 1) * MLA_V] = _softmax_pv(q_ref[0, :, cols], k_parts, v_parts, MLA_V)


def _mla_attention(qm, km, vm, ctx, tq):
    b, n, _ = qm.shape
    has_ctx = ctx is not None
    full = lambda a: pl.BlockSpec((1,) + a.shape[1:], lambda i, j: (i, 0, 0))
    in_specs = [pl.BlockSpec((1, tq, 1024), lambda i, j: (i, j, 0)), full(km), full(vm)]
    args = [qm, km, vm]
    if has_ctx:
        in_specs += [full(ctx[0]), full(ctx[1])]
        args += [ctx[0], ctx[1]]
    return pl.pallas_call(
        functools.partial(_mla_kernel, has_ctx=has_ctx),
        grid=(b, n // tq),
        in_specs=in_specs,
        out_specs=pl.BlockSpec((1, tq, 512), lambda i, j: (i, j, 0)),
        out_shape=jax.ShapeDtypeStruct((b, n, 512), F32),
        compiler_params=_cparams(("arbitrary", "arbitrary")),
        name="latent_attention",
    )(*args)


def _outproj_kernel(x_ref, mod_ref, yh_ref, yd_ref, ym_ref, p0_ref, p1_ref, p2_ref,
                    m0_ref, m1_ref, m2_ref, wb_ref, wo_ref, g_ref, b_ref, o_ref, *, alpha):
    merged = None
    for y_ref, p_ref, m_ref, n in ((yh_ref, p0_ref, m0_ref, 0), (yd_ref, p1_ref, m1_ref, 1),
                                   (ym_ref, p2_ref, m2_ref, 2)):
        p = p_ref[...]
        br = (y_ref[...] * (p * _sigmoid(p))).astype(BF16)
        term = _sigmoid(m_ref[...]) * _dot(br, wb_ref[0, n])
        merged = term if merged is None else merged + term
    out = _dot(merged.astype(BF16), wo_ref[0])
    gate = mod_ref[0, 2:3, :]
    y = alpha * x_ref[...] + gate * out
    mu = jnp.mean(y, axis=-1, keepdims=True)
    yc = y - mu
    var = jnp.mean(yc * yc, axis=-1, keepdims=True)
    o_ref[...] = yc * lax.rsqrt(var + LN_EPS) * g_ref[0] + b_ref[0]


def _outproj(x2d, mod, y_hy, y_da, y_mla, proj, w_branch, w_out, ln_g, ln_b, l, seq_len, alpha):
    t = x2d.shape[0]
    per_batch = mod.shape[0] > 1
    tm = min(256, seq_len if per_batch else t)
    tiles_per_batch = seq_len // tm
    mod_idx = (lambda i: (i // tiles_per_batch, 0, 0)) if per_batch else (lambda i: (0, 0, 0))
    row = lambda w, idx=0: pl.BlockSpec((tm, w), lambda i: (i, idx))
    depth = ln_g.shape[0]
    return pl.pallas_call(
        functools.partial(_outproj_kernel, alpha=alpha),
        grid=(t // tm,),
        in_specs=[row(D_MODEL), pl.BlockSpec((1, 3, D_MODEL), mod_idx),
                  row(BR_W), row(BR_W), row(BR_W),
                  row(BR_W, COL_PATHS // BR_W), row(BR_W, COL_PATHS // BR_W + 1),
                  row(BR_W, COL_PATHS // BR_W + 2),
                  row(D_MODEL, COL_MERGE // D_MODEL), row(D_MODEL, COL_MERGE // D_MODEL + 1),
                  row(D_MODEL, COL_MERGE // D_MODEL + 2),
                  pl.BlockSpec((1, N_BRANCH, BR_W, D_MODEL), lambda i: (l, 0, 0, 0)),
                  pl.BlockSpec((1, D_MODEL, D_MODEL), lambda i: (l, 0, 0)),
                  pl.BlockSpec((1, 1, D_MODEL), lambda i: (l, 0, 0)),
                  pl.BlockSpec((1, 1, D_MODEL), lambda i: (l, 0, 0))],
        out_specs=row(D_MODEL),
        out_shape=jax.ShapeDtypeStruct((t, D_MODEL), F32),
        compiler_params=_cparams(("arbitrary",)),
        name="outproj",
    )(x2d, mod, y_hy, y_da, y_mla, proj, proj, proj, proj, proj, proj, w_branch, w_out,
      ln_g.reshape(depth, 1, D_MODEL), ln_b.reshape(depth, 1, D_MODEL))


def _layer(x, mod, l, wts, spectra, ctx=None):
    b, n, _ = x.shape
    depth = wts["ln_g"].shape[0]
    alpha = (2 * depth) ** 0.25
    x2d = x.reshape(b * n, D_MODEL)
    proj = _inproj(x2d, mod, wts["w_in"], l, n)
    proj3 = proj.reshape(b, n, D_IN_P)

    kr_s, ki_s, kny_s = spectra
    z2 = _hyena_order(proj3, 0, proj3, 1, wts["conv_w"], wts["conv_b"], kr_s, ki_s, kny_s,
                      wts["skip"], l, 0, True)
    y_hy = _hyena_order(z2, 0, proj3, 2, wts["conv_w"], wts["conv_b"], kr_s, ki_s, kny_s,
                        wts["skip"], l, 1, False)

    qd, kd, vd, qm, km, vm, ckv = _prep(proj, wts, l, n, rope=ctx is not None)
    r3 = lambda a: a.reshape(b, n, a.shape[-1])
    qd, kd, vd, qm, km, vm = map(r3, (qd, kd, vd, qm, km, vm))
    if ctx is not None:
        kd_c, vd_c, km_c, vm_c = _prep_cache(*ctx, wts, l)
        da_ctx, mla_ctx = (kd_c, vd_c), (km_c, vm_c)
    else:
        da_ctx = mla_ctx = None
    tq = min(256, n)
    y_da = _da_attention(qd, kd, vd, da_ctx, wts["da_lambda"], wts["da_subln"], l, tq)
    y_mla = _mla_attention(qm, km, vm, mla_ctx, tq)

    x_new = _outproj(x2d, mod, y_hy.reshape(b * n, HY_W), y_da.reshape(b * n, BR_W),
                     y_mla.reshape(b * n, BR_W), proj, wts["w_branch"], wts["w_out"],
                     wts["ln_g"], wts["ln_b"], l, n, alpha)
    new_ctx = (proj3[:, :, COL_K:COL_V], proj3[:, :, COL_V:COL_SMALL], ckv.reshape(b, n, MLA_KV_LORA),
               proj3[:, :, COL_SMALL + MLA_Q_LORA + MLA_KV_LORA:COL_SMALL + SMALL_USED])
    return x_new.reshape(b, n, D_MODEL), new_ctx


def _prepare_weights(w_in, hy_conv_w, hy_conv_b, hy_ffn_w1, hy_skip, da_lambda, da_subln,
                     mla_q_norm, mla_w_uq, mla_kv_norm, mla_w_ukv, w_branch, w_out, ln_g, ln_b):
    depth = w_in.shape[0]
    zpad = jnp.zeros((depth, D_MODEL, SMALL_W - SMALL_USED), w_in.dtype)
    w_in_p = jnp.concatenate([w_in[..., :ORIG_PATHS], zpad, w_in[..., ORIG_MERGE:ORIG_END],
                              w_in[..., ORIG_PATHS:ORIG_MERGE]], axis=-1).astype(BF16)
    uq = mla_w_uq.reshape(depth, MLA_Q_LORA, MLA_HEADS, MLA_NOPE + MLA_ROPE)
    uq = jnp.pad(uq, ((0, 0), (0, 0), (0, 0), (0, MLA_HEAD_PAD - MLA_NOPE - MLA_ROPE)))
    w_uq = uq.reshape(depth, MLA_Q_LORA, MLA_HEADS * MLA_HEAD_PAD).astype(BF16)
    ukv = mla_w_ukv.reshape(depth, MLA_KV_LORA, MLA_HEADS, MLA_NOPE + MLA_V)
    uk = jnp.pad(ukv[..., :MLA_NOPE], ((0, 0), (0, 0), (0, 0), (0, MLA_HEAD_PAD - MLA_NOPE)))
    w_uk = uk.reshape(depth, MLA_KV_LORA, MLA_HEADS * MLA_HEAD_PAD).astype(BF16)
    uv = jnp.pad(ukv[..., MLA_NOPE:], ((0, 0), (0, 0), (0, 0), (0, MLA_HEAD_PAD - MLA_V)))
    w_uv = uv.reshape(depth, MLA_KV_LORA, MLA_HEADS * MLA_HEAD_PAD).astype(BF16)
    e = np.zeros((LANES, MLA_HEADS * MLA_HEAD_PAD), np.float32)
    for h in range(MLA_HEADS):
        for i in range(MLA_ROPE):
            e[i, h * MLA_HEAD_PAD + MLA_NOPE + i] = 1.0
    w1p = jnp.pad(hy_ffn_w1, ((0, 0), (0, LANES - hy_ffn_w1.shape[1]), (0, 0)))
    return {
        "w_in": w_in_p, "conv_w": hy_conv_w, "conv_b": hy_conv_b.reshape(depth, 1, -1),
        "skip": hy_skip.reshape(depth, HY_ORDER, 1, HY_W), "w1p": w1p,
        "da_lambda": da_lambda, "da_subln": da_subln,
        "q_norm": mla_q_norm.reshape(depth, 1, -1), "kv_norm": mla_kv_norm.reshape(depth, 1, -1),
        "w_uq": w_uq, "w_uk": w_uk, "w_uv": w_uv, "e_place": jnp.asarray(e, dtype=BF16),
        "w_branch": w_branch.astype(BF16), "w_out": w_out.astype(BF16), "ln_g": ln_g, "ln_b": ln_b,
    }


def kernel(x_prompt, x_sample, c, cache_diff_k, cache_diff_v, cache_mla_ckv, cache_mla_krope, c_ctx, w_mod, b_mod, w_in, hy_conv_w, hy_conv_b, hy_ffn_w1, hy_ffn_b1, hy_ffn_w2, hy_ffn_b2, hy_ffn_w3, hy_sin_freq, hy_log_decay, hy_skip, da_lambda, da_subln, mla_q_norm, mla_w_uq, mla_kv_norm, mla_w_ukv, w_branch, w_out, ln_g, ln_b):
    depth = w_in.shape[0]
    b_ctx, n_ctx, _ = x_prompt.shape
    b_lat, n_lat, _ = x_sample.shape
    wts = _prepare_weights(w_in, hy_conv_w, hy_conv_b, hy_ffn_w1, hy_skip, da_lambda, da_subln,
                           mla_q_norm, mla_w_uq, mla_kv_norm, mla_w_ukv, w_branch, w_out, ln_g, ln_b)

    rows = -(-(1 + b_lat) // 8) * 8
    cvec = jnp.zeros((rows, D_MODEL), F32).at[0].set(c_ctx).at[1:1 + b_lat].set(c)
    mods = _modulation(cvec, w_mod, b_mod).reshape(depth, rows, 3, D_MODEL)

    spec_args = (wts["w1p"], hy_ffn_b1, hy_ffn_w2, hy_ffn_b2, hy_ffn_w3, hy_sin_freq, hy_log_decay)
    spectra_ctx = _hyena_spectra(n_ctx, *spec_args)
    spectra_lat = spectra_ctx if n_lat == n_ctx else _hyena_spectra(n_lat, *spec_args)

    xp = x_prompt
    states = []
    for l in range(depth):
        xp, new_ctx = _layer(xp, mods[l, 0:1], l, wts, spectra_ctx)
        states.append(new_ctx)
    state_k = jnp.stack([s[0] for s in states], axis=1).reshape(
        b_ctx, depth, n_ctx, 2, DA_HEADS, DA_HEAD_DIM)
    state_v = jnp.stack([s[1] for s in states], axis=1).reshape(
        b_ctx, depth, n_ctx, DA_HEADS, DA_V_DIM)
    state_ckv = jnp.stack([s[2] for s in states], axis=1)
    state_kr = jnp.stack([s[3] for s in states], axis=1)

    xs = x_sample
    ctx = (cache_diff_k, cache_diff_v, cache_mla_ckv, cache_mla_krope)
    for l in range(depth):
        xs, _ = _layer(xs, mods[l, 1:1 + b_lat], l, wts, spectra_lat, ctx=ctx)

    return (xp, xs, state_k, state_v, state_ckv, state_kr)
```

```python
import functools
import math

import numpy as np
import jax
import jax.numpy as jnp
from jax import lax
from jax.experimental import pallas as pl
from jax.experimental.pallas import tpu as pltpu

F32 = jnp.float32
BF16 = jnp.bfloat16

D_MODEL = 1024
N_BRANCH = 3
BR_W = 512
HY_W = 512
HY_ORDER = 2
HY_BANDS = 16
HY_FFN = 64
DA_HEADS = 4
DA_HEAD_DIM = 64
DA_V_DIM = 2 * DA_HEAD_DIM
MLA_HEADS = 8
MLA_NOPE = 64
MLA_ROPE = 32
MLA_V = 64
MLA_Q_LORA = 384
MLA_KV_LORA = 256
GRID_W = 64
ROPE_BASE = 10000.0
LN_EPS = 1e-5
RMS_EPS = 1e-6

LANES = 128
MLA_HEAD_PAD = LANES
VMEM_LIMIT = 56 * 1024 * 1024

COL_HY = 0
COL_Q = 3 * HY_W
COL_K = COL_Q + 2 * DA_HEADS * DA_HEAD_DIM
COL_V = COL_K + 2 * DA_HEADS * DA_HEAD_DIM
COL_SMALL = COL_V + DA_HEADS * DA_V_DIM
SMALL_W = 1024
SMALL_USED = MLA_Q_LORA + MLA_KV_LORA + MLA_ROPE
D_IN_A = COL_SMALL + SMALL_W
GATE_MERGE = 0
GATE_PATHS = N_BRANCH * D_MODEL
D_IN_G = GATE_PATHS + N_BRANCH * BR_W
ORIG_PATHS = COL_SMALL + SMALL_USED
ORIG_MERGE = ORIG_PATHS + N_BRANCH * BR_W
ORIG_END = ORIG_MERGE + N_BRANCH * D_MODEL


def _cparams(sem):
    return pltpu.CompilerParams(dimension_semantics=sem, vmem_limit_bytes=VMEM_LIMIT)


def _split_bf16(x):
    hi = x.astype(BF16)
    lo = (x - hi.astype(F32)).astype(BF16)
    return hi, lo


def _dot(a, b):
    return jnp.dot(a, b, preferred_element_type=F32)


def _dot_nt(a, b):
    return lax.dot_general(a, b, (((1,), (1,)), ((), ())), preferred_element_type=F32)


def _dot3(a, b):
    ah, al = _split_bf16(a)
    bh, bl = _split_bf16(b)
    return _dot(ah, bh) + _dot(ah, bl) + _dot(al, bh)


def _sigmoid(x):
    return 1.0 / (1.0 + jnp.exp(-x))


@functools.lru_cache(maxsize=None)
def _dft_tables(n_tok):
    k = np.arange(n_tok, dtype=np.int64)
    ang = np.pi * ((np.outer(k, k) % (2 * n_tok)).astype(np.float64)) / n_tok
    return (jnp.asarray(np.cos(ang), dtype=BF16), jnp.asarray(np.sin(ang), dtype=BF16))


@functools.lru_cache(maxsize=None)
def _hyena_feats(n_tok):
    t = np.arange(n_tok, dtype=np.float64)
    t_lin = t / (n_tok - 1)
    bands = np.arange(1, HY_BANDS + 1, dtype=np.float64)
    ang = (2.0 * np.pi / n_tok) * t[:, None] * bands
    feats = np.concatenate([t_lin[:, None], np.cos(ang), -np.sin(ang)], axis=-1)
    out = np.zeros((n_tok, LANES), np.float64)
    out[:, :feats.shape[1]] = feats
    return jnp.asarray(out, dtype=F32)


def _rope_pattern(pos_row, pos_col, dd):
    q = dd // 4
    freqs = 1.0 / (ROPE_BASE ** (np.arange(q, dtype=np.float64) / q))
    n = pos_row.shape[0]
    cos = np.zeros((n, dd)); sa = np.zeros((n, dd)); sb = np.zeros((n, dd))
    for g, pos in enumerate((pos_row, pos_col)):
        ang = pos[:, None].astype(np.float64) * freqs
        base = g * 2 * q
        cos[:, base:base + q] = np.cos(ang); cos[:, base + q:base + 2 * q] = np.cos(ang)
        sa[:, base:base + q] = -np.sin(ang)
        sb[:, base + q:base + 2 * q] = np.sin(ang)
    return cos, sa, sb


@functools.lru_cache(maxsize=None)
def _rope_tables(n_tok):
    t = np.arange(n_tok)
    row, col = t // GRID_W, t % GRID_W
    ones = np.ones((n_tok, LANES)); zeros = np.zeros((n_tok, LANES))
    c, a, b = _rope_pattern(row, col, DA_HEAD_DIM)
    da = [np.tile(c, (1, 2)), np.tile(a, (1, 2)), np.tile(b, (1, 2))]
    c, a, b = _rope_pattern(row, col, MLA_ROPE)
    mq = [ones.copy(), zeros.copy(), zeros.copy()]
    kr = [ones.copy(), zeros.copy(), zeros.copy()]
    for dst, src in zip(mq, (c, a, b)):
        dst[:, MLA_NOPE:MLA_NOPE + MLA_ROPE] = src
    for dst, src in zip(kr, (c, a, b)):
        dst[:, :MLA_ROPE] = src
    tab = np.stack([np.stack(da), np.stack(mq), np.stack(kr)])
    return jnp.asarray(tab, dtype=F32)


def _rope_apply(x, tab_ref, kind, q):
    cos = tab_ref[kind, 0]
    sa = tab_ref[kind, 1]
    sb = tab_ref[kind, 2]
    outs = []
    for c in range(x.shape[1] // LANES):
        xc = x[:, c * LANES:(c + 1) * LANES]
        outs.append(xc * cos + pltpu.roll(xc, LANES - q, 1) * sa + pltpu.roll(xc, q, 1) * sb)
    return outs[0] if len(outs) == 1 else jnp.concatenate(outs, axis=1)


def _mod_kernel(c_ref, w_ref, b_ref, o_ref):
    c = c_ref[...]
    o_ref[0] = _dot3(c * _sigmoid(c), w_ref[0]) + b_ref[0]


def _modulation(cvec, w_mod, b_mod):
    depth = w_mod.shape[0]
    rows = cvec.shape[0]
    return pl.pallas_call(
        _mod_kernel,
        grid=(depth, 3),
        in_specs=[pl.BlockSpec((rows, D_MODEL), lambda l, j: (0, 0)),
                  pl.BlockSpec((1, D_MODEL, D_MODEL), lambda l, j: (l, 0, j)),
                  pl.BlockSpec((1, 1, D_MODEL), lambda l, j: (l, 0, j))],
        out_specs=pl.BlockSpec((1, rows, D_MODEL), lambda l, j: (l, 0, j)),
        out_shape=jax.ShapeDtypeStruct((depth, rows, 3 * D_MODEL), F32),
        compiler_params=_cparams(("arbitrary", "arbitrary")),
        name="modulation",
    )(cvec, w_mod, b_mod.reshape(depth, 1, 3 * D_MODEL))


def _filt_kernel(feats_ref, w1_ref, b1_ref, w2_ref, b2_ref, w3f_ref, w3b_ref, sf_ref,
                 ldf_ref, ldb_ref, c_ref, s_ref, kr_ref, ki_ref, kny_ref, hid_ref):
    n_tok = feats_ref.shape[0]
    inv_n = 1.0 / (2 * n_tok)

    @pl.when((pl.program_id(1) == 0) & (pl.program_id(2) == 0))
    def _():
        sf = sf_ref[0]
        hid = jnp.sin(sf[0:1] * (_dot3(feats_ref[...], w1_ref[0]) + b1_ref[0]))
        hid_ref[...] = jnp.sin(sf[1:2] * (_dot3(hid, w2_ref[0]) + b2_ref[0]))

    hid = hid_ref[...]
    t_lin = feats_ref[:, 0:1]
    hf = _dot3(hid, w3f_ref[0]) * jnp.exp(-t_lin * jnp.exp(ldf_ref[0]))
    hb = _dot3(hid, w3b_ref[0]) * jnp.exp(-t_lin * jnp.exp(ldb_ref[0]))
    row = lax.broadcasted_iota(jnp.int32, hf.shape, 0)
    hb = jnp.where(row == 0, 0.0, hb)
    a = hf + hb
    d = hf - hb
    wk = jnp.where(row == 0, inv_n, 2.0 * inv_n)
    kr_ref[0, 0] = _dot(c_ref[...], a.astype(BF16)) * wk
    ki_ref[0, 0] = -_dot(s_ref[...], d.astype(BF16)) * wk
    sign = jnp.where((row & 1) == 0, 1.0, -1.0)
    kny_ref[0, 0] = jnp.sum(a * sign, axis=0, keepdims=True) * inv_n


def _hyena_spectra(n_tok, w1p, b1, w2, b2, w3, sf, log_decay):
    depth = w3.shape[0]
    ct = 256
    nc = HY_W // ct
    cmat, smat = _dft_tables(n_tok)
    feats = _hyena_feats(n_tok)
    const = lambda shape: pl.BlockSpec(shape, lambda l, n, c: (0,) * len(shape),
                                       pipeline_mode=pl.Buffered(1))
    per_l = lambda shape: pl.BlockSpec((1,) + shape, lambda l, n, c: (l, 0, 0))
    fcol = lambda l, n, c: (l, 0, n * nc + c)
    bcol = lambda l, n, c: (l, 0, HY_ORDER * nc + n * nc + c)
    out_spec = pl.BlockSpec((1, 1, n_tok, ct), lambda l, n, c: (l, n, 0, c))
    return pl.pallas_call(
        _filt_kernel,
        grid=(depth, HY_ORDER, nc),
        in_specs=[const((n_tok, LANES)),
                  per_l((LANES, HY_FFN)), per_l((1, HY_FFN)),
                  per_l((HY_FFN, HY_FFN)), per_l((1, HY_FFN)),
                  pl.BlockSpec((1, HY_FFN, ct), fcol), pl.BlockSpec((1, HY_FFN, ct), bcol),
                  per_l((2, HY_FFN)),
                  pl.BlockSpec((1, 1, ct), fcol), pl.BlockSpec((1, 1, ct), bcol),
                  const((n_tok, n_tok)), const((n_tok, n_tok))],
        out_specs=[out_spec, out_spec,
                   pl.BlockSpec((1, 1, 1, ct), lambda l, n, c: (l, n, 0, c))],
        out_shape=[jax.ShapeDtypeStruct((depth, HY_ORDER, n_tok, HY_W), F32),
                   jax.ShapeDtypeStruct((depth, HY_ORDER, n_tok, HY_W), F32),
                   jax.ShapeDtypeStruct((depth, HY_ORDER, 1, HY_W), F32)],
        scratch_shapes=[pltpu.VMEM((n_tok, HY_FFN), F32)],
        compiler_params=_cparams(("arbitrary",) * 3),
        name="hyena_spectra",
    )(feats, w1p, b1.reshape(depth, 1, HY_FFN), w2, b2.reshape(depth, 1, HY_FFN), w3, w3, sf,
      log_decay.reshape(depth, 1, -1), log_decay.reshape(depth, 1, -1), cmat, smat)


def _inproj_kernel(x_ref, mod_ref, w_ref, o_ref, *, gates):
    shift = mod_ref[0, 0:1, :]
    scale = mod_ref[0, 1:2, :]
    h = (x_ref[...] * (1.0 + scale) + shift).astype(BF16)
    acc = _dot(h, w_ref[0])
    if gates:
        tn = acc.shape[1]
        col = pl.program_id(0) * tn + lax.broadcasted_iota(jnp.int32, acc.shape, 1)
        sg = _sigmoid(acc)
        o_ref[...] = jnp.where(col >= N_BRANCH * D_MODEL, acc * sg, sg).astype(BF16)
    else:
        o_ref[...] = acc


def _inproj(x2d, mod, w, l, seq_len, gates):
    t = x2d.shape[0]
    n_out = w.shape[-1]
    per_batch = mod.shape[0] > 1
    tm = min(1024, seq_len if per_batch else t)
    tn = n_out // 2
    tiles_per_batch = seq_len // tm
    mod_idx = (lambda j, i: (i // tiles_per_batch, 0, 0)) if per_batch else (lambda j, i: (0, 0, 0))
    return pl.pallas_call(
        functools.partial(_inproj_kernel, gates=gates),
        grid=(n_out // tn, t // tm),
        in_specs=[pl.BlockSpec((tm, D_MODEL), lambda j, i: (i, 0)),
                  pl.BlockSpec((1, 3, D_MODEL), mod_idx),
                  pl.BlockSpec((1, D_MODEL, tn), lambda j, i: (l, 0, j))],
        out_specs=pl.BlockSpec((tm, tn), lambda j, i: (i, j)),
        out_shape=jax.ShapeDtypeStruct((t, n_out), BF16 if gates else F32),
        compiler_params=_cparams(("arbitrary", "arbitrary")),
        name="inproj_gates" if gates else "inproj",
    )(x2d, mod, w)


def _short_conv(x, w_ref, b_ref):
    n_tok = x.shape[0]
    row = lax.broadcasted_iota(jnp.int32, x.shape, 0)
    prev = jnp.where(row == 0, 0.0, pltpu.roll(x, 1, 0))
    nxt = jnp.where(row == n_tok - 1, 0.0, pltpu.roll(x, n_tok - 1, 0))
    return prev * w_ref[0:1, :] + x * w_ref[1:2, :] + nxt * w_ref[2:3, :] + b_ref[...]


def _hyena_kernel(*refs, conv_z, tk):
    if conv_z:
        (zin_ref, gin_ref, wz_ref, bz_ref, wg_ref, bg_ref, kr_ref, ki_ref, kny_ref, skip_ref,
         c_ref, s_ref, o_ref, zb_ref, acc_ref) = refs
    else:
        (zin_ref, gin_ref, wg_ref, bg_ref, kr_ref, ki_ref, kny_ref, skip_ref,
         c_ref, s_ref, o_ref, zb_ref, acc_ref) = refs
    nb, n_tok, ct = zin_ref.shape
    for i in range(nb):
        z = zin_ref[i]
        if conv_z:
            z = _short_conv(z, wz_ref, bz_ref)
        zb_ref[...] = z.astype(BF16)
        for j in range(n_tok // tk):
            rows = slice(j * tk, (j + 1) * tk)
            zr = _dot(c_ref[rows, :], zb_ref[...])
            zs = _dot(s_ref[rows, :], zb_ref[...])
            kr = kr_ref[0, 0, rows, :]
            ki = ki_ref[0, 0, rows, :]
            yr = (zr * kr + zs * ki).astype(BF16)
            ym = (zs * kr - zr * ki).astype(BF16)
            part = _dot(c_ref[:, rows], yr) + _dot(s_ref[:, rows], ym)
            if j == 0:
                acc_ref[...] = part
            else:
                acc_ref[...] += part
        row = lax.broadcasted_iota(jnp.int32, z.shape, 0)
        sign = jnp.where((row & 1) == 0, 1.0, -1.0)
        zny = jnp.sum(z * sign, axis=0, keepdims=True)
        y = acc_ref[...] + sign * (zny * kny_ref[0, 0]) + z * skip_ref[0, 0]
        o_ref[i] = _short_conv(gin_ref[i], wg_ref, bg_ref) * y


def _hyena_order(zin, zblk, gin, gblk, conv_w, conv_b, kr, ki, kny, skip, l, order, conv_z):
    b, n_tok = zin.shape[0], zin.shape[1]
    cmat, smat = _dft_tables(n_tok)
    if n_tok >= 2048:
        ct, nb = 256, 1
    else:
        ct, nb = 512, min(8, b)
    nc = HY_W // ct
    tk = min(256, n_tok)
    const = lambda shape: pl.BlockSpec(shape, lambda c, i: (0,) * len(shape),
                                       pipeline_mode=pl.Buffered(1))
    zspec = pl.BlockSpec((nb, n_tok, ct), lambda c, i: (i, 0, zblk * nc + c))
    gspec = pl.BlockSpec((nb, n_tok, ct), lambda c, i: (i, 0, gblk * nc + c))
    wspec = lambda blk: pl.BlockSpec((1, 3, ct), lambda c, i: (l, 0, blk * nc + c))
    bspec = lambda blk: pl.BlockSpec((1, 1, ct), lambda c, i: (l, 0, blk * nc + c))
    kspec = pl.BlockSpec((1, 1, n_tok, ct), lambda c, i: (l, order, 0, c))
    in_specs = [zspec, gspec]
    args = [zin, gin]
    if conv_z:
        in_specs += [wspec(0), bspec(0)]
        args += [conv_w, conv_b]
    in_specs += [wspec(order + 1), bspec(order + 1), kspec, kspec,
                 pl.BlockSpec((1, 1, 1, ct), lambda c, i: (l, order, 0, c)),
                 pl.BlockSpec((1, 1, 1, ct), lambda c, i: (l, order, 0, c)),
                 const((n_tok, n_tok)), const((n_tok, n_tok))]
    args += [conv_w, conv_b, kr, ki, kny, skip, cmat, smat]

    def body(*refs):
        refs = list(refs)
        lo = 2
        n_small = 4 if conv_z else 2
        for k in range(lo, lo + n_small):
            refs[k] = refs[k].at[0]
        _hyena_kernel(*refs, conv_z=conv_z, tk=tk)

    return pl.pallas_call(
        body,
        grid=(nc, b // nb),
        in_specs=in_specs,
        out_specs=pl.BlockSpec((nb, n_tok, ct), lambda c, i: (i, 0, c)),
        out_shape=jax.ShapeDtypeStruct((b, n_tok, HY_W), F32),
        scratch_shapes=[pltpu.VMEM((n_tok, ct), BF16), pltpu.VMEM((n_tok, ct), F32)],
        compiler_params=_cparams(("arbitrary", "arbitrary")),
        name=f"hyena_order{order}",
    )(*args)


def _rms(x, g, n):
    ms = jnp.sum(x * x, axis=-1, keepdims=True) * (1.0 / n)
    return x * lax.rsqrt(ms + RMS_EPS) * g


def _da_values_ext(v):
    lane = lax.broadcasted_iota(jnp.int32, (v.shape[0], LANES), 1)
    ones_col = jnp.where(lane == 0, 1.0, 0.0).astype(BF16)
    pieces = []
    for h in range(DA_HEADS):
        pieces += [v[:, h * DA_V_DIM:(h + 1) * DA_V_DIM], ones_col]
    return jnp.concatenate(pieces, axis=1)


def _mla_values_ext(ckv_b, wv):
    v = _dot(ckv_b, wv)
    lane = lax.broadcasted_iota(jnp.int32, v.shape, 1)
    return jnp.where((lane & (MLA_HEAD_PAD - 1)) == MLA_V, 1.0, v).astype(BF16)


def _prep_kernel(*refs, rope, n_state_in):
    it = iter(refs)
    q_ref, k_ref, v_ref, sm_ref, qn_ref, wuq_ref, kvn_ref, wk_ref, wv_ref, e_ref = (
        next(it) for _ in range(10))
    tab_ref = next(it) if rope else None
    for _ in range(n_state_in):
        next(it)
    qd_ref, kd_ref, vd_ref, qm_ref, km_ref, vm_ref = (next(it) for _ in range(6))
    q = q_ref[...] * (DA_HEAD_DIM ** -0.5)
    k = k_ref[...]
    v = v_ref[...]
    dq = sm_ref[:, 0:MLA_Q_LORA]
    dkv = sm_ref[:, MLA_Q_LORA:MLA_Q_LORA + MLA_KV_LORA]
    kr = sm_ref[:, 5 * LANES:6 * LANES]
    c_kv = _rms(dkv, kvn_ref[0], MLA_KV_LORA)
    if not rope:
        ks_ref, vs_ref, ckvs_ref, krs_ref = (next(it) for _ in range(4))
        nbt, _, n_tok, _ = ks_ref.shape
        ks_ref[:, 0] = k.reshape(nbt, n_tok, k.shape[1])
        vs_ref[:, 0] = v.reshape(nbt, n_tok, v.shape[1])
        ckvs_ref[:, 0] = c_kv.reshape(nbt, n_tok, MLA_KV_LORA)
        krs_ref[:, 0] = kr[:, :MLA_ROPE].reshape(nbt, n_tok, MLA_ROPE)
    if rope:
        q = _rope_apply(q, tab_ref, 0, DA_HEAD_DIM // 4)
        k = _rope_apply(k, tab_ref, 0, DA_HEAD_DIM // 4)
    qd_ref[...] = q.astype(BF16)
    kd_ref[...] = k.astype(BF16)
    vd_ref[...] = _da_values_ext(v.astype(BF16))

    c_q = _rms(dq, qn_ref[0], MLA_Q_LORA)
    qm = _dot(c_q.astype(BF16), wuq_ref[0]) * ((MLA_NOPE + MLA_ROPE) ** -0.5)
    ckv_b = c_kv.astype(BF16)
    if rope:
        qm = _rope_apply(qm, tab_ref, 1, MLA_ROPE // 4)
        kr = _rope_apply(kr, tab_ref, 2, MLA_ROPE // 4)
    qm_ref[...] = qm.astype(BF16)
    km_ref[...] = (_dot(ckv_b, wk_ref[0]) + _dot(kr.astype(BF16), e_ref[...])).astype(BF16)
    vm_ref[...] = _mla_values_ext(ckv_b, wv_ref[0])


def _prep(proj, wts, l, seq_len, rope, states=None):
    t = proj.shape[0]
    depth = wts["ln_g"].shape[0]
    tm = min(512, seq_len if rope else t)
    tiles_per_seq = seq_len // tm
    blk = lambda w, idx: pl.BlockSpec((tm, w), lambda i: (i, idx))
    wl = lambda shape: pl.BlockSpec((1,) + shape, lambda i: (l, 0, 0))
    in_specs = [blk(512, COL_Q // 512), blk(512, COL_K // 512), blk(512, COL_V // 512),
                blk(SMALL_W, COL_SMALL // SMALL_W),
                wl((1, MLA_Q_LORA)), wl((MLA_Q_LORA, MLA_HEADS * MLA_HEAD_PAD)),
                wl((1, MLA_KV_LORA)), wl((MLA_KV_LORA, MLA_HEADS * MLA_HEAD_PAD)),
                wl((MLA_KV_LORA, MLA_HEADS * MLA_HEAD_PAD)),
                pl.BlockSpec((LANES, MLA_HEADS * MLA_HEAD_PAD), lambda i: (0, 0))]
    args = [proj, proj, proj, proj, wts["q_norm"], wts["w_uq"], wts["kv_norm"], wts["w_uk"],
            wts["w_uv"], wts["e_place"]]
    if rope:
        in_specs.append(pl.BlockSpec((3, 3, tm, LANES), lambda i: (0, 0, i % tiles_per_seq, 0)))
        args.append(_rope_tables(seq_len))
    out = lambda w: pl.BlockSpec((tm, w), lambda i: (i, 0))
    out_specs = [out(512), out(512), out(1024), out(1024), out(1024), out(1024)]
    out_shape = [jax.ShapeDtypeStruct((t, w), BF16) for w in (512, 512, 1024, 1024, 1024, 1024)]
    aliases = {}
    n_state_in = 0
    if not rope:
        nbt = tm // seq_len
        for w in (512, 512, MLA_KV_LORA, MLA_ROPE):
            out_specs.append(pl.BlockSpec((nbt, 1, seq_len, w), lambda i: (i, l, 0, 0)))
            out_shape.append(jax.ShapeDtypeStruct((t // seq_len, depth, seq_len, w), F32))
        if states is not None:
            n_state_in = len(states)
            for k, s in enumerate(states):
                aliases[len(args)] = 6 + k
                in_specs.append(pl.BlockSpec(memory_space=pl.ANY))
                args.append(s)
    return pl.pallas_call(
        functools.partial(_prep_kernel, rope=rope, n_state_in=n_state_in),
        grid=(t // tm,),
        in_specs=in_specs,
        out_specs=out_specs,
        out_shape=out_shape,
        input_output_aliases=aliases,
        compiler_params=_cparams(("arbitrary",)),
        name="attn_prep",
    )(*args)


def _prep_cache_kernel(k_ref, v_ref, ckv_ref, kr_ref, wk_ref, wv_ref, e_ref,
                       kd_ref, vd_ref, km_ref, vm_ref):
    kd_ref[0] = k_ref[0, 0].astype(BF16)
    vd_ref[0] = _da_values_ext(v_ref[0, 0].astype(BF16))
    ckv_b = ckv_ref[0, 0].astype(BF16)
    km_ref[0] = (_dot(ckv_b, wk_ref[0]) + _dot(kr_ref[0, 0].astype(BF16), e_ref[...])).astype(BF16)
    vm_ref[0] = _mla_values_ext(ckv_b, wv_ref[0])


def _prep_cache(cache_k, cache_v, cache_ckv, cache_kr, wts, l):
    b, _, p = cache_k.shape[:3]
    ck = cache_k.reshape(b, -1, p, 512)
    cv = cache_v.reshape(b, -1, p, 512)
    cin = lambda w: pl.BlockSpec((1, 1, p, w), lambda i: (i, l, 0, 0))
    wl = lambda shape: pl.BlockSpec((1,) + shape, lambda i: (l, 0, 0))
    out = lambda w: pl.BlockSpec((1, p, w), lambda i: (i, 0, 0))
    return pl.pallas_call(
        _prep_cache_kernel,
        grid=(b,),
        in_specs=[cin(512), cin(512), cin(MLA_KV_LORA), cin(MLA_ROPE),
                  wl((MLA_KV_LORA, MLA_HEADS * MLA_HEAD_PAD)), wl((MLA_KV_LORA, MLA_HEADS * MLA_HEAD_PAD)),
                  pl.BlockSpec((MLA_ROPE, MLA_HEADS * MLA_HEAD_PAD), lambda i: (0, 0))],
        out_specs=[out(512), out(1024), out(1024), out(1024)],
        out_shape=[jax.ShapeDtypeStruct((b, p, 512), BF16), jax.ShapeDtypeStruct((b, p, 1024), BF16),
                   jax.ShapeDtypeStruct((b, p, 1024), BF16), jax.ShapeDtypeStruct((b, p, 1024), BF16)],
        compiler_params=_cparams(("arbitrary",)),
        name="attn_prep_cache",
    )(ck, cv, cache_ckv, cache_kr, wts["w_uk"], wts["w_uv"], wts["e_place"][:MLA_ROPE])


def _softmax_pv(q, k_parts, v_parts, dv):
    s = [_dot_nt(q, k) for k in k_parts]
    m = s[0].max(axis=-1, keepdims=True)
    for sp in s[1:]:
        m = jnp.maximum(m, sp.max(axis=-1, keepdims=True))
    o = None
    for sp, v in zip(s, v_parts):
        pv = _dot(jnp.exp((sp - m).astype(BF16)), v)
        o = pv if o is None else o + pv
    return o[:, :dv] / o[:, dv:dv + 1]


def _da_kernel(*refs, lam_init, has_ctx):
    if has_ctx:
        q_ref, kn_ref, vn_ref, kc_ref, vc_ref, lam_ref, g_ref, o_ref = refs
    else:
        q_ref, kn_ref, vn_ref, lam_ref, g_ref, o_ref = refs
    lp = lam_ref[0]
    lam = (jnp.exp(jnp.sum(lp[0:1] * lp[1:2], axis=1, keepdims=True))
           - jnp.exp(jnp.sum(lp[2:3] * lp[3:4], axis=1, keepdims=True)) + lam_init)
    for bi in range(q_ref.shape[0]):
        for h in range(DA_HEADS):
            vcols = slice(h * 2 * DA_V_DIM, (h + 1) * 2 * DA_V_DIM)
            outs = []
            for j in range(2):
                cols = slice((j * DA_HEADS + h) * DA_HEAD_DIM, (j * DA_HEADS + h + 1) * DA_HEAD_DIM)
                k_parts = [kn_ref[bi, :, cols]]
                v_parts = [vn_ref[bi, :, vcols]]
                if has_ctx:
                    k_parts.insert(0, kc_ref[bi, :, cols])
                    v_parts.insert(0, vc_ref[bi, :, vcols])
                outs.append(_softmax_pv(q_ref[bi, :, cols], k_parts, v_parts, DA_V_DIM))
            o = outs[0] - lam * outs[1]
            o_ref[bi, :, h * DA_V_DIM:(h + 1) * DA_V_DIM] = (
                _rms(o, g_ref[0], DA_V_DIM) * (1.0 - lam_init))


def _da_attention(qd, kd, vd, ctx, da_lambda, da_subln, l, tq, nbq):
    b, n, _ = qd.shape
    has_ctx = ctx is not None
    lam_init = 0.8 - 0.6 * math.exp(-0.3 * l)
    full = lambda a: pl.BlockSpec((nbq,) + a.shape[1:], lambda i, j: (i, 0, 0))
    in_specs = [pl.BlockSpec((nbq, tq, 512), lambda i, j: (i, j, 0)), full(kd), full(vd)]
    args = [qd, kd, vd]
    if has_ctx:
        in_specs += [full(ctx[0]), full(ctx[1])]
        args += [ctx[0], ctx[1]]
    in_specs += [pl.BlockSpec((1, 4, DA_HEAD_DIM), lambda i, j: (l, 0, 0)),
                 pl.BlockSpec((1, 1, DA_V_DIM), lambda i, j: (l, 0, 0))]
    args += [da_lambda, da_subln.reshape(-1, 1, DA_V_DIM)]
    return pl.pallas_call(
        functools.partial(_da_kernel, lam_init=lam_init, has_ctx=has_ctx),
        grid=(b // nbq, n // tq),
        in_specs=in_specs,
        out_specs=pl.BlockSpec((nbq, tq, 512), lambda i, j: (i, j, 0)),
        out_shape=jax.ShapeDtypeStruct((b, n, 512), F32),
        compiler_params=_cparams(("arbitrary", "arbitrary")),
        name="diff_attention",
    )(*args)


def _mla_kernel(*refs, has_ctx):
    if has_ctx:
        q_ref, kn_ref, vn_ref, kc_ref, vc_ref, o_ref = refs
    else:
        q_ref, kn_ref, vn_ref, o_ref = refs
    for bi in range(q_ref.shape[0]):
        for h in range(MLA_HEADS):
            cols = slice(h * MLA_HEAD_PAD, (h + 1) * MLA_HEAD_PAD)
            k_parts = [kn_ref[bi, :, cols]]
            v_parts = [vn_ref[bi, :, cols]]
            if has_ctx:
                k_parts.insert(0, kc_ref[bi, :, cols])
                v_parts.insert(0, vc_ref[bi, :, cols])
            o_ref[bi, :, h * MLA_V:(h + 1) * MLA_V] = _softmax_pv(
                q_ref[bi, :, cols], k_parts, v_parts, MLA_V)


def _mla_attention(qm, km, vm, ctx, tq, nbq):
    b, n, _ = qm.shape
    has_ctx = ctx is not None
    full = lambda a: pl.BlockSpec((nbq,) + a.shape[1:], lambda i, j: (i, 0, 0))
    in_specs = [pl.BlockSpec((nbq, tq, 1024), lambda i, j: (i, j, 0)), full(km), full(vm)]
    args = [qm, km, vm]
    if has_ctx:
        in_specs += [full(ctx[0]), full(ctx[1])]
        args += [ctx[0], ctx[1]]
    return pl.pallas_call(
        functools.partial(_mla_kernel, has_ctx=has_ctx),
        grid=(b // nbq, n // tq),
        in_specs=in_specs,
        out_specs=pl.BlockSpec((nbq, tq, 512), lambda i, j: (i, j, 0)),
        out_shape=jax.ShapeDtypeStruct((b, n, 512), F32),
        compiler_params=_cparams(("arbitrary", "arbitrary")),
        name="latent_attention",
    )(*args)


def _outproj_kernel(x_ref, mod_ref, yh_ref, yd_ref, ym_ref, p0_ref, p1_ref, p2_ref,
                    m0_ref, m1_ref, m2_ref, wb_ref, wo_ref, g_ref, b_ref, o_ref, *, alpha):
    merged = None
    for y_ref, p_ref, m_ref, n in ((yh_ref, p0_ref, m0_ref, 0), (yd_ref, p1_ref, m1_ref, 1),
                                   (ym_ref, p2_ref, m2_ref, 2)):
        br = (y_ref[...] * p_ref[...].astype(F32)).astype(BF16)
        term = m_ref[...].astype(F32) * _dot(br, wb_ref[0, n])
        merged = term if merged is None else merged + term
    out = _dot(merged.astype(BF16), wo_ref[0])
    gate = mod_ref[0, 2:3, :]
    y = alpha * x_ref[...] + gate * out
    mu = jnp.mean(y, axis=-1, keepdims=True)
    yc = y - mu
    var = jnp.mean(yc * yc, axis=-1, keepdims=True)
    o_ref[...] = yc * lax.rsqrt(var + LN_EPS) * g_ref[0] + b_ref[0]


def _outproj(x2d, mod, y_hy, y_da, y_mla, gates, w_branch, w_out, ln_g, ln_b, l, seq_len, alpha):
    t = x2d.shape[0]
    per_batch = mod.shape[0] > 1
    tm = min(256, seq_len if per_batch else t)
    tiles_per_batch = seq_len // tm
    mod_idx = (lambda i: (i // tiles_per_batch, 0, 0)) if per_batch else (lambda i: (0, 0, 0))
    row = lambda w, idx=0: pl.BlockSpec((tm, w), lambda i: (i, idx))
    depth = ln_g.shape[0]
    return pl.pallas_call(
        functools.partial(_outproj_kernel, alpha=alpha),
        grid=(t // tm,),
        in_specs=[row(D_MODEL), pl.BlockSpec((1, 3, D_MODEL), mod_idx),
                  row(BR_W), row(BR_W), row(BR_W),
                  row(BR_W, GATE_PATHS // BR_W), row(BR_W, GATE_PATHS // BR_W + 1),
                  row(BR_W, GATE_PATHS // BR_W + 2),
                  row(D_MODEL, GATE_MERGE // D_MODEL), row(D_MODEL, GATE_MERGE // D_MODEL + 1),
                  row(D_MODEL, GATE_MERGE // D_MODEL + 2),
                  pl.BlockSpec((1, N_BRANCH, BR_W, D_MODEL), lambda i: (l, 0, 0, 0)),
                  pl.BlockSpec((1, D_MODEL, D_MODEL), lambda i: (l, 0, 0)),
                  pl.BlockSpec((1, 1, D_MODEL), lambda i: (l, 0, 0)),
                  pl.BlockSpec((1, 1, D_MODEL), lambda i: (l, 0, 0))],
        out_specs=row(D_MODEL),
        out_shape=jax.ShapeDtypeStruct((t, D_MODEL), F32),
        compiler_params=_cparams(("arbitrary",)),
        name="outproj",
    )(x2d, mod, y_hy, y_da, y_mla, gates, gates, gates, gates, gates, gates, w_branch, w_out,
      ln_g.reshape(depth, 1, D_MODEL), ln_b.reshape(depth, 1, D_MODEL))


def _layer(x, mod, l, wts, spectra, ctx=None, states=None):
    b, n, _ = x.shape
    depth = wts["ln_g"].shape[0]
    alpha = (2 * depth) ** 0.25
    x2d = x.reshape(b * n, D_MODEL)
    proj = _inproj(x2d, mod, wts["w_in"], l, n, gates=False)
    gates = _inproj(x2d, mod, wts["w_gate"], l, n, gates=True)
    proj3 = proj.reshape(b, n, D_IN_A)

    kr_s, ki_s, kny_s = spectra
    z2 = _hyena_order(proj3, 0, proj3, 1, wts["conv_w"], wts["conv_b"], kr_s, ki_s, kny_s,
                      wts["skip"], l, 0, True)
    y_hy = _hyena_order(z2, 0, proj3, 2, wts["conv_w"], wts["conv_b"], kr_s, ki_s, kny_s,
                        wts["skip"], l, 1, False)

    prep_out = _prep(proj, wts, l, n, rope=ctx is not None, states=states)
    r3 = lambda a: a.reshape(b, n, a.shape[-1])
    qd, kd, vd, qm, km, vm = map(r3, prep_out[:6])
    new_states = tuple(prep_out[6:]) if ctx is None else None
    if ctx is not None:
        kd_c, vd_c, km_c, vm_c = _prep_cache(*ctx, wts, l)
        da_ctx, mla_ctx = (kd_c, vd_c), (km_c, vm_c)
    else:
        da_ctx = mla_ctx = None
    tq = min(256, n)
    nbq = math.gcd(b, max(1, 1024 // n))
    y_da = _da_attention(qd, kd, vd, da_ctx, wts["da_lambda"], wts["da_subln"], l, tq, nbq)
    y_mla = _mla_attention(qm, km, vm, mla_ctx, tq, nbq)

    x_new = _outproj(x2d, mod, y_hy.reshape(b * n, HY_W), y_da.reshape(b * n, BR_W),
                     y_mla.reshape(b * n, BR_W), gates, wts["w_branch"], wts["w_out"],
                     wts["ln_g"], wts["ln_b"], l, n, alpha)
    return x_new.reshape(b, n, D_MODEL), new_states


def _prepare_weights(w_in, hy_conv_w, hy_conv_b, hy_ffn_w1, hy_skip, da_lambda, da_subln,
                     mla_q_norm, mla_w_uq, mla_kv_norm, mla_w_ukv, w_branch, w_out, ln_g, ln_b):
    depth = w_in.shape[0]
    w_in_p = jnp.pad(w_in[..., :ORIG_PATHS].astype(BF16),
                     ((0, 0), (0, 0), (0, SMALL_W - SMALL_USED)))
    w_gate = jnp.concatenate([w_in[..., ORIG_MERGE:ORIG_END], w_in[..., ORIG_PATHS:ORIG_MERGE]],
                             axis=-1).astype(BF16)
    uq = mla_w_uq.reshape(depth, MLA_Q_LORA, MLA_HEADS, MLA_NOPE + MLA_ROPE)
    uq = jnp.pad(uq, ((0, 0), (0, 0), (0, 0), (0, MLA_HEAD_PAD - MLA_NOPE - MLA_ROPE)))
    w_uq = uq.reshape(depth, MLA_Q_LORA, MLA_HEADS * MLA_HEAD_PAD).astype(BF16)
    ukv = mla_w_ukv.reshape(depth, MLA_KV_LORA, MLA_HEADS, MLA_NOPE + MLA_V)
    uk = jnp.pad(ukv[..., :MLA_NOPE], ((0, 0), (0, 0), (0, 0), (0, MLA_HEAD_PAD - MLA_NOPE)))
    w_uk = uk.reshape(depth, MLA_KV_LORA, MLA_HEADS * MLA_HEAD_PAD).astype(BF16)
    uv = jnp.pad(ukv[..., MLA_NOPE:], ((0, 0), (0, 0), (0, 0), (0, MLA_HEAD_PAD - MLA_V)))
    w_uv = uv.reshape(depth, MLA_KV_LORA, MLA_HEADS * MLA_HEAD_PAD).astype(BF16)
    e = np.zeros((LANES, MLA_HEADS * MLA_HEAD_PAD), np.float32)
    for h in range(MLA_HEADS):
        for i in range(MLA_ROPE):
            e[i, h * MLA_HEAD_PAD + MLA_NOPE + i] = 1.0
    w1p = jnp.pad(hy_ffn_w1, ((0, 0), (0, LANES - hy_ffn_w1.shape[1]), (0, 0)))
    return {
        "w_in": w_in_p, "w_gate": w_gate, "conv_w": hy_conv_w, "conv_b": hy_conv_b.reshape(depth, 1, -1),
        "skip": hy_skip.reshape(depth, HY_ORDER, 1, HY_W), "w1p": w1p,
        "da_lambda": da_lambda, "da_subln": da_subln,
        "q_norm": mla_q_norm.reshape(depth, 1, -1), "kv_norm": mla_kv_norm.reshape(depth, 1, -1),
        "w_uq": w_uq, "w_uk": w_uk, "w_uv": w_uv, "e_place": jnp.asarray(e, dtype=BF16),
        "w_branch": w_branch.astype(BF16), "w_out": w_out.astype(BF16), "ln_g": ln_g, "ln_b": ln_b,
    }


def kernel(x_prompt, x_sample, c, cache_diff_k, cache_diff_v, cache_mla_ckv, cache_mla_krope, c_ctx, w_mod, b_mod, w_in, hy_conv_w, hy_conv_b, hy_ffn_w1, hy_ffn_b1, hy_ffn_w2, hy_ffn_b2, hy_ffn_w3, hy_sin_freq, hy_log_decay, hy_skip, da_lambda, da_subln, mla_q_norm, mla_w_uq, mla_kv_norm, mla_w_ukv, w_branch, w_out, ln_g, ln_b):
    depth = w_in.shape[0]
    b_ctx, n_ctx, _ = x_prompt.shape
    b_lat, n_lat, _ = x_sample.shape
    wts = _prepare_weights(w_in, hy_conv_w, hy_conv_b, hy_ffn_w1, hy_skip, da_lambda, da_subln,
                           mla_q_norm, mla_w_uq, mla_kv_norm, mla_w_ukv, w_branch, w_out, ln_g, ln_b)

    rows = -(-(1 + b_lat) // 8) * 8
    cvec = jnp.zeros((rows, D_MODEL), F32).at[0].set(c_ctx).at[1:1 + b_lat].set(c)
    mods = _modulation(cvec, w_mod, b_mod).reshape(depth, rows, 3, D_MODEL)

    spec_args = (wts["w1p"], hy_ffn_b1, hy_ffn_w2, hy_ffn_b2, hy_ffn_w3, hy_sin_freq, hy_log_decay)
    spectra_ctx = _hyena_spectra(n_ctx, *spec_args)
    spectra_lat = spectra_ctx if n_lat == n_ctx else _hyena_spectra(n_lat, *spec_args)

    xp = x_prompt
    states = None
    for l in range(depth):
        xp, states = _layer(xp, mods[l, 0:1], l, wts, spectra_ctx, states=states)
    state_k = states[0].reshape(b_ctx, depth, n_ctx, 2, DA_HEADS, DA_HEAD_DIM)
    state_v = states[1].reshape(b_ctx, depth, n_ctx, DA_HEADS, DA_V_DIM)
    state_ckv, state_kr = states[2], states[3]

    xs = x_sample
    ctx = (cache_diff_k, cache_diff_v, cache_mla_ckv, cache_mla_krope)
    for l in range(depth):
        xs, _ = _layer(xs, mods[l, 1:1 + b_lat], l, wts, spectra_lat, ctx=ctx)

    return (xp, xs, state_k, state_v, state_ckv, state_kr)
```

```python
import functools
import math

import numpy as np
import jax
import jax.numpy as jnp
from jax import lax
from jax.experimental import pallas as pl
from jax.experimental.pallas import tpu as pltpu

F32 = jnp.float32
BF16 = jnp.bfloat16

D_MODEL = 1024
N_BRANCH = 3
BR_W = 512
HY_W = 512
HY_ORDER = 2
HY_BANDS = 16
HY_FFN = 64
DA_HEADS = 4
DA_HEAD_DIM = 64
DA_V_DIM = 2 * DA_HEAD_DIM
MLA_HEADS = 8
MLA_NOPE = 64
MLA_ROPE = 32
MLA_V = 64
MLA_Q_LORA = 384
MLA_KV_LORA = 256
GRID_W = 64
ROPE_BASE = 10000.0
LN_EPS = 1e-5
RMS_EPS = 1e-6

LANES = 128
MLA_HEAD_PAD = LANES
MLA_V_PAD = 2 * LANES
VMEM_LIMIT = 56 * 1024 * 1024
ATTN_TQ = 512

COL_HY = 0
COL_Q = 3 * HY_W
COL_K = COL_Q + 2 * DA_HEADS * DA_HEAD_DIM
COL_V = COL_K + 2 * DA_HEADS * DA_HEAD_DIM
COL_SMALL = COL_V + DA_HEADS * DA_V_DIM
SMALL_W = 1024
SMALL_USED = MLA_Q_LORA + MLA_KV_LORA + MLA_ROPE
D_IN_A = COL_SMALL + SMALL_W
GATE_MERGE = 0
GATE_PATHS = N_BRANCH * D_MODEL
D_IN_G = GATE_PATHS + N_BRANCH * BR_W
ORIG_PATHS = COL_SMALL + SMALL_USED
ORIG_MERGE = ORIG_PATHS + N_BRANCH * BR_W
ORIG_END = ORIG_MERGE + N_BRANCH * D_MODEL


def _cparams(sem):
    return pltpu.CompilerParams(dimension_semantics=sem, vmem_limit_bytes=VMEM_LIMIT)


def _split_bf16(x):
    hi = x.astype(BF16)
    lo = (x - hi.astype(F32)).astype(BF16)
    return hi, lo


def _dot(a, b):
    return jnp.dot(a, b, preferred_element_type=F32)


def _dot_nt(a, b):
    return lax.dot_general(a, b, (((1,), (1,)), ((), ())), preferred_element_type=F32)


def _dot3(a, b):
    ah, al = _split_bf16(a)
    bh, bl = _split_bf16(b)
    return _dot(ah, bh) + _dot(ah, bl) + _dot(al, bh)


def _sigmoid(x):
    return 1.0 / (1.0 + jnp.exp(-x))


@functools.lru_cache(maxsize=None)
def _dft_tables(n_tok):
    k = np.arange(n_tok, dtype=np.int64)
    ang = np.pi * ((np.outer(k, k) % (2 * n_tok)).astype(np.float64)) / n_tok
    return (jnp.asarray(np.cos(ang), dtype=BF16), jnp.asarray(np.sin(ang), dtype=BF16))


@functools.lru_cache(maxsize=None)
def _hyena_feats(n_tok):
    t = np.arange(n_tok, dtype=np.float64)
    t_lin = t / (n_tok - 1)
    bands = np.arange(1, HY_BANDS + 1, dtype=np.float64)
    ang = (2.0 * np.pi / n_tok) * t[:, None] * bands
    feats = np.concatenate([t_lin[:, None], np.cos(ang), -np.sin(ang)], axis=-1)
    out = np.zeros((n_tok, LANES), np.float64)
    out[:, :feats.shape[1]] = feats
    return jnp.asarray(out, dtype=F32)


def _rope_pattern(pos_row, pos_col, dd):
    q = dd // 4
    freqs = 1.0 / (ROPE_BASE ** (np.arange(q, dtype=np.float64) / q))
    n = pos_row.shape[0]
    cos = np.zeros((n, dd)); sa = np.zeros((n, dd)); sb = np.zeros((n, dd))
    for g, pos in enumerate((pos_row, pos_col)):
        ang = pos[:, None].astype(np.float64) * freqs
        base = g * 2 * q
        cos[:, base:base + q] = np.cos(ang); cos[:, base + q:base + 2 * q] = np.cos(ang)
        sa[:, base:base + q] = -np.sin(ang)
        sb[:, base + q:base + 2 * q] = np.sin(ang)
    return cos, sa, sb


@functools.lru_cache(maxsize=None)
def _rope_tables(n_tok):
    t = np.arange(n_tok)
    row, col = t // GRID_W, t % GRID_W
    ones = np.ones((n_tok, LANES)); zeros = np.zeros((n_tok, LANES))
    c, a, b = _rope_pattern(row, col, DA_HEAD_DIM)
    da = [np.tile(c, (1, 2)), np.tile(a, (1, 2)), np.tile(b, (1, 2))]
    c, a, b = _rope_pattern(row, col, MLA_ROPE)
    mq = [ones.copy(), zeros.copy(), zeros.copy()]
    kr = [ones.copy(), zeros.copy(), zeros.copy()]
    for dst, src in zip(mq, (c, a, b)):
        dst[:, MLA_NOPE:MLA_NOPE + MLA_ROPE] = src
    for dst, src in zip(kr, (c, a, b)):
        dst[:, :MLA_ROPE] = src
    tab = np.stack([np.stack(da), np.stack(mq), np.stack(kr)])
    return jnp.asarray(tab, dtype=F32)


def _rope_apply(x, tab_ref, kind, q):
    cos = tab_ref[kind, 0]
    sa = tab_ref[kind, 1]
    sb = tab_ref[kind, 2]
    outs = []
    for c in range(x.shape[1] // LANES):
        xc = x[:, c * LANES:(c + 1) * LANES]
        outs.append(xc * cos + pltpu.roll(xc, LANES - q, 1) * sa + pltpu.roll(xc, q, 1) * sb)
    return outs[0] if len(outs) == 1 else jnp.concatenate(outs, axis=1)


def _mod_kernel(c_ref, w_ref, b_ref, o_ref):
    c = c_ref[...]
    o_ref[0] = _dot3(c * _sigmoid(c), w_ref[0]) + b_ref[0]


def _modulation(cvec, w_mod, b_mod):
    depth = w_mod.shape[0]
    rows = cvec.shape[0]
    return pl.pallas_call(
        _mod_kernel,
        grid=(depth, 3),
        in_specs=[pl.BlockSpec((rows, D_MODEL), lambda l, j: (0, 0)),
                  pl.BlockSpec((1, D_MODEL, D_MODEL), lambda l, j: (l, 0, j)),
                  pl.BlockSpec((1, 1, D_MODEL), lambda l, j: (l, 0, j))],
        out_specs=pl.BlockSpec((1, rows, D_MODEL), lambda l, j: (l, 0, j)),
        out_shape=jax.ShapeDtypeStruct((depth, rows, 3 * D_MODEL), F32),
        compiler_params=_cparams(("arbitrary", "arbitrary")),
        name="modulation",
    )(cvec, w_mod, b_mod.reshape(depth, 1, 3 * D_MODEL))


def _filt_kernel(feats_ref, w1_ref, b1_ref, w2_ref, b2_ref, w3f_ref, w3b_ref, sf_ref,
                 ldf_ref, ldb_ref, c_ref, s_ref, kr_ref, ki_ref, kny_ref, hid_ref):
    n_tok = feats_ref.shape[0]
    inv_n = 1.0 / (2 * n_tok)

    @pl.when((pl.program_id(1) == 0) & (pl.program_id(2) == 0))
    def _():
        sf = sf_ref[0]
        hid = jnp.sin(sf[0:1] * (_dot3(feats_ref[...], w1_ref[0]) + b1_ref[0]))
        hid_ref[...] = jnp.sin(sf[1:2] * (_dot3(hid, w2_ref[0]) + b2_ref[0]))

    hid = hid_ref[...]
    t_lin = feats_ref[:, 0:1]
    hf = _dot3(hid, w3f_ref[0]) * jnp.exp(-t_lin * jnp.exp(ldf_ref[0]))
    hb = _dot3(hid, w3b_ref[0]) * jnp.exp(-t_lin * jnp.exp(ldb_ref[0]))
    row = lax.broadcasted_iota(jnp.int32, hf.shape, 0)
    hb = jnp.where(row == 0, 0.0, hb)
    a = hf + hb
    d = hf - hb
    wk = jnp.where(row == 0, inv_n, 2.0 * inv_n)
    kr_ref[0, 0] = _dot(c_ref[...], a.astype(BF16)) * wk
    ki_ref[0, 0] = -_dot(s_ref[...], d.astype(BF16)) * wk
    sign = jnp.where((row & 1) == 0, 1.0, -1.0)
    kny_ref[0, 0] = jnp.sum(a * sign, axis=0, keepdims=True) * inv_n


def _hyena_spectra(n_tok, w1p, b1, w2, b2, w3, sf, log_decay):
    depth = w3.shape[0]
    ct = 256
    nc = HY_W // ct
    cmat, smat = _dft_tables(n_tok)
    feats = _hyena_feats(n_tok)
    const = lambda shape: pl.BlockSpec(shape, lambda l, n, c: (0,) * len(shape),
                                       pipeline_mode=pl.Buffered(1))
    per_l = lambda shape: pl.BlockSpec((1,) + shape, lambda l, n, c: (l, 0, 0))
    fcol = lambda l, n, c: (l, 0, n * nc + c)
    bcol = lambda l, n, c: (l, 0, HY_ORDER * nc + n * nc + c)
    out_spec = pl.BlockSpec((1, 1, n_tok, ct), lambda l, n, c: (l, n, 0, c))
    return pl.pallas_call(
        _filt_kernel,
        grid=(depth, HY_ORDER, nc),
        in_specs=[const((n_tok, LANES)),
                  per_l((LANES, HY_FFN)), per_l((1, HY_FFN)),
                  per_l((HY_FFN, HY_FFN)), per_l((1, HY_FFN)),
                  pl.BlockSpec((1, HY_FFN, ct), fcol), pl.BlockSpec((1, HY_FFN, ct), bcol),
                  per_l((2, HY_FFN)),
                  pl.BlockSpec((1, 1, ct), fcol), pl.BlockSpec((1, 1, ct), bcol),
                  const((n_tok, n_tok)), const((n_tok, n_tok))],
        out_specs=[out_spec, out_spec,
                   pl.BlockSpec((1, 1, 1, ct), lambda l, n, c: (l, n, 0, c))],
        out_shape=[jax.ShapeDtypeStruct((depth, HY_ORDER, n_tok, HY_W), F32),
                   jax.ShapeDtypeStruct((depth, HY_ORDER, n_tok, HY_W), F32),
                   jax.ShapeDtypeStruct((depth, HY_ORDER, 1, HY_W), F32)],
        scratch_shapes=[pltpu.VMEM((n_tok, HY_FFN), F32)],
        compiler_params=_cparams(("arbitrary",) * 3),
        name="hyena_spectra",
    )(feats, w1p, b1.reshape(depth, 1, HY_FFN), w2, b2.reshape(depth, 1, HY_FFN), w3, w3, sf,
      log_decay.reshape(depth, 1, -1), log_decay.reshape(depth, 1, -1), cmat, smat)


def _inproj_kernel(x_ref, mod_ref, w_ref, o_ref, *, gates):
    shift = mod_ref[0, 0:1, :]
    scale = mod_ref[0, 1:2, :]
    h = (x_ref[...] * (1.0 + scale) + shift).astype(BF16)
    acc = _dot(h, w_ref[0])
    if gates:
        tn = acc.shape[1]
        col = pl.program_id(0) * tn + lax.broadcasted_iota(jnp.int32, acc.shape, 1)
        sg = _sigmoid(acc)
        o_ref[...] = jnp.where(col >= N_BRANCH * D_MODEL, acc * sg, sg).astype(BF16)
    else:
        o_ref[...] = acc


def _inproj(x2d, mod, w, l, seq_len, gates):
    t = x2d.shape[0]
    n_out = w.shape[-1]
    per_batch = mod.shape[0] > 1
    tm = min(1024, seq_len if per_batch else t)
    tn = n_out // 2
    tiles_per_batch = seq_len // tm
    mod_idx = (lambda j, i: (i // tiles_per_batch, 0, 0)) if per_batch else (lambda j, i: (0, 0, 0))
    return pl.pallas_call(
        functools.partial(_inproj_kernel, gates=gates),
        grid=(n_out // tn, t // tm),
        in_specs=[pl.BlockSpec((tm, D_MODEL), lambda j, i: (i, 0)),
                  pl.BlockSpec((1, 3, D_MODEL), mod_idx),
                  pl.BlockSpec((1, D_MODEL, tn), lambda j, i: (l, 0, j))],
        out_specs=pl.BlockSpec((tm, tn), lambda j, i: (i, j)),
        out_shape=jax.ShapeDtypeStruct((t, n_out), BF16 if gates else F32),
        compiler_params=_cparams(("arbitrary", "arbitrary")),
        name="inproj_gates" if gates else "inproj",
    )(x2d, mod, w)


def _short_conv(x, w_ref, b_ref):
    n_tok = x.shape[0]
    row = lax.broadcasted_iota(jnp.int32, x.shape, 0)
    prev = jnp.where(row == 0, 0.0, pltpu.roll(x, 1, 0))
    nxt = jnp.where(row == n_tok - 1, 0.0, pltpu.roll(x, n_tok - 1, 0))
    return prev * w_ref[0:1, :] + x * w_ref[1:2, :] + nxt * w_ref[2:3, :] + b_ref[...]


def _hyena_kernel(*refs, conv_z, tk):
    if conv_z:
        (zin_ref, gin_ref, wz_ref, bz_ref, wg_ref, bg_ref, kr_ref, ki_ref, kny_ref, skip_ref,
         c_ref, s_ref, o_ref, zb_ref, acc_ref) = refs
    else:
        (zin_ref, gin_ref, wg_ref, bg_ref, kr_ref, ki_ref, kny_ref, skip_ref,
         c_ref, s_ref, o_ref, zb_ref, acc_ref) = refs
    nb, n_tok, ct = zin_ref.shape
    for i in range(nb):
        z = zin_ref[i]
        if conv_z:
            z = _short_conv(z, wz_ref, bz_ref)
        zb_ref[...] = z.astype(BF16)
        for j in range(n_tok // tk):
            rows = slice(j * tk, (j + 1) * tk)
            zr = _dot(c_ref[rows, :], zb_ref[...])
            zs = _dot(s_ref[rows, :], zb_ref[...])
            kr = kr_ref[0, 0, rows, :]
            ki = ki_ref[0, 0, rows, :]
            yr = (zr * kr + zs * ki).astype(BF16)
            ym = (zs * kr - zr * ki).astype(BF16)
            part = _dot(c_ref[:, rows], yr) + _dot(s_ref[:, rows], ym)
            if j == 0:
                acc_ref[...] = part
            else:
                acc_ref[...] += part
        row = lax.broadcasted_iota(jnp.int32, z.shape, 0)
        sign = jnp.where((row & 1) == 0, 1.0, -1.0)
        zny = jnp.sum(z * sign, axis=0, keepdims=True)
        y = acc_ref[...] + sign * (zny * kny_ref[0, 0]) + z * skip_ref[0, 0]
        o_ref[i] = _short_conv(gin_ref[i], wg_ref, bg_ref) * y


def _hyena_order(zin, zblk, gin, gblk, conv_w, conv_b, kr, ki, kny, skip, l, order, conv_z):
    b, n_tok = zin.shape[0], zin.shape[1]
    cmat, smat = _dft_tables(n_tok)
    if n_tok >= 2048:
        ct, nb = 256, 1
    else:
        ct, nb = 512, min(8, b)
    nc = HY_W // ct
    tk = min(256, n_tok)
    const = lambda shape: pl.BlockSpec(shape, lambda c, i: (0,) * len(shape),
                                       pipeline_mode=pl.Buffered(1))
    zspec = pl.BlockSpec((nb, n_tok, ct), lambda c, i: (i, 0, zblk * nc + c))
    gspec = pl.BlockSpec((nb, n_tok, ct), lambda c, i: (i, 0, gblk * nc + c))
    wspec = lambda blk: pl.BlockSpec((1, 3, ct), lambda c, i: (l, 0, blk * nc + c))
    bspec = lambda blk: pl.BlockSpec((1, 1, ct), lambda c, i: (l, 0, blk * nc + c))
    kspec = pl.BlockSpec((1, 1, n_tok, ct), lambda c, i: (l, order, 0, c))
    in_specs = [zspec, gspec]
    args = [zin, gin]
    if conv_z:
        in_specs += [wspec(0), bspec(0)]
        args += [conv_w, conv_b]
    in_specs += [wspec(order + 1), bspec(order + 1), kspec, kspec,
                 pl.BlockSpec((1, 1, 1, ct), lambda c, i: (l, order, 0, c)),
                 pl.BlockSpec((1, 1, 1, ct), lambda c, i: (l, order, 0, c)),
                 const((n_tok, n_tok)), const((n_tok, n_tok))]
    args += [conv_w, conv_b, kr, ki, kny, skip, cmat, smat]

    def body(*refs):
        refs = list(refs)
        lo = 2
        n_small = 4 if conv_z else 2
        for k in range(lo, lo + n_small):
            refs[k] = refs[k].at[0]
        _hyena_kernel(*refs, conv_z=conv_z, tk=tk)

    return pl.pallas_call(
        body,
        grid=(nc, b // nb),
        in_specs=in_specs,
        out_specs=pl.BlockSpec((nb, n_tok, ct), lambda c, i: (i, 0, c)),
        out_shape=jax.ShapeDtypeStruct((b, n_tok, HY_W), F32),
        scratch_shapes=[pltpu.VMEM((n_tok, ct), BF16), pltpu.VMEM((n_tok, ct), F32)],
        compiler_params=_cparams(("arbitrary", "arbitrary")),
        name=f"hyena_order{order}",
    )(*args)


def _rms(x, g, n):
    ms = jnp.sum(x * x, axis=-1, keepdims=True) * (1.0 / n)
    return x * lax.rsqrt(ms + RMS_EPS) * g


def _da_values_ext(v):
    lane = lax.broadcasted_iota(jnp.int32, (v.shape[0], LANES), 1)
    ones_col = jnp.where(lane == 0, 1.0, 0.0).astype(BF16)
    pieces = []
    for h in range(DA_HEADS):
        pieces += [v[:, h * DA_V_DIM:(h + 1) * DA_V_DIM], ones_col]
    return jnp.concatenate(pieces, axis=1)


def _mla_values_ext(ckv_b, wv):
    v = _dot(ckv_b, wv)
    lane = lax.broadcasted_iota(jnp.int32, v.shape, 1)
    return jnp.where((lane & (MLA_V_PAD - 1)) == MLA_V, 1.0, v).astype(BF16)


def _prep_kernel(*refs, rope, n_state_in):
    it = iter(refs)
    q_ref, k_ref, v_ref, sm_ref, qn_ref, wuq_ref, kvn_ref, wk_ref, wv_ref, e_ref = (
        next(it) for _ in range(10))
    tab_ref = next(it) if rope else None
    for _ in range(n_state_in):
        next(it)
    qd_ref, kd_ref, vd_ref, qm_ref, km_ref, vm_ref = (next(it) for _ in range(6))
    q = q_ref[...] * (DA_HEAD_DIM ** -0.5)
    k = k_ref[...]
    v = v_ref[...]
    dq = sm_ref[:, 0:MLA_Q_LORA]
    dkv = sm_ref[:, MLA_Q_LORA:MLA_Q_LORA + MLA_KV_LORA]
    kr = sm_ref[:, 5 * LANES:6 * LANES]
    c_kv = _rms(dkv, kvn_ref[0], MLA_KV_LORA)
    if not rope:
        ks_ref, vs_ref, ckvs_ref, krs_ref = (next(it) for _ in range(4))
        nbt, _, n_tok, _ = ks_ref.shape
        ks_ref[:, 0] = k.reshape(nbt, n_tok, k.shape[1])
        vs_ref[:, 0] = v.reshape(nbt, n_tok, v.shape[1])
        ckvs_ref[:, 0] = c_kv.reshape(nbt, n_tok, MLA_KV_LORA)
        krs_ref[:, 0] = kr[:, :MLA_ROPE].reshape(nbt, n_tok, MLA_ROPE)
    if rope:
        q = _rope_apply(q, tab_ref, 0, DA_HEAD_DIM // 4)
        k = _rope_apply(k, tab_ref, 0, DA_HEAD_DIM // 4)
    qd_ref[...] = q.astype(BF16)
    kd_ref[...] = k.astype(BF16)
    vd_ref[...] = _da_values_ext(v.astype(BF16))

    c_q = _rms(dq, qn_ref[0], MLA_Q_LORA)
    qm = _dot(c_q.astype(BF16), wuq_ref[0]) * ((MLA_NOPE + MLA_ROPE) ** -0.5)
    ckv_b = c_kv.astype(BF16)
    if rope:
        qm = _rope_apply(qm, tab_ref, 1, MLA_ROPE // 4)
        kr = _rope_apply(kr, tab_ref, 2, MLA_ROPE // 4)
    qm_ref[...] = qm.astype(BF16)
    km_ref[...] = (_dot(ckv_b, wk_ref[0]) + _dot(kr.astype(BF16), e_ref[...])).astype(BF16)
    vm_ref[...] = _mla_values_ext(ckv_b, wv_ref[0])


def _prep(proj, wts, l, seq_len, rope, states=None):
    t = proj.shape[0]
    depth = wts["ln_g"].shape[0]
    tm = min(512, seq_len if rope else t)
    tiles_per_seq = seq_len // tm
    blk = lambda w, idx: pl.BlockSpec((tm, w), lambda i: (i, idx))
    wl = lambda shape: pl.BlockSpec((1,) + shape, lambda i: (l, 0, 0))
    in_specs = [blk(512, COL_Q // 512), blk(512, COL_K // 512), blk(512, COL_V // 512),
                blk(SMALL_W, COL_SMALL // SMALL_W),
                wl((1, MLA_Q_LORA)), wl((MLA_Q_LORA, MLA_HEADS * MLA_HEAD_PAD)),
                wl((1, MLA_KV_LORA)), wl((MLA_KV_LORA, MLA_HEADS * MLA_HEAD_PAD)),
                wl((MLA_KV_LORA, MLA_HEADS * MLA_V_PAD)),
                pl.BlockSpec((LANES, MLA_HEADS * MLA_HEAD_PAD), lambda i: (0, 0))]
    args = [proj, proj, proj, proj, wts["q_norm"], wts["w_uq"], wts["kv_norm"], wts["w_uk"],
            wts["w_uv"], wts["e_place"]]
    if rope:
        in_specs.append(pl.BlockSpec((3, 3, tm, LANES), lambda i: (0, 0, i % tiles_per_seq, 0)))
        args.append(_rope_tables(seq_len))
    out = lambda w: pl.BlockSpec((tm, w), lambda i: (i, 0))
    widths = (512, 512, DA_HEADS * 2 * DA_V_DIM, 1024, 1024, MLA_HEADS * MLA_V_PAD)
    out_specs = [out(w) for w in widths]
    out_shape = [jax.ShapeDtypeStruct((t, w), BF16) for w in widths]
    aliases = {}
    n_state_in = 0
    if not rope:
        nbt = tm // seq_len
        for w in (512, 512, MLA_KV_LORA, MLA_ROPE):
            out_specs.append(pl.BlockSpec((nbt, 1, seq_len, w), lambda i: (i, l, 0, 0)))
            out_shape.append(jax.ShapeDtypeStruct((t // seq_len, depth, seq_len, w), F32))
        if states is not None:
            n_state_in = len(states)
            for k, s in enumerate(states):
                aliases[len(args)] = 6 + k
                in_specs.append(pl.BlockSpec(memory_space=pl.ANY))
                args.append(s)
    return pl.pallas_call(
        functools.partial(_prep_kernel, rope=rope, n_state_in=n_state_in),
        grid=(t // tm,),
        in_specs=in_specs,
        out_specs=out_specs,
        out_shape=out_shape,
        input_output_aliases=aliases,
        compiler_params=_cparams(("arbitrary",)),
        name="attn_prep",
    )(*args)


def _prep_cache_kernel(k_ref, v_ref, ckv_ref, kr_ref, wk_ref, wv_ref, e_ref,
                       kd_ref, vd_ref, km_ref, vm_ref):
    kd_ref[0] = k_ref[0, 0].astype(BF16)
    vd_ref[0] = _da_values_ext(v_ref[0, 0].astype(BF16))
    ckv_b = ckv_ref[0, 0].astype(BF16)
    km_ref[0] = (_dot(ckv_b, wk_ref[0]) + _dot(kr_ref[0, 0].astype(BF16), e_ref[...])).astype(BF16)
    vm_ref[0] = _mla_values_ext(ckv_b, wv_ref[0])


def _prep_cache(cache_k, cache_v, cache_ckv, cache_kr, wts, l):
    b, _, p = cache_k.shape[:3]
    ck = cache_k.reshape(b, -1, p, 512)
    cv = cache_v.reshape(b, -1, p, 512)
    cin = lambda w: pl.BlockSpec((1, 1, p, w), lambda i: (i, l, 0, 0))
    wl = lambda shape: pl.BlockSpec((1,) + shape, lambda i: (l, 0, 0))
    out = lambda w: pl.BlockSpec((1, p, w), lambda i: (i, 0, 0))
    return pl.pallas_call(
        _prep_cache_kernel,
        grid=(b,),
        in_specs=[cin(512), cin(512), cin(MLA_KV_LORA), cin(MLA_ROPE),
                  wl((MLA_KV_LORA, MLA_HEADS * MLA_HEAD_PAD)), wl((MLA_KV_LORA, MLA_HEADS * MLA_V_PAD)),
                  pl.BlockSpec((MLA_ROPE, MLA_HEADS * MLA_HEAD_PAD), lambda i: (0, 0))],
        out_specs=[out(512), out(1024), out(1024), out(MLA_HEADS * MLA_V_PAD)],
        out_shape=[jax.ShapeDtypeStruct((b, p, 512), BF16), jax.ShapeDtypeStruct((b, p, 1024), BF16),
                   jax.ShapeDtypeStruct((b, p, 1024), BF16),
                   jax.ShapeDtypeStruct((b, p, MLA_HEADS * MLA_V_PAD), BF16)],
        compiler_params=_cparams(("arbitrary",)),
        name="attn_prep_cache",
    )(ck, cv, cache_ckv, cache_kr, wts["w_uk"], wts["w_uv"], wts["e_place"][:MLA_ROPE])


def _softmax_pv(q, k_parts, v_parts, dv, den_on_mxu):
    s = [_dot_nt(q, k) for k in k_parts]
    m = s[0].max(axis=-1, keepdims=True)
    for sp in s[1:]:
        m = jnp.maximum(m, sp.max(axis=-1, keepdims=True))
    o = None
    den = None
    for sp, v in zip(s, v_parts):
        if den_on_mxu:
            pv = _dot(jnp.exp((sp - m).astype(BF16)), v)
        else:
            p = jnp.exp(sp - m)
            psum = p.sum(axis=-1, keepdims=True)
            den = psum if den is None else den + psum
            pv = _dot(p.astype(BF16), v[:, :dv])
        o = pv if o is None else o + pv
    if den_on_mxu:
        return o[:, :dv] / o[:, dv:dv + 1]
    return o / den


def _da_kernel(*refs, lam_init, has_ctx):
    if has_ctx:
        q_ref, kn_ref, vn_ref, kc_ref, vc_ref, lam_ref, g_ref, o_ref = refs
    else:
        q_ref, kn_ref, vn_ref, lam_ref, g_ref, o_ref = refs
    lp = lam_ref[0]
    lam = (jnp.exp(jnp.sum(lp[0:1] * lp[1:2], axis=1, keepdims=True))
           - jnp.exp(jnp.sum(lp[2:3] * lp[3:4], axis=1, keepdims=True)) + lam_init)
    for bi in range(q_ref.shape[0]):
        for h in range(DA_HEADS):
            vcols = slice(h * 2 * DA_V_DIM, (h + 1) * 2 * DA_V_DIM)
            outs = []
            for j in range(2):
                cols = slice((j * DA_HEADS + h) * DA_HEAD_DIM, (j * DA_HEADS + h + 1) * DA_HEAD_DIM)
                k_parts = [kn_ref[bi, :, cols]]
                v_parts = [vn_ref[bi, :, vcols]]
                if has_ctx:
                    k_parts.insert(0, kc_ref[bi, :, cols])
                    v_parts.insert(0, vc_ref[bi, :, vcols])
                outs.append(_softmax_pv(q_ref[bi, :, cols], k_parts, v_parts, DA_V_DIM, has_ctx))
            o = outs[0] - lam * outs[1]
            o_ref[bi, :, h * DA_V_DIM:(h + 1) * DA_V_DIM] = (
                _rms(o, g_ref[0], DA_V_DIM) * (1.0 - lam_init))


def _da_attention(qd, kd, vd, ctx, da_lambda, da_subln, l, tq, nbq):
    b, n, _ = qd.shape
    has_ctx = ctx is not None
    lam_init = 0.8 - 0.6 * math.exp(-0.3 * l)
    full = lambda a: pl.BlockSpec((nbq,) + a.shape[1:], lambda i, j: (i, 0, 0))
    in_specs = [pl.BlockSpec((nbq, tq, 512), lambda i, j: (i, j, 0)), full(kd), full(vd)]
    args = [qd, kd, vd]
    if has_ctx:
        in_specs += [full(ctx[0]), full(ctx[1])]
        args += [ctx[0], ctx[1]]
    in_specs += [pl.BlockSpec((1, 4, DA_HEAD_DIM), lambda i, j: (l, 0, 0)),
                 pl.BlockSpec((1, 1, DA_V_DIM), lambda i, j: (l, 0, 0))]
    args += [da_lambda, da_subln.reshape(-1, 1, DA_V_DIM)]
    return pl.pallas_call(
        functools.partial(_da_kernel, lam_init=lam_init, has_ctx=has_ctx),
        grid=(b // nbq, n // tq),
        in_specs=in_specs,
        out_specs=pl.BlockSpec((nbq, tq, 512), lambda i, j: (i, j, 0)),
        out_shape=jax.ShapeDtypeStruct((b, n, 512), F32),
        compiler_params=_cparams(("arbitrary", "arbitrary")),
        name="diff_attention",
    )(*args)


def _mla_kernel(*refs, has_ctx):
    if has_ctx:
        q_ref, kn_ref, vn_ref, kc_ref, vc_ref, o_ref = refs
    else:
        q_ref, kn_ref, vn_ref, o_ref = refs
    for bi in range(q_ref.shape[0]):
        for h in range(MLA_HEADS):
            cols = slice(h * MLA_HEAD_PAD, (h + 1) * MLA_HEAD_PAD)
            vcols = slice(h * MLA_V_PAD, (h + 1) * MLA_V_PAD)
            k_parts = [kn_ref[bi, :, cols]]
            v_parts = [vn_ref[bi, :, vcols]]
            if has_ctx:
                k_parts.insert(0, kc_ref[bi, :, cols])
                v_parts.insert(0, vc_ref[bi, :, vcols])
            o_ref[bi, :, h * MLA_V:(h + 1) * MLA_V] = _softmax_pv(
                q_ref[bi, :, cols], k_parts, v_parts, MLA_V, has_ctx)


def _mla_attention(qm, km, vm, ctx, tq, nbq):
    b, n, _ = qm.shape
    has_ctx = ctx is not None
    full = lambda a: pl.BlockSpec((nbq,) + a.shape[1:], lambda i, j: (i, 0, 0))
    in_specs = [pl.BlockSpec((nbq, tq, 1024), lambda i, j: (i, j, 0)), full(km), full(vm)]
    args = [qm, km, vm]
    if has_ctx:
        in_specs += [full(ctx[0]), full(ctx[1])]
        args += [ctx[0], ctx[1]]
    return pl.pallas_call(
        functools.partial(_mla_kernel, has_ctx=has_ctx),
        grid=(b // nbq, n // tq),
        in_specs=in_specs,
        out_specs=pl.BlockSpec((nbq, tq, 512), lambda i, j: (i, j, 0)),
        out_shape=jax.ShapeDtypeStruct((b, n, 512), F32),
        compiler_params=_cparams(("arbitrary", "arbitrary")),
        name="latent_attention",
    )(*args)


def _outproj_kernel(x_ref, mod_ref, yh_ref, yd_ref, ym_ref, p0_ref, p1_ref, p2_ref,
                    m0_ref, m1_ref, m2_ref, wb_ref, wo_ref, g_ref, b_ref, o_ref, *, alpha):
    merged = None
    for y_ref, p_ref, m_ref, n in ((yh_ref, p0_ref, m0_ref, 0), (yd_ref, p1_ref, m1_ref, 1),
                                   (ym_ref, p2_ref, m2_ref, 2)):
        br = (y_ref[...] * p_ref[...].astype(F32)).astype(BF16)
        term = m_ref[...].astype(F32) * _dot(br, wb_ref[0, n])
        merged = term if merged is None else merged + term
    out = _dot(merged.astype(BF16), wo_ref[0])
    gate = mod_ref[0, 2:3, :]
    y = alpha * x_ref[...] + gate * out
    mu = jnp.mean(y, axis=-1, keepdims=True)
    yc = y - mu
    var = jnp.mean(yc * yc, axis=-1, keepdims=True)
    o_ref[...] = yc * lax.rsqrt(var + LN_EPS) * g_ref[0] + b_ref[0]


def _outproj(x2d, mod, y_hy, y_da, y_mla, gates, w_branch, w_out, ln_g, ln_b, l, seq_len, alpha):
    t = x2d.shape[0]
    per_batch = mod.shape[0] > 1
    tm = min(256, seq_len if per_batch else t)
    tiles_per_batch = seq_len // tm
    mod_idx = (lambda i: (i // tiles_per_batch, 0, 0)) if per_batch else (lambda i: (0, 0, 0))
    row = lambda w, idx=0: pl.BlockSpec((tm, w), lambda i: (i, idx))
    depth = ln_g.shape[0]
    return pl.pallas_call(
        functools.partial(_outproj_kernel, alpha=alpha),
        grid=(t // tm,),
        in_specs=[row(D_MODEL), pl.BlockSpec((1, 3, D_MODEL), mod_idx),
                  row(BR_W), row(BR_W), row(BR_W),
                  row(BR_W, GATE_PATHS // BR_W), row(BR_W, GATE_PATHS // BR_W + 1),
                  row(BR_W, GATE_PATHS // BR_W + 2),
                  row(D_MODEL, GATE_MERGE // D_MODEL), row(D_MODEL, GATE_MERGE // D_MODEL + 1),
                  row(D_MODEL, GATE_MERGE // D_MODEL + 2),
                  pl.BlockSpec((1, N_BRANCH, BR_W, D_MODEL), lambda i: (l, 0, 0, 0)),
                  pl.BlockSpec((1, D_MODEL, D_MODEL), lambda i: (l, 0, 0)),
                  pl.BlockSpec((1, 1, D_MODEL), lambda i: (l, 0, 0)),
                  pl.BlockSpec((1, 1, D_MODEL), lambda i: (l, 0, 0))],
        out_specs=row(D_MODEL),
        out_shape=jax.ShapeDtypeStruct((t, D_MODEL), F32),
        compiler_params=_cparams(("arbitrary",)),
        name="outproj",
    )(x2d, mod, y_hy, y_da, y_mla, gates, gates, gates, gates, gates, gates, w_branch, w_out,
      ln_g.reshape(depth, 1, D_MODEL), ln_b.reshape(depth, 1, D_MODEL))


def _layer(x, mod, l, wts, spectra, ctx=None, states=None):
    b, n, _ = x.shape
    depth = wts["ln_g"].shape[0]
    alpha = (2 * depth) ** 0.25
    x2d = x.reshape(b * n, D_MODEL)
    proj = _inproj(x2d, mod, wts["w_in"], l, n, gates=False)
    gates = _inproj(x2d, mod, wts["w_gate"], l, n, gates=True)
    proj3 = proj.reshape(b, n, D_IN_A)

    kr_s, ki_s, kny_s = spectra
    z2 = _hyena_order(proj3, 0, proj3, 1, wts["conv_w"], wts["conv_b"], kr_s, ki_s, kny_s,
                      wts["skip"], l, 0, True)
    y_hy = _hyena_order(z2, 0, proj3, 2, wts["conv_w"], wts["conv_b"], kr_s, ki_s, kny_s,
                        wts["skip"], l, 1, False)

    prep_out = _prep(proj, wts, l, n, rope=ctx is not None, states=states)
    r3 = lambda a: a.reshape(b, n, a.shape[-1])
    qd, kd, vd, qm, km, vm = map(r3, prep_out[:6])
    new_states = tuple(prep_out[6:]) if ctx is None else None
    if ctx is not None:
        kd_c, vd_c, km_c, vm_c = _prep_cache(*ctx, wts, l)
        da_ctx, mla_ctx = (kd_c, vd_c), (km_c, vm_c)
    else:
        da_ctx = mla_ctx = None
    tq = min(ATTN_TQ, n)
    nbq = 1
    y_da = _da_attention(qd, kd, vd, da_ctx, wts["da_lambda"], wts["da_subln"], l, tq, nbq)
    y_mla = _mla_attention(qm, km, vm, mla_ctx, tq, nbq)

    x_new = _outproj(x2d, mod, y_hy.reshape(b * n, HY_W), y_da.reshape(b * n, BR_W),
                     y_mla.reshape(b * n, BR_W), gates, wts["w_branch"], wts["w_out"],
                     wts["ln_g"], wts["ln_b"], l, n, alpha)
    return x_new.reshape(b, n, D_MODEL), new_states


def _prepare_weights(w_in, hy_conv_w, hy_conv_b, hy_ffn_w1, hy_skip, da_lambda, da_subln,
                     mla_q_norm, mla_w_uq, mla_kv_norm, mla_w_ukv, w_branch, w_out, ln_g, ln_b):
    depth = w_in.shape[0]
    w_in_p = jnp.pad(w_in[..., :ORIG_PATHS].astype(BF16),
                     ((0, 0), (0, 0), (0, SMALL_W - SMALL_USED)))
    w_gate = jnp.concatenate([w_in[..., ORIG_MERGE:ORIG_END], w_in[..., ORIG_PATHS:ORIG_MERGE]],
                             axis=-1).astype(BF16)
    uq = mla_w_uq.reshape(depth, MLA_Q_LORA, MLA_HEADS, MLA_NOPE + MLA_ROPE)
    uq = jnp.pad(uq, ((0, 0), (0, 0), (0, 0), (0, MLA_HEAD_PAD - MLA_NOPE - MLA_ROPE)))
    w_uq = uq.reshape(depth, MLA_Q_LORA, MLA_HEADS * MLA_HEAD_PAD).astype(BF16)
    ukv = mla_w_ukv.reshape(depth, MLA_KV_LORA, MLA_HEADS, MLA_NOPE + MLA_V)
    uk = jnp.pad(ukv[..., :MLA_NOPE], ((0, 0), (0, 0), (0, 0), (0, MLA_HEAD_PAD - MLA_NOPE)))
    w_uk = uk.reshape(depth, MLA_KV_LORA, MLA_HEADS * MLA_HEAD_PAD).astype(BF16)
    uv = jnp.pad(ukv[..., MLA_NOPE:], ((0, 0), (0, 0), (0, 0), (0, MLA_V_PAD - MLA_V)))
    w_uv = uv.reshape(depth, MLA_KV_LORA, MLA_HEADS * MLA_V_PAD).astype(BF16)
    e = np.zeros((LANES, MLA_HEADS * MLA_HEAD_PAD), np.float32)
    for h in range(MLA_HEADS):
        for i in range(MLA_ROPE):
            e[i, h * MLA_HEAD_PAD + MLA_NOPE + i] = 1.0
    w1p = jnp.pad(hy_ffn_w1, ((0, 0), (0, LANES - hy_ffn_w1.shape[1]), (0, 0)))
    return {
        "w_in": w_in_p, "w_gate": w_gate, "conv_w": hy_conv_w, "conv_b": hy_conv_b.reshape(depth, 1, -1),
        "skip": hy_skip.reshape(depth, HY_ORDER, 1, HY_W), "w1p": w1p,
        "da_lambda": da_lambda, "da_subln": da_subln,
        "q_norm": mla_q_norm.reshape(depth, 1, -1), "kv_norm": mla_kv_norm.reshape(depth, 1, -1),
        "w_uq": w_uq, "w_uk": w_uk, "w_uv": w_uv, "e_place": jnp.asarray(e, dtype=BF16),
        "w_branch": w_branch.astype(BF16), "w_out": w_out.astype(BF16), "ln_g": ln_g, "ln_b": ln_b,
    }


def kernel(x_prompt, x_sample, c, cache_diff_k, cache_diff_v, cache_mla_ckv, cache_mla_krope, c_ctx, w_mod, b_mod, w_in, hy_conv_w, hy_conv_b, hy_ffn_w1, hy_ffn_b1, hy_ffn_w2, hy_ffn_b2, hy_ffn_w3, hy_sin_freq, hy_log_decay, hy_skip, da_lambda, da_subln, mla_q_norm, mla_w_uq, mla_kv_norm, mla_w_ukv, w_branch, w_out, ln_g, ln_b):
    depth = w_in.shape[0]
    b_ctx, n_ctx, _ = x_prompt.shape
    b_lat, n_lat, _ = x_sample.shape
    wts = _prepare_weights(w_in, hy_conv_w, hy_conv_b, hy_ffn_w1, hy_skip, da_lambda, da_subln,
                           mla_q_norm, mla_w_uq, mla_kv_norm, mla_w_ukv, w_branch, w_out, ln_g, ln_b)

    rows = -(-(1 + b_lat) // 8) * 8
    cvec = jnp.zeros((rows, D_MODEL), F32).at[0].set(c_ctx).at[1:1 + b_lat].set(c)
    mods = _modulation(cvec, w_mod, b_mod).reshape(depth, rows, 3, D_MODEL)

    spec_args = (wts["w1p"], hy_ffn_b1, hy_ffn_w2, hy_ffn_b2, hy_ffn_w3, hy_sin_freq, hy_log_decay)
    spectra_ctx = _hyena_spectra(n_ctx, *spec_args)
    spectra_lat = spectra_ctx if n_lat == n_ctx else _hyena_spectra(n_lat, *spec_args)

    xp = x_prompt
    states = None
    for l in range(depth):
        xp, states = _layer(xp, mods[l, 0:1], l, wts, spectra_ctx, states=states)
    state_k = states[0].reshape(b_ctx, depth, n_ctx, 2, DA_HEADS, DA_HEAD_DIM)
    state_v = states[1].reshape(b_ctx, depth, n_ctx, DA_HEADS, DA_V_DIM)
    state_ckv, state_kr = states[2], states[3]

    xs = x_sample
    ctx = (cache_diff_k, cache_diff_v, cache_mla_ckv, cache_mla_krope)
    for l in range(depth):
        xs, _ = _layer(xs, mods[l, 1:1 + b_lat], l, wts, spectra_lat, ctx=ctx)

    return (xp, xs, state_k, state_v, state_ckv, state_kr)
```

```python
import functools
import math

import numpy as np
import jax
import jax.numpy as jnp
from jax import lax
from jax.experimental import pallas as pl
from jax.experimental.pallas import tpu as pltpu

F32 = jnp.float32
BF16 = jnp.bfloat16

D_MODEL = 1024
N_BRANCH = 3
BR_W = 512
HY_W = 512
HY_ORDER = 2
HY_BANDS = 16
HY_FFN = 64
DA_HEADS = 4
DA_HEAD_DIM = 64
DA_V_DIM = 2 * DA_HEAD_DIM
MLA_HEADS = 8
MLA_NOPE = 64
MLA_ROPE = 32
MLA_V = 64
MLA_Q_LORA = 384
MLA_KV_LORA = 256
GRID_W = 64
ROPE_BASE = 10000.0
LN_EPS = 1e-5
RMS_EPS = 1e-6

LANES = 128
MLA_HEAD_PAD = LANES
MLA_V_PAD = 2 * LANES
VMEM_LIMIT = 56 * 1024 * 1024
ATTN_TQ = 512

COL_HY = 0
COL_Q = 3 * HY_W
COL_K = COL_Q + 2 * DA_HEADS * DA_HEAD_DIM
COL_V = COL_K + 2 * DA_HEADS * DA_HEAD_DIM
COL_SMALL = COL_V + DA_HEADS * DA_V_DIM
SMALL_W = 1024
SMALL_USED = MLA_Q_LORA + MLA_KV_LORA + MLA_ROPE
D_IN_A = COL_SMALL + SMALL_W
GATE_MERGE = 0
GATE_PATHS = N_BRANCH * D_MODEL
D_IN_G = GATE_PATHS + N_BRANCH * BR_W
ORIG_PATHS = COL_SMALL + SMALL_USED
ORIG_MERGE = ORIG_PATHS + N_BRANCH * BR_W
ORIG_END = ORIG_MERGE + N_BRANCH * D_MODEL


def _cparams(sem):
    return pltpu.CompilerParams(dimension_semantics=sem, vmem_limit_bytes=VMEM_LIMIT)


def _split_bf16(x):
    hi = x.astype(BF16)
    lo = (x - hi.astype(F32)).astype(BF16)
    return hi, lo


def _dot(a, b):
    return jnp.dot(a, b, preferred_element_type=F32)


def _dot_nt(a, b):
    return lax.dot_general(a, b, (((1,), (1,)), ((), ())), preferred_element_type=F32)


def _dot3(a, b):
    ah, al = _split_bf16(a)
    bh, bl = _split_bf16(b)
    return _dot(ah, bh) + _dot(ah, bl) + _dot(al, bh)


def _sigmoid(x):
    return 1.0 / (1.0 + jnp.exp(-x))


@functools.lru_cache(maxsize=None)
def _dft_tables(n_tok):
    k = np.arange(n_tok, dtype=np.int64)
    ang = np.pi * ((np.outer(k, k) % (2 * n_tok)).astype(np.float64)) / n_tok
    return (jnp.asarray(np.cos(ang), dtype=BF16), jnp.asarray(np.sin(ang), dtype=BF16))


@functools.lru_cache(maxsize=None)
def _hyena_feats(n_tok):
    t = np.arange(n_tok, dtype=np.float64)
    t_lin = t / (n_tok - 1)
    bands = np.arange(1, HY_BANDS + 1, dtype=np.float64)
    ang = (2.0 * np.pi / n_tok) * t[:, None] * bands
    feats = np.concatenate([t_lin[:, None], np.cos(ang), -np.sin(ang)], axis=-1)
    out = np.zeros((n_tok, LANES), np.float64)
    out[:, :feats.shape[1]] = feats
    return jnp.asarray(out, dtype=F32)


def _rope_pattern(pos_row, pos_col, dd):
    q = dd // 4
    freqs = 1.0 / (ROPE_BASE ** (np.arange(q, dtype=np.float64) / q))
    n = pos_row.shape[0]
    cos = np.zeros((n, dd)); sa = np.zeros((n, dd)); sb = np.zeros((n, dd))
    for g, pos in enumerate((pos_row, pos_col)):
        ang = pos[:, None].astype(np.float64) * freqs
        base = g * 2 * q
        cos[:, base:base + q] = np.cos(ang); cos[:, base + q:base + 2 * q] = np.cos(ang)
        sa[:, base:base + q] = -np.sin(ang)
        sb[:, base + q:base + 2 * q] = np.sin(ang)
    return cos, sa, sb


@functools.lru_cache(maxsize=None)
def _rope_tables(n_tok):
    t = np.arange(n_tok)
    row, col = t // GRID_W, t % GRID_W
    ones = np.ones((n_tok, LANES)); zeros = np.zeros((n_tok, LANES))
    c, a, b = _rope_pattern(row, col, DA_HEAD_DIM)
    da = [np.tile(c, (1, 2)), np.tile(a, (1, 2)), np.tile(b, (1, 2))]
    c, a, b = _rope_pattern(row, col, MLA_ROPE)
    mq = [ones.copy(), zeros.copy(), zeros.copy()]
    kr = [ones.copy(), zeros.copy(), zeros.copy()]
    for dst, src in zip(mq, (c, a, b)):
        dst[:, MLA_NOPE:MLA_NOPE + MLA_ROPE] = src
    for dst, src in zip(kr, (c, a, b)):
        dst[:, :MLA_ROPE] = src
    tab = np.stack([np.stack(da), np.stack(mq), np.stack(kr)])
    return jnp.asarray(tab, dtype=F32)


def _rope_apply(x, tab_ref, kind, q):
    cos = tab_ref[kind, 0]
    sa = tab_ref[kind, 1]
    sb = tab_ref[kind, 2]
    outs = []
    for c in range(x.shape[1] // LANES):
        xc = x[:, c * LANES:(c + 1) * LANES]
        outs.append(xc * cos + pltpu.roll(xc, LANES - q, 1) * sa + pltpu.roll(xc, q, 1) * sb)
    return outs[0] if len(outs) == 1 else jnp.concatenate(outs, axis=1)


def _mod_kernel(c_ref, w_ref, b_ref, o_ref):
    c = c_ref[...]
    o_ref[0] = _dot3(c * _sigmoid(c), w_ref[0]) + b_ref[0]


def _modulation(cvec, w_mod, b_mod):
    depth = w_mod.shape[0]
    rows = cvec.shape[0]
    return pl.pallas_call(
        _mod_kernel,
        grid=(depth, 3),
        in_specs=[pl.BlockSpec((rows, D_MODEL), lambda l, j: (0, 0)),
                  pl.BlockSpec((1, D_MODEL, D_MODEL), lambda l, j: (l, 0, j)),
                  pl.BlockSpec((1, 1, D_MODEL), lambda l, j: (l, 0, j))],
        out_specs=pl.BlockSpec((1, rows, D_MODEL), lambda l, j: (l, 0, j)),
        out_shape=jax.ShapeDtypeStruct((depth, rows, 3 * D_MODEL), F32),
        compiler_params=_cparams(("arbitrary", "arbitrary")),
        name="modulation",
    )(cvec, w_mod, b_mod.reshape(depth, 1, 3 * D_MODEL))


def _filt_kernel(feats_ref, w1_ref, b1_ref, w2_ref, b2_ref, w3f_ref, w3b_ref, sf_ref,
                 ldf_ref, ldb_ref, c_ref, s_ref, kr_ref, ki_ref, kny_ref, hid_ref):
    n_tok = feats_ref.shape[0]
    inv_n = 1.0 / (2 * n_tok)

    @pl.when((pl.program_id(1) == 0) & (pl.program_id(2) == 0))
    def _():
        sf = sf_ref[0]
        hid = jnp.sin(sf[0:1] * (_dot3(feats_ref[...], w1_ref[0]) + b1_ref[0]))
        hid_ref[...] = jnp.sin(sf[1:2] * (_dot3(hid, w2_ref[0]) + b2_ref[0]))

    hid = hid_ref[...]
    t_lin = feats_ref[:, 0:1]
    hf = _dot3(hid, w3f_ref[0]) * jnp.exp(-t_lin * jnp.exp(ldf_ref[0]))
    hb = _dot3(hid, w3b_ref[0]) * jnp.exp(-t_lin * jnp.exp(ldb_ref[0]))
    row = lax.broadcasted_iota(jnp.int32, hf.shape, 0)
    hb = jnp.where(row == 0, 0.0, hb)
    a = hf + hb
    d = hf - hb
    wk = jnp.where(row == 0, inv_n, 2.0 * inv_n)
    kr_ref[0, 0] = _dot(c_ref[...], a.astype(BF16)) * wk
    ki_ref[0, 0] = -_dot(s_ref[...], d.astype(BF16)) * wk
    sign = jnp.where((row & 1) == 0, 1.0, -1.0)
    kny_ref[0, 0] = jnp.sum(a * sign, axis=0, keepdims=True) * inv_n


def _hyena_spectra(n_tok, w1p, b1, w2, b2, w3, sf, log_decay):
    depth = w3.shape[0]
    ct = 256
    nc = HY_W // ct
    cmat, smat = _dft_tables(n_tok)
    feats = _hyena_feats(n_tok)
    const = lambda shape: pl.BlockSpec(shape, lambda l, n, c: (0,) * len(shape),
                                       pipeline_mode=pl.Buffered(1))
    per_l = lambda shape: pl.BlockSpec((1,) + shape, lambda l, n, c: (l, 0, 0))
    fcol = lambda l, n, c: (l, 0, n * nc + c)
    bcol = lambda l, n, c: (l, 0, HY_ORDER * nc + n * nc + c)
    out_spec = pl.BlockSpec((1, 1, n_tok, ct), lambda l, n, c: (l, n, 0, c))
    return pl.pallas_call(
        _filt_kernel,
        grid=(depth, HY_ORDER, nc),
        in_specs=[const((n_tok, LANES)),
                  per_l((LANES, HY_FFN)), per_l((1, HY_FFN)),
                  per_l((HY_FFN, HY_FFN)), per_l((1, HY_FFN)),
                  pl.BlockSpec((1, HY_FFN, ct), fcol), pl.BlockSpec((1, HY_FFN, ct), bcol),
                  per_l((2, HY_FFN)),
                  pl.BlockSpec((1, 1, ct), fcol), pl.BlockSpec((1, 1, ct), bcol),
                  const((n_tok, n_tok)), const((n_tok, n_tok))],
        out_specs=[out_spec, out_spec,
                   pl.BlockSpec((1, 1, 1, ct), lambda l, n, c: (l, n, 0, c))],
        out_shape=[jax.ShapeDtypeStruct((depth, HY_ORDER, n_tok, HY_W), F32),
                   jax.ShapeDtypeStruct((depth, HY_ORDER, n_tok, HY_W), F32),
                   jax.ShapeDtypeStruct((depth, HY_ORDER, 1, HY_W), F32)],
        scratch_shapes=[pltpu.VMEM((n_tok, HY_FFN), F32)],
        compiler_params=_cparams(("arbitrary",) * 3),
        name="hyena_spectra",
    )(feats, w1p, b1.reshape(depth, 1, HY_FFN), w2, b2.reshape(depth, 1, HY_FFN), w3, w3, sf,
      log_decay.reshape(depth, 1, -1), log_decay.reshape(depth, 1, -1), cmat, smat)


def _inproj_kernel(x_ref, mod_ref, w_ref, o_ref, *, gates):
    shift = mod_ref[0, 0:1, :]
    scale = mod_ref[0, 1:2, :]
    h = (x_ref[...] * (1.0 + scale) + shift).astype(BF16)
    acc = _dot(h, w_ref[0])
    if gates:
        tn = acc.shape[1]
        col = pl.program_id(0) * tn + lax.broadcasted_iota(jnp.int32, acc.shape, 1)
        sg = _sigmoid(acc)
        o_ref[...] = jnp.where(col >= N_BRANCH * D_MODEL, acc * sg, sg).astype(BF16)
    else:
        o_ref[...] = acc


def _inproj(x2d, mod, w, l, seq_len, gates):
    t = x2d.shape[0]
    n_out = w.shape[-1]
    per_batch = mod.shape[0] > 1
    tm = min(1024, seq_len if per_batch else t)
    tn = n_out // 2
    tiles_per_batch = seq_len // tm
    mod_idx = (lambda j, i: (i // tiles_per_batch, 0, 0)) if per_batch else (lambda j, i: (0, 0, 0))
    return pl.pallas_call(
        functools.partial(_inproj_kernel, gates=gates),
        grid=(n_out // tn, t // tm),
        in_specs=[pl.BlockSpec((tm, D_MODEL), lambda j, i: (i, 0)),
                  pl.BlockSpec((1, 3, D_MODEL), mod_idx),
                  pl.BlockSpec((1, D_MODEL, tn), lambda j, i: (l, 0, j))],
        out_specs=pl.BlockSpec((tm, tn), lambda j, i: (i, j)),
        out_shape=jax.ShapeDtypeStruct((t, n_out), BF16 if gates else F32),
        compiler_params=_cparams(("arbitrary", "arbitrary")),
        name="inproj_gates" if gates else "inproj",
    )(x2d, mod, w)


def _short_conv(x, w_ref, b_ref):
    n_tok = x.shape[0]
    row = lax.broadcasted_iota(jnp.int32, x.shape, 0)
    prev = jnp.where(row == 0, 0.0, pltpu.roll(x, 1, 0))
    nxt = jnp.where(row == n_tok - 1, 0.0, pltpu.roll(x, n_tok - 1, 0))
    return prev * w_ref[0:1, :] + x * w_ref[1:2, :] + nxt * w_ref[2:3, :] + b_ref[...]


def _hyena_kernel(*refs, conv_z, tk):
    if conv_z:
        (zin_ref, gin_ref, wz_ref, bz_ref, wg_ref, bg_ref, kr_ref, ki_ref, kny_ref, skip_ref,
         c_ref, s_ref, o_ref, zb_ref, acc_ref) = refs
    else:
        (zin_ref, gin_ref, wg_ref, bg_ref, kr_ref, ki_ref, kny_ref, skip_ref,
         c_ref, s_ref, o_ref, zb_ref, acc_ref) = refs
    nb, n_tok, ct = zin_ref.shape
    for i in range(nb):
        z = zin_ref[i]
        if conv_z:
            z = _short_conv(z, wz_ref, bz_ref)
        zb_ref[i] = z.astype(BF16)
        for j in range(n_tok // tk):
            rows = slice(j * tk, (j + 1) * tk)
            zr = _dot(c_ref[rows, :], zb_ref[i])
            zs = _dot(s_ref[rows, :], zb_ref[i])
            kr = kr_ref[0, 0, rows, :]
            ki = ki_ref[0, 0, rows, :]
            yr = (zr * kr + zs * ki).astype(BF16)
            ym = (zs * kr - zr * ki).astype(BF16)
            part = _dot(c_ref[:, rows], yr) + _dot(s_ref[:, rows], ym)
            if j == 0:
                acc_ref[i] = part
            else:
                acc_ref[i] += part
        row = lax.broadcasted_iota(jnp.int32, z.shape, 0)
        sign = jnp.where((row & 1) == 0, 1.0, -1.0)
        zny = jnp.sum(z * sign, axis=0, keepdims=True)
        y = acc_ref[i] + sign * (zny * kny_ref[0, 0]) + z * skip_ref[0, 0]
        o_ref[i] = _short_conv(gin_ref[i], wg_ref, bg_ref) * y


def _hyena_order(zin, zblk, gin, gblk, conv_w, conv_b, kr, ki, kny, skip, l, order, conv_z):
    b, n_tok = zin.shape[0], zin.shape[1]
    cmat, smat = _dft_tables(n_tok)
    if n_tok >= 2048:
        ct, nb = 256, 2
    else:
        ct, nb = 512, min(8, b)
    nc = HY_W // ct
    tk = min(256, n_tok)
    const = lambda shape: pl.BlockSpec(shape, lambda c, i: (0,) * len(shape),
                                       pipeline_mode=pl.Buffered(1))
    zspec = pl.BlockSpec((nb, n_tok, ct), lambda c, i: (i, 0, zblk * nc + c))
    gspec = pl.BlockSpec((nb, n_tok, ct), lambda c, i: (i, 0, gblk * nc + c))
    wspec = lambda blk: pl.BlockSpec((1, 3, ct), lambda c, i: (l, 0, blk * nc + c))
    bspec = lambda blk: pl.BlockSpec((1, 1, ct), lambda c, i: (l, 0, blk * nc + c))
    kspec = pl.BlockSpec((1, 1, n_tok, ct), lambda c, i: (l, order, 0, c),
                         pipeline_mode=pl.Buffered(1))
    in_specs = [zspec, gspec]
    args = [zin, gin]
    if conv_z:
        in_specs += [wspec(0), bspec(0)]
        args += [conv_w, conv_b]
    in_specs += [wspec(order + 1), bspec(order + 1), kspec, kspec,
                 pl.BlockSpec((1, 1, 1, ct), lambda c, i: (l, order, 0, c)),
                 pl.BlockSpec((1, 1, 1, ct), lambda c, i: (l, order, 0, c)),
                 const((n_tok, n_tok)), const((n_tok, n_tok))]
    args += [conv_w, conv_b, kr, ki, kny, skip, cmat, smat]

    def body(*refs):
        refs = list(refs)
        lo = 2
        n_small = 4 if conv_z else 2
        for k in range(lo, lo + n_small):
            refs[k] = refs[k].at[0]
        _hyena_kernel(*refs, conv_z=conv_z, tk=tk)

    return pl.pallas_call(
        body,
        grid=(nc, b // nb),
        in_specs=in_specs,
        out_specs=pl.BlockSpec((nb, n_tok, ct), lambda c, i: (i, 0, c)),
        out_shape=jax.ShapeDtypeStruct((b, n_tok, HY_W), F32),
        scratch_shapes=[pltpu.VMEM((nb, n_tok, ct), BF16), pltpu.VMEM((nb, n_tok, ct), F32)],
        compiler_params=_cparams(("arbitrary", "arbitrary")),
        name=f"hyena_order{order}",
    )(*args)


def _rms(x, g, n):
    ms = jnp.sum(x * x, axis=-1, keepdims=True) * (1.0 / n)
    return x * lax.rsqrt(ms + RMS_EPS) * g


def _da_values_ext(v):
    lane = lax.broadcasted_iota(jnp.int32, (v.shape[0], LANES), 1)
    ones_col = jnp.where(lane == 0, 1.0, 0.0).astype(BF16)
    pieces = []
    for h in range(DA_HEADS):
        pieces += [v[:, h * DA_V_DIM:(h + 1) * DA_V_DIM], ones_col]
    return jnp.concatenate(pieces, axis=1)


def _mla_values_ext(ckv_b, wv):
    v = _dot(ckv_b, wv)
    lane = lax.broadcasted_iota(jnp.int32, v.shape, 1)
    return jnp.where((lane & (MLA_V_PAD - 1)) == MLA_V, 1.0, v).astype(BF16)


def _prep_kernel(*refs, rope, n_state_in):
    it = iter(refs)
    q_ref, k_ref, v_ref, sm_ref, qn_ref, wuq_ref, kvn_ref, wk_ref, wv_ref, e_ref = (
        next(it) for _ in range(10))
    tab_ref = next(it) if rope else None
    for _ in range(n_state_in):
        next(it)
    qd_ref, kd_ref, vd_ref, qm_ref, km_ref, vm_ref = (next(it) for _ in range(6))
    q = q_ref[...] * (DA_HEAD_DIM ** -0.5)
    k = k_ref[...]
    v = v_ref[...]
    dq = sm_ref[:, 0:MLA_Q_LORA]
    dkv = sm_ref[:, MLA_Q_LORA:MLA_Q_LORA + MLA_KV_LORA]
    kr = sm_ref[:, 5 * LANES:6 * LANES]
    c_kv = _rms(dkv, kvn_ref[0], MLA_KV_LORA)
    if not rope:
        ks_ref, vs_ref, ckvs_ref, krs_ref = (next(it) for _ in range(4))
        nbt, _, n_tok, _ = ks_ref.shape
        ks_ref[:, 0] = k.reshape(nbt, n_tok, k.shape[1])
        vs_ref[:, 0] = v.reshape(nbt, n_tok, v.shape[1])
        ckvs_ref[:, 0] = c_kv.reshape(nbt, n_tok, MLA_KV_LORA)
        krs_ref[:, 0] = kr[:, :MLA_ROPE].reshape(nbt, n_tok, MLA_ROPE)
    if rope:
        q = _rope_apply(q, tab_ref, 0, DA_HEAD_DIM // 4)
        k = _rope_apply(k, tab_ref, 0, DA_HEAD_DIM // 4)
    qd_ref[...] = q.astype(BF16)
    kd_ref[...] = k.astype(BF16)
    vd_ref[...] = _da_values_ext(v.astype(BF16))

    c_q = _rms(dq, qn_ref[0], MLA_Q_LORA)
    qm = _dot(c_q.astype(BF16), wuq_ref[0]) * ((MLA_NOPE + MLA_ROPE) ** -0.5)
    ckv_b = c_kv.astype(BF16)
    if rope:
        qm = _rope_apply(qm, tab_ref, 1, MLA_ROPE // 4)
        kr = _rope_apply(kr, tab_ref, 2, MLA_ROPE // 4)
    qm_ref[...] = qm.astype(BF16)
    km_ref[...] = (_dot(ckv_b, wk_ref[0]) + _dot(kr.astype(BF16), e_ref[...])).astype(BF16)
    vm_ref[...] = _mla_values_ext(ckv_b, wv_ref[0])


def _prep(proj, wts, l, seq_len, rope, states=None):
    t = proj.shape[0]
    depth = wts["ln_g"].shape[0]
    tm = min(512, seq_len if rope else t)
    tiles_per_seq = seq_len // tm
    blk = lambda w, idx: pl.BlockSpec((tm, w), lambda i: (i, idx))
    wl = lambda shape: pl.BlockSpec((1,) + shape, lambda i: (l, 0, 0))
    in_specs = [blk(512, COL_Q // 512), blk(512, COL_K // 512), blk(512, COL_V // 512),
                blk(SMALL_W, COL_SMALL // SMALL_W),
                wl((1, MLA_Q_LORA)), wl((MLA_Q_LORA, MLA_HEADS * MLA_HEAD_PAD)),
                wl((1, MLA_KV_LORA)), wl((MLA_KV_LORA, MLA_HEADS * MLA_HEAD_PAD)),
                wl((MLA_KV_LORA, MLA_HEADS * MLA_V_PAD)),
                pl.BlockSpec((LANES, MLA_HEADS * MLA_HEAD_PAD), lambda i: (0, 0))]
    args = [proj, proj, proj, proj, wts["q_norm"], wts["w_uq"], wts["kv_norm"], wts["w_uk"],
            wts["w_uv"], wts["e_place"]]
    if rope:
        in_specs.append(pl.BlockSpec((3, 3, tm, LANES), lambda i: (0, 0, i % tiles_per_seq, 0)))
        args.append(_rope_tables(seq_len))
    out = lambda w: pl.BlockSpec((tm, w), lambda i: (i, 0))
    widths = (512, 512, DA_HEADS * 2 * DA_V_DIM, 1024, 1024, MLA_HEADS * MLA_V_PAD)
    out_specs = [out(w) for w in widths]
    out_shape = [jax.ShapeDtypeStruct((t, w), BF16) for w in widths]
    aliases = {}
    n_state_in = 0
    if not rope:
        nbt = tm // seq_len
        for w in (512, 512, MLA_KV_LORA, MLA_ROPE):
            out_specs.append(pl.BlockSpec((nbt, 1, seq_len, w), lambda i: (i, l, 0, 0)))
            out_shape.append(jax.ShapeDtypeStruct((t // seq_len, depth, seq_len, w), F32))
        if states is not None:
            n_state_in = len(states)
            for k, s in enumerate(states):
                aliases[len(args)] = 6 + k
                in_specs.append(pl.BlockSpec(memory_space=pl.ANY))
                args.append(s)
    return pl.pallas_call(
        functools.partial(_prep_kernel, rope=rope, n_state_in=n_state_in),
        grid=(t // tm,),
        in_specs=in_specs,
        out_specs=out_specs,
        out_shape=out_shape,
        input_output_aliases=aliases,
        compiler_params=_cparams(("arbitrary",)),
        name="attn_prep",
    )(*args)


def _prep_cache_kernel(k_ref, v_ref, ckv_ref, kr_ref, wk_ref, wv_ref, e_ref,
                       kd_ref, vd_ref, km_ref, vm_ref):
    kd_ref[0] = k_ref[0, 0].astype(BF16)
    vd_ref[0] = _da_values_ext(v_ref[0, 0].astype(BF16))
    ckv_b = ckv_ref[0, 0].astype(BF16)
    km_ref[0] = (_dot(ckv_b, wk_ref[0]) + _dot(kr_ref[0, 0].astype(BF16), e_ref[...])).astype(BF16)
    vm_ref[0] = _mla_values_ext(ckv_b, wv_ref[0])


def _prep_cache(cache_k, cache_v, cache_ckv, cache_kr, wts, l):
    b, _, p = cache_k.shape[:3]
    ck = cache_k.reshape(b, -1, p, 512)
    cv = cache_v.reshape(b, -1, p, 512)
    cin = lambda w: pl.BlockSpec((1, 1, p, w), lambda i: (i, l, 0, 0))
    wl = lambda shape: pl.BlockSpec((1,) + shape, lambda i: (l, 0, 0))
    out = lambda w: pl.BlockSpec((1, p, w), lambda i: (i, 0, 0))
    return pl.pallas_call(
        _prep_cache_kernel,
        grid=(b,),
        in_specs=[cin(512), cin(512), cin(MLA_KV_LORA), cin(MLA_ROPE),
                  wl((MLA_KV_LORA, MLA_HEADS * MLA_HEAD_PAD)), wl((MLA_KV_LORA, MLA_HEADS * MLA_V_PAD)),
                  pl.BlockSpec((MLA_ROPE, MLA_HEADS * MLA_HEAD_PAD), lambda i: (0, 0))],
        out_specs=[out(512), out(1024), out(1024), out(MLA_HEADS * MLA_V_PAD)],
        out_shape=[jax.ShapeDtypeStruct((b, p, 512), BF16), jax.ShapeDtypeStruct((b, p, 1024), BF16),
                   jax.ShapeDtypeStruct((b, p, 1024), BF16),
                   jax.ShapeDtypeStruct((b, p, MLA_HEADS * MLA_V_PAD), BF16)],
        compiler_params=_cparams(("arbitrary",)),
        name="attn_prep_cache",
    )(ck, cv, cache_ckv, cache_kr, wts["w_uk"], wts["w_uv"], wts["e_place"][:MLA_ROPE])


def _softmax_pv(q, k_parts, v_parts, dv, den_on_mxu):
    s = [_dot_nt(q, k) for k in k_parts]
    m = s[0].max(axis=-1, keepdims=True)
    for sp in s[1:]:
        m = jnp.maximum(m, sp.max(axis=-1, keepdims=True))
    o = None
    den = None
    for sp, v in zip(s, v_parts):
        if den_on_mxu:
            pv = _dot(jnp.exp((sp - m).astype(BF16)), v)
        else:
            p = jnp.exp(sp - m)
            psum = p.sum(axis=-1, keepdims=True)
            den = psum if den is None else den + psum
            pv = _dot(p.astype(BF16), v[:, :dv])
        o = pv if o is None else o + pv
    if den_on_mxu:
        return o[:, :dv] / o[:, dv:dv + 1]
    return o / den


def _da_kernel(*refs, lam_init, has_ctx):
    if has_ctx:
        q_ref, kn_ref, vn_ref, kc_ref, vc_ref, lam_ref, g_ref, o_ref = refs
    else:
        q_ref, kn_ref, vn_ref, lam_ref, g_ref, o_ref = refs
    lp = lam_ref[0]
    lam = (jnp.exp(jnp.sum(lp[0:1] * lp[1:2], axis=1, keepdims=True))
           - jnp.exp(jnp.sum(lp[2:3] * lp[3:4], axis=1, keepdims=True)) + lam_init)
    for bi in range(q_ref.shape[0]):
        for h in range(DA_HEADS):
            vcols = slice(h * 2 * DA_V_DIM, (h + 1) * 2 * DA_V_DIM)
            outs = []
            for j in range(2):
                cols = slice((j * DA_HEADS + h) * DA_HEAD_DIM, (j * DA_HEADS + h + 1) * DA_HEAD_DIM)
                k_parts = [kn_ref[bi, :, cols]]
                v_parts = [vn_ref[bi, :, vcols]]
                if has_ctx:
                    k_parts.insert(0, kc_ref[bi, :, cols])
                    v_parts.insert(0, vc_ref[bi, :, vcols])
                outs.append(_softmax_pv(q_ref[bi, :, cols], k_parts, v_parts, DA_V_DIM, has_ctx))
            o = outs[0] - lam * outs[1]
            o_ref[bi, :, h * DA_V_DIM:(h + 1) * DA_V_DIM] = (
                _rms(o, g_ref[0], DA_V_DIM) * (1.0 - lam_init))


def _da_attention(qd, kd, vd, ctx, da_lambda, da_subln, l, tq, nbq):
    b, n, _ = qd.shape
    has_ctx = ctx is not None
    lam_init = 0.8 - 0.6 * math.exp(-0.3 * l)
    full = lambda a: pl.BlockSpec((nbq,) + a.shape[1:], lambda i, j: (i, 0, 0))
    in_specs = [pl.BlockSpec((nbq, tq, 512), lambda i, j: (i, j, 0)), full(kd), full(vd)]
    args = [qd, kd, vd]
    if has_ctx:
        in_specs += [full(ctx[0]), full(ctx[1])]
        args += [ctx[0], ctx[1]]
    in_specs += [pl.BlockSpec((1, 4, DA_HEAD_DIM), lambda i, j: (l, 0, 0)),
                 pl.BlockSpec((1, 1, DA_V_DIM), lambda i, j: (l, 0, 0))]
    args += [da_lambda, da_subln.reshape(-1, 1, DA_V_DIM)]
    return pl.pallas_call(
        functools.partial(_da_kernel, lam_init=lam_init, has_ctx=has_ctx),
        grid=(b // nbq, n // tq),
        in_specs=in_specs,
        out_specs=pl.BlockSpec((nbq, tq, 512), lambda i, j: (i, j, 0)),
        out_shape=jax.ShapeDtypeStruct((b, n, 512), F32),
        compiler_params=_cparams(("arbitrary", "arbitrary")),
        name="diff_attention",
    )(*args)


def _mla_kernel(*refs, has_ctx):
    if has_ctx:
        q_ref, kn_ref, vn_ref, kc_ref, vc_ref, o_ref = refs
    else:
        q_ref, kn_ref, vn_ref, o_ref = refs
    for bi in range(q_ref.shape[0]):
        for h in range(MLA_HEADS):
            cols = slice(h * MLA_HEAD_PAD, (h + 1) * MLA_HEAD_PAD)
            vcols = slice(h * MLA_V_PAD, (h + 1) * MLA_V_PAD)
            k_parts = [kn_ref[bi, :, cols]]
            v_parts = [vn_ref[bi, :, vcols]]
            if has_ctx:
                k_parts.insert(0, kc_ref[bi, :, cols])
                v_parts.insert(0, vc_ref[bi, :, vcols])
            o_ref[bi, :, h * MLA_V:(h + 1) * MLA_V] = _softmax_pv(
                q_ref[bi, :, cols], k_parts, v_parts, MLA_V, has_ctx)


def _mla_attention(qm, km, vm, ctx, tq, nbq):
    b, n, _ = qm.shape
    has_ctx = ctx is not None
    full = lambda a: pl.BlockSpec((nbq,) + a.shape[1:], lambda i, j: (i, 0, 0))
    in_specs = [pl.BlockSpec((nbq, tq, 1024), lambda i, j: (i, j, 0)), full(km), full(vm)]
    args = [qm, km, vm]
    if has_ctx:
        in_specs += [full(ctx[0]), full(ctx[1])]
        args += [ctx[0], ctx[1]]
    return pl.pallas_call(
        functools.partial(_mla_kernel, has_ctx=has_ctx),
        grid=(b // nbq, n // tq),
        in_specs=in_specs,
        out_specs=pl.BlockSpec((nbq, tq, 512), lambda i, j: (i, j, 0)),
        out_shape=jax.ShapeDtypeStruct((b, n, 512), F32),
        compiler_params=_cparams(("arbitrary", "arbitrary")),
        name="latent_attention",
    )(*args)


def _outproj_kernel(x_ref, mod_ref, yh_ref, yd_ref, ym_ref, p0_ref, p1_ref, p2_ref,
                    m0_ref, m1_ref, m2_ref, wb_ref, wo_ref, g_ref, b_ref, o_ref, *, alpha):
    merged = None
    for y_ref, p_ref, m_ref, n in ((yh_ref, p0_ref, m0_ref, 0), (yd_ref, p1_ref, m1_ref, 1),
                                   (ym_ref, p2_ref, m2_ref, 2)):
        br = (y_ref[...] * p_ref[...].astype(F32)).astype(BF16)
        term = m_ref[...].astype(F32) * _dot(br, wb_ref[0, n])
        merged = term if merged is None else merged + term
    out = _dot(merged.astype(BF16), wo_ref[0])
    gate = mod_ref[0, 2:3, :]
    y = alpha * x_ref[...] + gate * out
    mu = jnp.mean(y, axis=-1, keepdims=True)
    yc = y - mu
    var = jnp.mean(yc * yc, axis=-1, keepdims=True)
    o_ref[...] = yc * lax.rsqrt(var + LN_EPS) * g_ref[0] + b_ref[0]


def _outproj(x2d, mod, y_hy, y_da, y_mla, gates, w_branch, w_out, ln_g, ln_b, l, seq_len, alpha):
    t = x2d.shape[0]
    per_batch = mod.shape[0] > 1
    tm = min(512, seq_len if per_batch else t)
    tiles_per_batch = seq_len // tm
    mod_idx = (lambda i: (i // tiles_per_batch, 0, 0)) if per_batch else (lambda i: (0, 0, 0))
    row = lambda w, idx=0: pl.BlockSpec((tm, w), lambda i: (i, idx))
    depth = ln_g.shape[0]
    return pl.pallas_call(
        functools.partial(_outproj_kernel, alpha=alpha),
        grid=(t // tm,),
        in_specs=[row(D_MODEL), pl.BlockSpec((1, 3, D_MODEL), mod_idx),
                  row(BR_W), row(BR_W), row(BR_W),
                  row(BR_W, GATE_PATHS // BR_W), row(BR_W, GATE_PATHS // BR_W + 1),
                  row(BR_W, GATE_PATHS // BR_W + 2),
                  row(D_MODEL, GATE_MERGE // D_MODEL), row(D_MODEL, GATE_MERGE // D_MODEL + 1),
                  row(D_MODEL, GATE_MERGE // D_MODEL + 2),
                  pl.BlockSpec((1, N_BRANCH, BR_W, D_MODEL), lambda i: (l, 0, 0, 0)),
                  pl.BlockSpec((1, D_MODEL, D_MODEL), lambda i: (l, 0, 0)),
                  pl.BlockSpec((1, 1, D_MODEL), lambda i: (l, 0, 0)),
                  pl.BlockSpec((1, 1, D_MODEL), lambda i: (l, 0, 0))],
        out_specs=row(D_MODEL),
        out_shape=jax.ShapeDtypeStruct((t, D_MODEL), F32),
        compiler_params=_cparams(("arbitrary",)),
        name="outproj",
    )(x2d, mod, y_hy, y_da, y_mla, gates, gates, gates, gates, gates, gates, w_branch, w_out,
      ln_g.reshape(depth, 1, D_MODEL), ln_b.reshape(depth, 1, D_MODEL))


def _layer(x, mod, l, wts, spectra, ctx=None, states=None):
    b, n, _ = x.shape
    depth = wts["ln_g"].shape[0]
    alpha = (2 * depth) ** 0.25
    x2d = x.reshape(b * n, D_MODEL)
    proj = _inproj(x2d, mod, wts["w_in"], l, n, gates=False)
    gates = _inproj(x2d, mod, wts["w_gate"], l, n, gates=True)
    proj3 = proj.reshape(b, n, D_IN_A)

    kr_s, ki_s, kny_s = spectra
    z2 = _hyena_order(proj3, 0, proj3, 1, wts["conv_w"], wts["conv_b"], kr_s, ki_s, kny_s,
                      wts["skip"], l, 0, True)
    y_hy = _hyena_order(z2, 0, proj3, 2, wts["conv_w"], wts["conv_b"], kr_s, ki_s, kny_s,
                        wts["skip"], l, 1, False)

    prep_out = _prep(proj, wts, l, n, rope=ctx is not None, states=states)
    r3 = lambda a: a.reshape(b, n, a.shape[-1])
    qd, kd, vd, qm, km, vm = map(r3, prep_out[:6])
    new_states = tuple(prep_out[6:]) if ctx is None else None
    if ctx is not None:
        kd_c, vd_c, km_c, vm_c = _prep_cache(*ctx, wts, l)
        da_ctx, mla_ctx = (kd_c, vd_c), (km_c, vm_c)
    else:
        da_ctx = mla_ctx = None
    tq = min(ATTN_TQ, n)
    nbq = 1
    y_da = _da_attention(qd, kd, vd, da_ctx, wts["da_lambda"], wts["da_subln"], l, tq, nbq)
    y_mla = _mla_attention(qm, km, vm, mla_ctx, tq, nbq)

    x_new = _outproj(x2d, mod, y_hy.reshape(b * n, HY_W), y_da.reshape(b * n, BR_W),
                     y_mla.reshape(b * n, BR_W), gates, wts["w_branch"], wts["w_out"],
                     wts["ln_g"], wts["ln_b"], l, n, alpha)
    return x_new.reshape(b, n, D_MODEL), new_states


def _prepare_weights(w_in, hy_conv_w, hy_conv_b, hy_ffn_w1, hy_skip, da_lambda, da_subln,
                     mla_q_norm, mla_w_uq, mla_kv_norm, mla_w_ukv, w_branch, w_out, ln_g, ln_b):
    depth = w_in.shape[0]
    w_in_p = jnp.pad(w_in[..., :ORIG_PATHS].astype(BF16),
                     ((0, 0), (0, 0), (0, SMALL_W - SMALL_USED)))
    w_gate = jnp.concatenate([w_in[..., ORIG_MERGE:ORIG_END], w_in[..., ORIG_PATHS:ORIG_MERGE]],
                             axis=-1).astype(BF16)
    uq = mla_w_uq.reshape(depth, MLA_Q_LORA, MLA_HEADS, MLA_NOPE + MLA_ROPE)
    uq = jnp.pad(uq, ((0, 0), (0, 0), (0, 0), (0, MLA_HEAD_PAD - MLA_NOPE - MLA_ROPE)))
    w_uq = uq.reshape(depth, MLA_Q_LORA, MLA_HEADS * MLA_HEAD_PAD).astype(BF16)
    ukv = mla_w_ukv.reshape(depth, MLA_KV_LORA, MLA_HEADS, MLA_NOPE + MLA_V)
    uk = jnp.pad(ukv[..., :MLA_NOPE], ((0, 0), (0, 0), (0, 0), (0, MLA_HEAD_PAD - MLA_NOPE)))
    w_uk = uk.reshape(depth, MLA_KV_LORA, MLA_HEADS * MLA_HEAD_PAD).astype(BF16)
    uv = jnp.pad(ukv[..., MLA_NOPE:], ((0, 0), (0, 0), (0, 0), (0, MLA_V_PAD - MLA_V)))
    w_uv = uv.reshape(depth, MLA_KV_LORA, MLA_HEADS * MLA_V_PAD).astype(BF16)
    e = np.zeros((LANES, MLA_HEADS * MLA_HEAD_PAD), np.float32)
    for h in range(MLA_HEADS):
        for i in range(MLA_ROPE):
            e[i, h * MLA_HEAD_PAD + MLA_NOPE + i] = 1.0
    w1p = jnp.pad(hy_ffn_w1, ((0, 0), (0, LANES - hy_ffn_w1.shape[1]), (0, 0)))
    return {
        "w_in": w_in_p, "w_gate": w_gate, "conv_w": hy_conv_w, "conv_b": hy_conv_b.reshape(depth, 1, -1),
        "skip": hy_skip.reshape(depth, HY_ORDER, 1, HY_W), "w1p": w1p,
        "da_lambda": da_lambda, "da_subln": da_subln,
        "q_norm": mla_q_norm.reshape(depth, 1, -1), "kv_norm": mla_kv_norm.reshape(depth, 1, -1),
        "w_uq": w_uq, "w_uk": w_uk, "w_uv": w_uv, "e_place": jnp.asarray(e, dtype=BF16),
        "w_branch": w_branch.astype(BF16), "w_out": w_out.astype(BF16), "ln_g": ln_g, "ln_b": ln_b,
    }


def kernel(x_prompt, x_sample, c, cache_diff_k, cache_diff_v, cache_mla_ckv, cache_mla_krope, c_ctx, w_mod, b_mod, w_in, hy_conv_w, hy_conv_b, hy_ffn_w1, hy_ffn_b1, hy_ffn_w2, hy_ffn_b2, hy_ffn_w3, hy_sin_freq, hy_log_decay, hy_skip, da_lambda, da_subln, mla_q_norm, mla_w_uq, mla_kv_norm, mla_w_ukv, w_branch, w_out, ln_g, ln_b):
    depth = w_in.shape[0]
    b_ctx, n_ctx, _ = x_prompt.shape
    b_lat, n_lat, _ = x_sample.shape
    wts = _prepare_weights(w_in, hy_conv_w, hy_conv_b, hy_ffn_w1, hy_skip, da_lambda, da_subln,
                           mla_q_norm, mla_w_uq, mla_kv_norm, mla_w_ukv, w_branch, w_out, ln_g, ln_b)

    rows = -(-(1 + b_lat) // 8) * 8
    cvec = jnp.zeros((rows, D_MODEL), F32).at[0].set(c_ctx).at[1:1 + b_lat].set(c)
    mods = _modulation(cvec, w_mod, b_mod).reshape(depth, rows, 3, D_MODEL)

    spec_args = (wts["w1p"], hy_ffn_b1, hy_ffn_w2, hy_ffn_b2, hy_ffn_w3, hy_sin_freq, hy_log_decay)
    spectra_ctx = _hyena_spectra(n_ctx, *spec_args)
    spectra_lat = spectra_ctx if n_lat == n_ctx else _hyena_spectra(n_lat, *spec_args)

    xp = x_prompt
    states = None
    for l in range(depth):
        xp, states = _layer(xp, mods[l, 0:1], l, wts, spectra_ctx, states=states)
    state_k = states[0].reshape(b_ctx, depth, n_ctx, 2, DA_HEADS, DA_HEAD_DIM)
    state_v = states[1].reshape(b_ctx, depth, n_ctx, DA_HEADS, DA_V_DIM)
    state_ckv, state_kr = states[2], states[3]

    xs = x_sample
    ctx = (cache_diff_k, cache_diff_v, cache_mla_ckv, cache_mla_krope)
    for l in range(depth):
        xs, _ = _layer(xs, mods[l, 1:1 + b_lat], l, wts, spectra_lat, ctx=ctx)

    return (xp, xs, state_k, state_v, state_ckv, state_kr)
```

```python
import functools
import math

import numpy as np
import jax
import jax.numpy as jnp
from jax import lax
from jax.experimental import pallas as pl
from jax.experimental.pallas import tpu as pltpu

F32 = jnp.float32
BF16 = jnp.bfloat16

D_MODEL = 1024
N_BRANCH = 3
BR_W = 512
HY_W = 512
HY_ORDER = 2
HY_BANDS = 16
HY_FFN = 64
DA_HEADS = 4
DA_HEAD_DIM = 64
DA_V_DIM = 2 * DA_HEAD_DIM
MLA_HEADS = 8
MLA_NOPE = 64
MLA_ROPE = 32
MLA_V = 64
MLA_Q_LORA = 384
MLA_KV_LORA = 256
GRID_W = 64
ROPE_BASE = 10000.0
LN_EPS = 1e-5
RMS_EPS = 1e-6

LANES = 128
MLA_HEAD_PAD = LANES
MLA_V_PAD = 2 * LANES
VMEM_LIMIT = 56 * 1024 * 1024
ATTN_TQ = 512

COL_HY = 0
COL_Q = 3 * HY_W
COL_K = COL_Q + 2 * DA_HEADS * DA_HEAD_DIM
COL_V = COL_K + 2 * DA_HEADS * DA_HEAD_DIM
COL_SMALL = COL_V + DA_HEADS * DA_V_DIM
SMALL_W = 1024
SMALL_USED = MLA_Q_LORA + MLA_KV_LORA + MLA_ROPE
D_IN_A = COL_SMALL + SMALL_W
GATE_MERGE = 0
GATE_PATHS = N_BRANCH * D_MODEL
D_IN_G = GATE_PATHS + N_BRANCH * BR_W
ORIG_PATHS = COL_SMALL + SMALL_USED
ORIG_MERGE = ORIG_PATHS + N_BRANCH * BR_W
ORIG_END = ORIG_MERGE + N_BRANCH * D_MODEL


def _cparams(sem):
    return pltpu.CompilerParams(dimension_semantics=sem, vmem_limit_bytes=VMEM_LIMIT)


def _split_bf16(x):
    hi = x.astype(BF16)
    lo = (x - hi.astype(F32)).astype(BF16)
    return hi, lo


def _dot(a, b):
    return jnp.dot(a, b, preferred_element_type=F32)


def _dot_nt(a, b):
    return lax.dot_general(a, b, (((1,), (1,)), ((), ())), preferred_element_type=F32)


def _dot3(a, b):
    ah, al = _split_bf16(a)
    bh, bl = _split_bf16(b)
    return _dot(ah, bh) + _dot(ah, bl) + _dot(al, bh)


def _sigmoid(x):
    return 1.0 / (1.0 + jnp.exp(-x))


@functools.lru_cache(maxsize=None)
def _dft_tables(n_tok):
    k = np.arange(n_tok, dtype=np.int64)
    ang = np.pi * ((np.outer(k, k) % (2 * n_tok)).astype(np.float64)) / n_tok
    return (jnp.asarray(np.cos(ang), dtype=BF16), jnp.asarray(np.sin(ang), dtype=BF16))


@functools.lru_cache(maxsize=None)
def _hyena_feats(n_tok):
    t = np.arange(n_tok, dtype=np.float64)
    t_lin = t / (n_tok - 1)
    bands = np.arange(1, HY_BANDS + 1, dtype=np.float64)
    ang = (2.0 * np.pi / n_tok) * t[:, None] * bands
    feats = np.concatenate([t_lin[:, None], np.cos(ang), -np.sin(ang)], axis=-1)
    out = np.zeros((n_tok, LANES), np.float64)
    out[:, :feats.shape[1]] = feats
    return jnp.asarray(out, dtype=F32)


def _rope_pattern(pos_row, pos_col, dd):
    q = dd // 4
    freqs = 1.0 / (ROPE_BASE ** (np.arange(q, dtype=np.float64) / q))
    n = pos_row.shape[0]
    cos = np.zeros((n, dd)); sa = np.zeros((n, dd)); sb = np.zeros((n, dd))
    for g, pos in enumerate((pos_row, pos_col)):
        ang = pos[:, None].astype(np.float64) * freqs
        base = g * 2 * q
        cos[:, base:base + q] = np.cos(ang); cos[:, base + q:base + 2 * q] = np.cos(ang)
        sa[:, base:base + q] = -np.sin(ang)
        sb[:, base + q:base + 2 * q] = np.sin(ang)
    return cos, sa, sb


@functools.lru_cache(maxsize=None)
def _rope_tables(n_tok):
    t = np.arange(n_tok)
    row, col = t // GRID_W, t % GRID_W
    ones = np.ones((n_tok, LANES)); zeros = np.zeros((n_tok, LANES))
    c, a, b = _rope_pattern(row, col, DA_HEAD_DIM)
    da = [np.tile(c, (1, 2)), np.tile(a, (1, 2)), np.tile(b, (1, 2))]
    c, a, b = _rope_pattern(row, col, MLA_ROPE)
    mq = [ones.copy(), zeros.copy(), zeros.copy()]
    kr = [ones.copy(), zeros.copy(), zeros.copy()]
    for dst, src in zip(mq, (c, a, b)):
        dst[:, MLA_NOPE:MLA_NOPE + MLA_ROPE] = src
    for dst, src in zip(kr, (c, a, b)):
        dst[:, :MLA_ROPE] = src
    tab = np.stack([np.stack(da), np.stack(mq), np.stack(kr)])
    return jnp.asarray(tab, dtype=F32)


def _rope_apply(x, tab_ref, kind, q):
    cos = tab_ref[kind, 0]
    sa = tab_ref[kind, 1]
    sb = tab_ref[kind, 2]
    outs = []
    for c in range(x.shape[1] // LANES):
        xc = x[:, c * LANES:(c + 1) * LANES]
        outs.append(xc * cos + pltpu.roll(xc, LANES - q, 1) * sa + pltpu.roll(xc, q, 1) * sb)
    return outs[0] if len(outs) == 1 else jnp.concatenate(outs, axis=1)


def _rope_mix(x, rot, tab_ref, kind):
    cos = tab_ref[kind, 0]
    sin = tab_ref[kind, 2] - tab_ref[kind, 1]
    outs = []
    for c in range(x.shape[1] // LANES):
        cols = slice(c * LANES, (c + 1) * LANES)
        outs.append(x[:, cols] * cos + rot[:, cols] * sin)
    return jnp.concatenate(outs, axis=1)


def _rot_half_spec(width, period, groups):
    src = np.arange(width)
    sign = np.zeros(width, np.float32)
    for base in range(0, width, period):
        for start, dd in groups:
            q = dd // 4
            for half in range(2):
                lo = base + start + half * 2 * q
                for i in range(q):
                    src[lo + i], sign[lo + i] = lo + i + q, -1.0
                    src[lo + q + i], sign[lo + q + i] = lo + i, 1.0
    return src, sign


@functools.lru_cache(maxsize=None)
def _da_rot_matrix():
    width = 2 * LANES
    src, sign = _rot_half_spec(width, DA_HEAD_DIM, ((0, DA_HEAD_DIM),))
    p = np.zeros((width, width), np.float32)
    p[src, np.arange(width)] = sign
    return jnp.asarray(p, dtype=BF16)


def _mod_kernel(c_ref, w_ref, b_ref, o_ref):
    c = c_ref[...]
    o_ref[0] = _dot3(c * _sigmoid(c), w_ref[0]) + b_ref[0]


def _modulation(cvec, w_mod, b_mod):
    depth = w_mod.shape[0]
    rows = cvec.shape[0]
    return pl.pallas_call(
        _mod_kernel,
        grid=(depth, 3),
        in_specs=[pl.BlockSpec((rows, D_MODEL), lambda l, j: (0, 0)),
                  pl.BlockSpec((1, D_MODEL, D_MODEL), lambda l, j: (l, 0, j)),
                  pl.BlockSpec((1, 1, D_MODEL), lambda l, j: (l, 0, j))],
        out_specs=pl.BlockSpec((1, rows, D_MODEL), lambda l, j: (l, 0, j)),
        out_shape=jax.ShapeDtypeStruct((depth, rows, 3 * D_MODEL), F32),
        compiler_params=_cparams(("arbitrary", "arbitrary")),
        name="modulation",
    )(cvec, w_mod, b_mod.reshape(depth, 1, 3 * D_MODEL))


def _filt_kernel(feats_ref, w1_ref, b1_ref, w2_ref, b2_ref, w3f_ref, w3b_ref, sf_ref,
                 ldf_ref, ldb_ref, c_ref, s_ref, kr_ref, ki_ref, kny_ref, hid_ref):
    n_tok = feats_ref.shape[0]
    inv_n = 1.0 / (2 * n_tok)

    @pl.when((pl.program_id(1) == 0) & (pl.program_id(2) == 0))
    def _():
        sf = sf_ref[0]
        hid = jnp.sin(sf[0:1] * (_dot3(feats_ref[...], w1_ref[0]) + b1_ref[0]))
        hid_ref[...] = jnp.sin(sf[1:2] * (_dot3(hid, w2_ref[0]) + b2_ref[0]))

    hid = hid_ref[...]
    t_lin = feats_ref[:, 0:1]
    hf = _dot3(hid, w3f_ref[0]) * jnp.exp(-t_lin * jnp.exp(ldf_ref[0]))
    hb = _dot3(hid, w3b_ref[0]) * jnp.exp(-t_lin * jnp.exp(ldb_ref[0]))
    row = lax.broadcasted_iota(jnp.int32, hf.shape, 0)
    hb = jnp.where(row == 0, 0.0, hb)
    a = hf + hb
    d = hf - hb
    wk = jnp.where(row == 0, inv_n, 2.0 * inv_n)
    kr_ref[0, 0] = _dot(c_ref[...], a.astype(BF16)) * wk
    ki_ref[0, 0] = -_dot(s_ref[...], d.astype(BF16)) * wk
    sign = jnp.where((row & 1) == 0, 1.0, -1.0)
    kny_ref[0, 0] = jnp.sum(a * sign, axis=0, keepdims=True) * inv_n


def _hyena_spectra(n_tok, w1p, b1, w2, b2, w3, sf, log_decay):
    depth = w3.shape[0]
    ct = 256
    nc = HY_W // ct
    cmat, smat = _dft_tables(n_tok)
    feats = _hyena_feats(n_tok)
    const = lambda shape: pl.BlockSpec(shape, lambda l, n, c: (0,) * len(shape),
                                       pipeline_mode=pl.Buffered(1))
    per_l = lambda shape: pl.BlockSpec((1,) + shape, lambda l, n, c: (l, 0, 0))
    fcol = lambda l, n, c: (l, 0, n * nc + c)
    bcol = lambda l, n, c: (l, 0, HY_ORDER * nc + n * nc + c)
    out_spec = pl.BlockSpec((1, 1, n_tok, ct), lambda l, n, c: (l, n, 0, c))
    return pl.pallas_call(
        _filt_kernel,
        grid=(depth, HY_ORDER, nc),
        in_specs=[const((n_tok, LANES)),
                  per_l((LANES, HY_FFN)), per_l((1, HY_FFN)),
                  per_l((HY_FFN, HY_FFN)), per_l((1, HY_FFN)),
                  pl.BlockSpec((1, HY_FFN, ct), fcol), pl.BlockSpec((1, HY_FFN, ct), bcol),
                  per_l((2, HY_FFN)),
                  pl.BlockSpec((1, 1, ct), fcol), pl.BlockSpec((1, 1, ct), bcol),
                  const((n_tok, n_tok)), const((n_tok, n_tok))],
        out_specs=[out_spec, out_spec,
                   pl.BlockSpec((1, 1, 1, ct), lambda l, n, c: (l, n, 0, c))],
        out_shape=[jax.ShapeDtypeStruct((depth, HY_ORDER, n_tok, HY_W), F32),
                   jax.ShapeDtypeStruct((depth, HY_ORDER, n_tok, HY_W), F32),
                   jax.ShapeDtypeStruct((depth, HY_ORDER, 1, HY_W), F32)],
        scratch_shapes=[pltpu.VMEM((n_tok, HY_FFN), F32)],
        compiler_params=_cparams(("arbitrary",) * 3),
        name="hyena_spectra",
    )(feats, w1p, b1.reshape(depth, 1, HY_FFN), w2, b2.reshape(depth, 1, HY_FFN), w3, w3, sf,
      log_decay.reshape(depth, 1, -1), log_decay.reshape(depth, 1, -1), cmat, smat)


def _inproj_kernel(x_ref, mod_ref, w_ref, o_ref, *, gates):
    shift = mod_ref[0, 0:1, :]
    scale = mod_ref[0, 1:2, :]
    h = (x_ref[...] * (1.0 + scale) + shift).astype(BF16)
    acc = _dot(h, w_ref[0])
    if gates:
        tn = acc.shape[1]
        col = pl.program_id(0) * tn + lax.broadcasted_iota(jnp.int32, acc.shape, 1)
        sg = _sigmoid(acc)
        o_ref[...] = jnp.where(col >= N_BRANCH * D_MODEL, acc * sg, sg).astype(BF16)
    else:
        o_ref[...] = acc


def _inproj(x2d, mod, w, l, seq_len, gates):
    t = x2d.shape[0]
    n_out = w.shape[-1]
    per_batch = mod.shape[0] > 1
    tm = min(1024, seq_len if per_batch else t)
    tn = n_out // 2
    tiles_per_batch = seq_len // tm
    mod_idx = (lambda j, i: (i // tiles_per_batch, 0, 0)) if per_batch else (lambda j, i: (0, 0, 0))
    return pl.pallas_call(
        functools.partial(_inproj_kernel, gates=gates),
        grid=(n_out // tn, t // tm),
        in_specs=[pl.BlockSpec((tm, D_MODEL), lambda j, i: (i, 0)),
                  pl.BlockSpec((1, 3, D_MODEL), mod_idx),
                  pl.BlockSpec((1, D_MODEL, tn), lambda j, i: (l, 0, j))],
        out_specs=pl.BlockSpec((tm, tn), lambda j, i: (i, j)),
        out_shape=jax.ShapeDtypeStruct((t, n_out), BF16 if gates else F32),
        compiler_params=_cparams(("arbitrary", "arbitrary")),
        name="inproj_gates" if gates else "inproj",
    )(x2d, mod, w)


def _short_conv(x, w_ref, b_ref):
    n_tok = x.shape[0]
    row = lax.broadcasted_iota(jnp.int32, x.shape, 0)
    prev = jnp.where(row == 0, 0.0, pltpu.roll(x, 1, 0))
    nxt = jnp.where(row == n_tok - 1, 0.0, pltpu.roll(x, n_tok - 1, 0))
    return prev * w_ref[0:1, :] + x * w_ref[1:2, :] + nxt * w_ref[2:3, :] + b_ref[...]


def _hyena_kernel(*refs, conv_z, tk):
    if conv_z:
        (zin_ref, gin_ref, wz_ref, bz_ref, wg_ref, bg_ref, kr_ref, ki_ref, kny_ref, skip_ref,
         c_ref, s_ref, o_ref, zb_ref, acc_ref) = refs
    else:
        (zin_ref, gin_ref, wg_ref, bg_ref, kr_ref, ki_ref, kny_ref, skip_ref,
         c_ref, s_ref, o_ref, zb_ref, acc_ref) = refs
    nb, n_tok, ct = zin_ref.shape
    for i in range(nb):
        z = zin_ref[i]
        if conv_z:
            z = _short_conv(z, wz_ref, bz_ref)
        zb_ref[i] = z.astype(BF16)
        for j in range(n_tok // tk):
            rows = slice(j * tk, (j + 1) * tk)
            zr = _dot(c_ref[rows, :], zb_ref[i])
            zs = _dot(s_ref[rows, :], zb_ref[i])
            kr = kr_ref[0, 0, rows, :]
            ki = ki_ref[0, 0, rows, :]
            yr = (zr * kr + zs * ki).astype(BF16)
            ym = (zs * kr - zr * ki).astype(BF16)
            part = _dot(c_ref[:, rows], yr) + _dot(s_ref[:, rows], ym)
            if j == 0:
                acc_ref[i] = part
            else:
                acc_ref[i] += part
        row = lax.broadcasted_iota(jnp.int32, z.shape, 0)
        sign = jnp.where((row & 1) == 0, 1.0, -1.0)
        zny = jnp.sum(z * sign, axis=0, keepdims=True)
        y = acc_ref[i] + sign * (zny * kny_ref[0, 0]) + z * skip_ref[0, 0]
        o_ref[i] = _short_conv(gin_ref[i], wg_ref, bg_ref) * y


def _hyena_order(zin, zblk, gin, gblk, conv_w, conv_b, kr, ki, kny, skip, l, order, conv_z):
    b, n_tok = zin.shape[0], zin.shape[1]
    cmat, smat = _dft_tables(n_tok)
    if n_tok >= 2048:
        ct, nb = 256, 1
    else:
        ct, nb = 512, min(8, b)
    nc = HY_W // ct
    tk = min(256, n_tok)
    const = lambda shape: pl.BlockSpec(shape, lambda c, i: (0,) * len(shape),
                                       pipeline_mode=pl.Buffered(1))
    zspec = pl.BlockSpec((nb, n_tok, ct), lambda c, i: (i, 0, zblk * nc + c))
    gspec = pl.BlockSpec((nb, n_tok, ct), lambda c, i: (i, 0, gblk * nc + c))
    wspec = lambda blk: pl.BlockSpec((1, 3, ct), lambda c, i: (l, 0, blk * nc + c))
    bspec = lambda blk: pl.BlockSpec((1, 1, ct), lambda c, i: (l, 0, blk * nc + c))
    kspec = pl.BlockSpec((1, 1, n_tok, ct), lambda c, i: (l, order, 0, c))
    in_specs = [zspec, gspec]
    args = [zin, gin]
    if conv_z:
        in_specs += [wspec(0), bspec(0)]
        args += [conv_w, conv_b]
    in_specs += [wspec(order + 1), bspec(order + 1), kspec, kspec,
                 pl.BlockSpec((1, 1, 1, ct), lambda c, i: (l, order, 0, c)),
                 pl.BlockSpec((1, 1, 1, ct), lambda c, i: (l, order, 0, c)),
                 const((n_tok, n_tok)), const((n_tok, n_tok))]
    args += [conv_w, conv_b, kr, ki, kny, skip, cmat, smat]

    def body(*refs):
        refs = list(refs)
        lo = 2
        n_small = 4 if conv_z else 2
        for k in range(lo, lo + n_small):
            refs[k] = refs[k].at[0]
        _hyena_kernel(*refs, conv_z=conv_z, tk=tk)

    return pl.pallas_call(
        body,
        grid=(nc, b // nb),
        in_specs=in_specs,
        out_specs=pl.BlockSpec((nb, n_tok, ct), lambda c, i: (i, 0, c)),
        out_shape=jax.ShapeDtypeStruct((b, n_tok, HY_W), F32),
        scratch_shapes=[pltpu.VMEM((nb, n_tok, ct), BF16), pltpu.VMEM((nb, n_tok, ct), F32)],
        compiler_params=_cparams(("arbitrary", "arbitrary")),
        name=f"hyena_order{order}",
    )(*args)


def _rms(x, g, n):
    ms = jnp.sum(x * x, axis=-1, keepdims=True) * (1.0 / n)
    return x * lax.rsqrt(ms + RMS_EPS) * g


def _da_values_ext(v):
    lane = lax.broadcasted_iota(jnp.int32, (v.shape[0], LANES), 1)
    ones_col = jnp.where(lane == 0, 1.0, 0.0).astype(BF16)
    pieces = []
    for h in range(DA_HEADS):
        pieces += [v[:, h * DA_V_DIM:(h + 1) * DA_V_DIM], ones_col]
    return jnp.concatenate(pieces, axis=1)


def _mla_values_ext(ckv_b, wv):
    v = _dot(ckv_b, wv)
    lane = lax.broadcasted_iota(jnp.int32, v.shape, 1)
    return jnp.where((lane & (MLA_V_PAD - 1)) == MLA_V, 1.0, v).astype(BF16)


def _prep_kernel(*refs, rope, n_state_in):
    it = iter(refs)
    q_ref, k_ref, v_ref, sm_ref, qn_ref, wuq_ref, kvn_ref, wk_ref, wv_ref, e_ref = (
        next(it) for _ in range(10))
    tab_ref, wuqr_ref, prot_ref = (next(it) for _ in range(3)) if rope else (None,) * 3
    for _ in range(n_state_in):
        next(it)
    qd_ref, kd_ref, vd_ref, qm_ref, km_ref, vm_ref = (next(it) for _ in range(6))
    q = q_ref[...] * (DA_HEAD_DIM ** -0.5)
    k = k_ref[...]
    v = v_ref[...]
    dq = sm_ref[:, 0:MLA_Q_LORA]
    dkv = sm_ref[:, MLA_Q_LORA:MLA_Q_LORA + MLA_KV_LORA]
    kr = sm_ref[:, 5 * LANES:6 * LANES]
    c_kv = _rms(dkv, kvn_ref[0], MLA_KV_LORA)
    if not rope:
        ks_ref, vs_ref, ckvs_ref, krs_ref = (next(it) for _ in range(4))
        nbt, _, n_tok, _ = ks_ref.shape
        ks_ref[:, 0] = k.reshape(nbt, n_tok, k.shape[1])
        vs_ref[:, 0] = v.reshape(nbt, n_tok, v.shape[1])
        ckvs_ref[:, 0] = c_kv.reshape(nbt, n_tok, MLA_KV_LORA)
        krs_ref[:, 0] = kr[:, :MLA_ROPE].reshape(nbt, n_tok, MLA_ROPE)
    if rope:
        def rot(x):
            hi, lo = _split_bf16(x)
            w = prot_ref.shape[0]
            return jnp.concatenate(
                [_dot(hi[:, c:c + w], prot_ref[...]) + _dot(lo[:, c:c + w], prot_ref[...])
                 for c in range(0, x.shape[1], w)], axis=1)
        q = _rope_mix(q, rot(q), tab_ref, 0)
        k = _rope_mix(k, rot(k), tab_ref, 0)
    qd_ref[...] = q.astype(BF16)
    kd_ref[...] = k.astype(BF16)

    mla_scale = (MLA_NOPE + MLA_ROPE) ** -0.5
    c_q = _rms(dq, qn_ref[0], MLA_Q_LORA).astype(BF16)
    qm = _dot(c_q, wuq_ref[0]) * mla_scale
    ckv_b = c_kv.astype(BF16)
    if rope:
        qm = _rope_mix(qm, _dot(c_q, wuqr_ref[0]) * mla_scale, tab_ref, 1)
        kr = _rope_apply(kr, tab_ref, 2, MLA_ROPE // 4)
        vd_ref[...] = _da_values_ext(v.astype(BF16))
        vm_ref[...] = _mla_values_ext(ckv_b, wv_ref[0])
    else:
        vd_ref[...] = v.astype(BF16)
        vm_ref[...] = _dot(ckv_b, wv_ref[0]).astype(BF16)
    qm_ref[...] = qm.astype(BF16)
    km_ref[...] = (_dot(ckv_b, wk_ref[0]) + _dot(kr.astype(BF16), e_ref[...])).astype(BF16)


def _prep(proj, wts, l, seq_len, rope, states=None):
    t = proj.shape[0]
    depth = wts["ln_g"].shape[0]
    tm = min(512, seq_len if rope else t)
    tiles_per_seq = seq_len // tm
    blk = lambda w, idx: pl.BlockSpec((tm, w), lambda i: (i, idx))
    wl = lambda shape: pl.BlockSpec((1,) + shape, lambda i: (l, 0, 0))
    in_specs = [blk(512, COL_Q // 512), blk(512, COL_K // 512), blk(512, COL_V // 512),
                blk(SMALL_W, COL_SMALL // SMALL_W),
                wl((1, MLA_Q_LORA)), wl((MLA_Q_LORA, MLA_HEADS * MLA_HEAD_PAD)),
                wl((1, MLA_KV_LORA)), wl((MLA_KV_LORA, MLA_HEADS * MLA_HEAD_PAD)),
                wl((MLA_KV_LORA, MLA_HEADS * (MLA_V_PAD if rope else MLA_V))),
                pl.BlockSpec((LANES, MLA_HEADS * MLA_HEAD_PAD), lambda i: (0, 0))]
    args = [proj, proj, proj, proj, wts["q_norm"], wts["w_uq"], wts["kv_norm"], wts["w_uk"],
            wts["w_uv_pad" if rope else "w_uv"], wts["e_place"]]
    if rope:
        prot = _da_rot_matrix()
        in_specs += [pl.BlockSpec((3, 3, tm, LANES), lambda i: (0, 0, i % tiles_per_seq, 0)),
                     wl((MLA_Q_LORA, MLA_HEADS * MLA_HEAD_PAD)),
                     pl.BlockSpec(prot.shape, lambda i: (0, 0))]
        args += [_rope_tables(seq_len), wts["w_uq_rot"], prot]
    out = lambda w: pl.BlockSpec((tm, w), lambda i: (i, 0))
    widths = ((512, 512, DA_HEADS * 2 * DA_V_DIM, 1024, 1024, MLA_HEADS * MLA_V_PAD) if rope
              else (512, 512, DA_HEADS * DA_V_DIM, 1024, 1024, MLA_HEADS * MLA_V))
    out_specs = [out(w) for w in widths]
    out_shape = [jax.ShapeDtypeStruct((t, w), BF16) for w in widths]
    aliases = {}
    n_state_in = 0
    if not rope:
        nbt = tm // seq_len
        for w in (512, 512, MLA_KV_LORA, MLA_ROPE):
            out_specs.append(pl.BlockSpec((nbt, 1, seq_len, w), lambda i: (i, l, 0, 0)))
            out_shape.append(jax.ShapeDtypeStruct((t // seq_len, depth, seq_len, w), F32))
        if states is not None:
            n_state_in = len(states)
            for k, s in enumerate(states):
                aliases[len(args)] = 6 + k
                in_specs.append(pl.BlockSpec(memory_space=pl.ANY))
                args.append(s)
    return pl.pallas_call(
        functools.partial(_prep_kernel, rope=rope, n_state_in=n_state_in),
        grid=(t // tm,),
        in_specs=in_specs,
        out_specs=out_specs,
        out_shape=out_shape,
        input_output_aliases=aliases,
        compiler_params=_cparams(("arbitrary",)),
        name="attn_prep",
    )(*args)


def _prep_cache_kernel(k_ref, v_ref, ckv_ref, kr_ref, wk_ref, wv_ref, e_ref,
                       kd_ref, vd_ref, km_ref, vm_ref):
    kd_ref[0] = k_ref[0, 0].astype(BF16)
    vd_ref[0] = _da_values_ext(v_ref[0, 0].astype(BF16))
    ckv_b = ckv_ref[0, 0].astype(BF16)
    km_ref[0] = (_dot(ckv_b, wk_ref[0]) + _dot(kr_ref[0, 0].astype(BF16), e_ref[...])).astype(BF16)
    vm_ref[0] = _mla_values_ext(ckv_b, wv_ref[0])


def _prep_cache(cache_k, cache_v, cache_ckv, cache_kr, wts, l):
    b, _, p = cache_k.shape[:3]
    ck = cache_k.reshape(b, -1, p, 512)
    cv = cache_v.reshape(b, -1, p, 512)
    cin = lambda w: pl.BlockSpec((1, 1, p, w), lambda i: (i, l, 0, 0))
    wl = lambda shape: pl.BlockSpec((1,) + shape, lambda i: (l, 0, 0))
    out = lambda w: pl.BlockSpec((1, p, w), lambda i: (i, 0, 0))
    return pl.pallas_call(
        _prep_cache_kernel,
        grid=(b,),
        in_specs=[cin(512), cin(512), cin(MLA_KV_LORA), cin(MLA_ROPE),
                  wl((MLA_KV_LORA, MLA_HEADS * MLA_HEAD_PAD)), wl((MLA_KV_LORA, MLA_HEADS * MLA_V_PAD)),
                  pl.BlockSpec((MLA_ROPE, MLA_HEADS * MLA_HEAD_PAD), lambda i: (0, 0))],
        out_specs=[out(512), out(1024), out(1024), out(MLA_HEADS * MLA_V_PAD)],
        out_shape=[jax.ShapeDtypeStruct((b, p, 512), BF16), jax.ShapeDtypeStruct((b, p, 1024), BF16),
                   jax.ShapeDtypeStruct((b, p, 1024), BF16),
                   jax.ShapeDtypeStruct((b, p, MLA_HEADS * MLA_V_PAD), BF16)],
        compiler_params=_cparams(("arbitrary",)),
        name="attn_prep_cache",
    )(ck, cv, cache_ckv, cache_kr, wts["w_uk"], wts["w_uv_pad"], wts["e_place"][:MLA_ROPE])


def _softmax_pv(q, k_parts, v_parts, dv, den_on_mxu):
    s = [_dot_nt(q, k) for k in k_parts]
    m = s[0].max(axis=-1, keepdims=True)
    for sp in s[1:]:
        m = jnp.maximum(m, sp.max(axis=-1, keepdims=True))
    o = None
    den = None
    for sp, v in zip(s, v_parts):
        if den_on_mxu:
            pv = _dot(jnp.exp((sp - m).astype(BF16)), v)
        else:
            p = jnp.exp(sp - m)
            psum = p.sum(axis=-1, keepdims=True)
            den = psum if den is None else den + psum
            pv = _dot(p.astype(BF16), v)
        o = pv if o is None else o + pv
    if den_on_mxu:
        return o[:, :dv] / o[:, dv:dv + 1]
    return o / den


def _da_kernel(*refs, lam_init, has_ctx):
    if has_ctx:
        q_ref, kn_ref, vn_ref, kc_ref, vc_ref, lam_ref, g_ref, o_ref = refs
    else:
        q_ref, kn_ref, vn_ref, lam_ref, g_ref, o_ref = refs
    lp = lam_ref[0]
    lam = (jnp.exp(jnp.sum(lp[0:1] * lp[1:2], axis=1, keepdims=True))
           - jnp.exp(jnp.sum(lp[2:3] * lp[3:4], axis=1, keepdims=True)) + lam_init)
    for bi in range(q_ref.shape[0]):
        for h in range(DA_HEADS):
            vw = 2 * DA_V_DIM if has_ctx else DA_V_DIM
            vcols = slice(h * vw, (h + 1) * vw)
            outs = []
            for j in range(2):
                cols = slice((j * DA_HEADS + h) * DA_HEAD_DIM, (j * DA_HEADS + h + 1) * DA_HEAD_DIM)
                k_parts = [kn_ref[bi, :, cols]]
                v_parts = [vn_ref[bi, :, vcols]]
                if has_ctx:
                    k_parts.insert(0, kc_ref[bi, :, cols])
                    v_parts.insert(0, vc_ref[bi, :, vcols])
                outs.append(_softmax_pv(q_ref[bi, :, cols], k_parts, v_parts, DA_V_DIM, has_ctx))
            o = outs[0] - lam * outs[1]
            o_ref[bi, :, h * DA_V_DIM:(h + 1) * DA_V_DIM] = (
                _rms(o, g_ref[0], DA_V_DIM) * (1.0 - lam_init))


def _da_attention(qd, kd, vd, ctx, da_lambda, da_subln, l, tq, nbq):
    b, n, _ = qd.shape
    has_ctx = ctx is not None
    lam_init = 0.8 - 0.6 * math.exp(-0.3 * l)
    full = lambda a: pl.BlockSpec((nbq,) + a.shape[1:], lambda i, j: (i, 0, 0))
    in_specs = [pl.BlockSpec((nbq, tq, 512), lambda i, j: (i, j, 0)), full(kd), full(vd)]
    args = [qd, kd, vd]
    if has_ctx:
        in_specs += [full(ctx[0]), full(ctx[1])]
        args += [ctx[0], ctx[1]]
    in_specs += [pl.BlockSpec((1, 4, DA_HEAD_DIM), lambda i, j: (l, 0, 0)),
                 pl.BlockSpec((1, 1, DA_V_DIM), lambda i, j: (l, 0, 0))]
    args += [da_lambda, da_subln.reshape(-1, 1, DA_V_DIM)]
    return pl.pallas_call(
        functools.partial(_da_kernel, lam_init=lam_init, has_ctx=has_ctx),
        grid=(b // nbq, n // tq),
        in_specs=in_specs,
        out_specs=pl.BlockSpec((nbq, tq, 512), lambda i, j: (i, j, 0)),
        out_shape=jax.ShapeDtypeStruct((b, n, 512), F32),
        compiler_params=_cparams(("arbitrary", "arbitrary")),
        name="diff_attention",
    )(*args)


def _mla_kernel(*refs, has_ctx):
    if has_ctx:
        q_ref, kn_ref, vn_ref, kc_ref, vc_ref, o_ref = refs
    else:
        q_ref, kn_ref, vn_ref, o_ref = refs
    for bi in range(q_ref.shape[0]):
        for h in range(MLA_HEADS):
            cols = slice(h * MLA_HEAD_PAD, (h + 1) * MLA_HEAD_PAD)
            vw = MLA_V_PAD if has_ctx else MLA_V
            vcols = slice(h * vw, (h + 1) * vw)
            k_parts = [kn_ref[bi, :, cols]]
            v_parts = [vn_ref[bi, :, vcols]]
            if has_ctx:
                k_parts.insert(0, kc_ref[bi, :, cols])
                v_parts.insert(0, vc_ref[bi, :, vcols])
            o_ref[bi, :, h * MLA_V:(h + 1) * MLA_V] = _softmax_pv(
                q_ref[bi, :, cols], k_parts, v_parts, MLA_V, has_ctx)


def _mla_attention(qm, km, vm, ctx, tq, nbq):
    b, n, _ = qm.shape
    has_ctx = ctx is not None
    full = lambda a: pl.BlockSpec((nbq,) + a.shape[1:], lambda i, j: (i, 0, 0))
    in_specs = [pl.BlockSpec((nbq, tq, 1024), lambda i, j: (i, j, 0)), full(km), full(vm)]
    args = [qm, km, vm]
    if has_ctx:
        in_specs += [full(ctx[0]), full(ctx[1])]
        args += [ctx[0], ctx[1]]
    return pl.pallas_call(
        functools.partial(_mla_kernel, has_ctx=has_ctx),
        grid=(b // nbq, n // tq),
        in_specs=in_specs,
        out_specs=pl.BlockSpec((nbq, tq, 512), lambda i, j: (i, j, 0)),
        out_shape=jax.ShapeDtypeStruct((b, n, 512), F32),
        compiler_params=_cparams(("arbitrary", "arbitrary")),
        name="latent_attention",
    )(*args)


def _outproj_kernel(x_ref, mod_ref, yh_ref, yd_ref, ym_ref, p0_ref, p1_ref, p2_ref,
                    m0_ref, m1_ref, m2_ref, wb_ref, wo_ref, g_ref, b_ref, o_ref, *, alpha):
    merged = None
    for y_ref, p_ref, m_ref, n in ((yh_ref, p0_ref, m0_ref, 0), (yd_ref, p1_ref, m1_ref, 1),
                                   (ym_ref, p2_ref, m2_ref, 2)):
        br = (y_ref[...] * p_ref[...].astype(F32)).astype(BF16)
        term = m_ref[...].astype(F32) * _dot(br, wb_ref[0, n])
        merged = term if merged is None else merged + term
    out = _dot(merged.astype(BF16), wo_ref[0])
    gate = mod_ref[0, 2:3, :]
    y = alpha * x_ref[...] + gate * out
    mu = jnp.mean(y, axis=-1, keepdims=True)
    yc = y - mu
    var = jnp.mean(yc * yc, axis=-1, keepdims=True)
    o_ref[...] = yc * lax.rsqrt(var + LN_EPS) * g_ref[0] + b_ref[0]


def _outproj(x2d, mod, y_hy, y_da, y_mla, gates, w_branch, w_out, ln_g, ln_b, l, seq_len, alpha):
    t = x2d.shape[0]
    per_batch = mod.shape[0] > 1
    tm = min(512, seq_len if per_batch else t)
    tiles_per_batch = seq_len // tm
    mod_idx = (lambda i: (i // tiles_per_batch, 0, 0)) if per_batch else (lambda i: (0, 0, 0))
    row = lambda w, idx=0: pl.BlockSpec((tm, w), lambda i: (i, idx))
    depth = ln_g.shape[0]
    return pl.pallas_call(
        functools.partial(_outproj_kernel, alpha=alpha),
        grid=(t // tm,),
        in_specs=[row(D_MODEL), pl.BlockSpec((1, 3, D_MODEL), mod_idx),
                  row(BR_W), row(BR_W), row(BR_W),
                  row(BR_W, GATE_PATHS // BR_W), row(BR_W, GATE_PATHS // BR_W + 1),
                  row(BR_W, GATE_PATHS // BR_W + 2),
                  row(D_MODEL, GATE_MERGE // D_MODEL), row(D_MODEL, GATE_MERGE // D_MODEL + 1),
                  row(D_MODEL, GATE_MERGE // D_MODEL + 2),
                  pl.BlockSpec((1, N_BRANCH, BR_W, D_MODEL), lambda i: (l, 0, 0, 0)),
                  pl.BlockSpec((1, D_MODEL, D_MODEL), lambda i: (l, 0, 0)),
                  pl.BlockSpec((1, 1, D_MODEL), lambda i: (l, 0, 0)),
                  pl.BlockSpec((1, 1, D_MODEL), lambda i: (l, 0, 0))],
        out_specs=row(D_MODEL),
        out_shape=jax.ShapeDtypeStruct((t, D_MODEL), F32),
        compiler_params=_cparams(("arbitrary",)),
        name="outproj",
    )(x2d, mod, y_hy, y_da, y_mla, gates, gates, gates, gates, gates, gates, w_branch, w_out,
      ln_g.reshape(depth, 1, D_MODEL), ln_b.reshape(depth, 1, D_MODEL))


def _layer(x, mod, l, wts, spectra, ctx=None, states=None):
    b, n, _ = x.shape
    depth = wts["ln_g"].shape[0]
    alpha = (2 * depth) ** 0.25
    x2d = x.reshape(b * n, D_MODEL)
    proj = _inproj(x2d, mod, wts["w_in"], l, n, gates=False)
    gates = _inproj(x2d, mod, wts["w_gate"], l, n, gates=True)
    proj3 = proj.reshape(b, n, D_IN_A)

    kr_s, ki_s, kny_s = spectra
    z2 = _hyena_order(proj3, 0, proj3, 1, wts["conv_w"], wts["conv_b"], kr_s, ki_s, kny_s,
                      wts["skip"], l, 0, True)
    y_hy = _hyena_order(z2, 0, proj3, 2, wts["conv_w"], wts["conv_b"], kr_s, ki_s, kny_s,
                        wts["skip"], l, 1, False)

    prep_out = _prep(proj, wts, l, n, rope=ctx is not None, states=states)
    r3 = lambda a: a.reshape(b, n, a.shape[-1])
    qd, kd, vd, qm, km, vm = map(r3, prep_out[:6])
    new_states = tuple(prep_out[6:]) if ctx is None else None
    if ctx is not None:
        kd_c, vd_c, km_c, vm_c = _prep_cache(*ctx, wts, l)
        da_ctx, mla_ctx = (kd_c, vd_c), (km_c, vm_c)
    else:
        da_ctx = mla_ctx = None
    tq = min(ATTN_TQ, n)
    nbq = 1
    y_da = _da_attention(qd, kd, vd, da_ctx, wts["da_lambda"], wts["da_subln"], l, tq, nbq)
    y_mla = _mla_attention(qm, km, vm, mla_ctx, tq, nbq)

    x_new = _outproj(x2d, mod, y_hy.reshape(b * n, HY_W), y_da.reshape(b * n, BR_W),
                     y_mla.reshape(b * n, BR_W), gates, wts["w_branch"], wts["w_out"],
                     wts["ln_g"], wts["ln_b"], l, n, alpha)
    return x_new.reshape(b, n, D_MODEL), new_states


def _prepare_weights(w_in, hy_conv_w, hy_conv_b, hy_ffn_w1, hy_skip, da_lambda, da_subln,
                     mla_q_norm, mla_w_uq, mla_kv_norm, mla_w_ukv, w_branch, w_out, ln_g, ln_b):
    depth = w_in.shape[0]
    w_in_p = jnp.pad(w_in[..., :ORIG_PATHS].astype(BF16),
                     ((0, 0), (0, 0), (0, SMALL_W - SMALL_USED)))
    w_gate = jnp.concatenate([w_in[..., ORIG_MERGE:ORIG_END], w_in[..., ORIG_PATHS:ORIG_MERGE]],
                             axis=-1).astype(BF16)
    uq = mla_w_uq.reshape(depth, MLA_Q_LORA, MLA_HEADS, MLA_NOPE + MLA_ROPE)
    uq = jnp.pad(uq, ((0, 0), (0, 0), (0, 0), (0, MLA_HEAD_PAD - MLA_NOPE - MLA_ROPE)))
    w_uq = uq.reshape(depth, MLA_Q_LORA, MLA_HEADS * MLA_HEAD_PAD).astype(BF16)
    ukv = mla_w_ukv.reshape(depth, MLA_KV_LORA, MLA_HEADS, MLA_NOPE + MLA_V)
    uk = jnp.pad(ukv[..., :MLA_NOPE], ((0, 0), (0, 0), (0, 0), (0, MLA_HEAD_PAD - MLA_NOPE)))
    w_uk = uk.reshape(depth, MLA_KV_LORA, MLA_HEADS * MLA_HEAD_PAD).astype(BF16)
    w_uv = ukv[..., MLA_NOPE:].reshape(depth, MLA_KV_LORA, MLA_HEADS * MLA_V).astype(BF16)
    uv = jnp.pad(ukv[..., MLA_NOPE:], ((0, 0), (0, 0), (0, 0), (0, MLA_V_PAD - MLA_V)))
    w_uv_pad = uv.reshape(depth, MLA_KV_LORA, MLA_HEADS * MLA_V_PAD).astype(BF16)
    src, sign = _rot_half_spec(MLA_HEADS * MLA_HEAD_PAD, MLA_HEAD_PAD, ((MLA_NOPE, MLA_ROPE),))
    w_uq_rot = w_uq[..., src] * jnp.asarray(sign, dtype=BF16)
    e = np.zeros((LANES, MLA_HEADS * MLA_HEAD_PAD), np.float32)
    for h in range(MLA_HEADS):
        for i in range(MLA_ROPE):
            e[i, h * MLA_HEAD_PAD + MLA_NOPE + i] = 1.0
    w1p = jnp.pad(hy_ffn_w1, ((0, 0), (0, LANES - hy_ffn_w1.shape[1]), (0, 0)))
    return {
        "w_in": w_in_p, "w_gate": w_gate, "conv_w": hy_conv_w, "conv_b": hy_conv_b.reshape(depth, 1, -1),
        "skip": hy_skip.reshape(depth, HY_ORDER, 1, HY_W), "w1p": w1p,
        "da_lambda": da_lambda, "da_subln": da_subln,
        "q_norm": mla_q_norm.reshape(depth, 1, -1), "kv_norm": mla_kv_norm.reshape(depth, 1, -1),
        "w_uq": w_uq, "w_uq_rot": w_uq_rot, "w_uk": w_uk, "w_uv": w_uv, "w_uv_pad": w_uv_pad, "e_place": jnp.asarray(e, dtype=BF16),
        "w_branch": w_branch.astype(BF16), "w_out": w_out.astype(BF16), "ln_g": ln_g, "ln_b": ln_b,
    }


def kernel(x_prompt, x_sample, c, cache_diff_k, cache_diff_v, cache_mla_ckv, cache_mla_krope, c_ctx, w_mod, b_mod, w_in, hy_conv_w, hy_conv_b, hy_ffn_w1, hy_ffn_b1, hy_ffn_w2, hy_ffn_b2, hy_ffn_w3, hy_sin_freq, hy_log_decay, hy_skip, da_lambda, da_subln, mla_q_norm, mla_w_uq, mla_kv_norm, mla_w_ukv, w_branch, w_out, ln_g, ln_b):
    depth = w_in.shape[0]
    b_ctx, n_ctx, _ = x_prompt.shape
    b_lat, n_lat, _ = x_sample.shape
    wts = _prepare_weights(w_in, hy_conv_w, hy_conv_b, hy_ffn_w1, hy_skip, da_lambda, da_subln,
                           mla_q_norm, mla_w_uq, mla_kv_norm, mla_w_ukv, w_branch, w_out, ln_g, ln_b)

    rows = -(-(1 + b_lat) // 8) * 8
    cvec = jnp.zeros((rows, D_MODEL), F32).at[0].set(c_ctx).at[1:1 + b_lat].set(c)
    mods = _modulation(cvec, w_mod, b_mod).reshape(depth, rows, 3, D_MODEL)

    spec_args = (wts["w1p"], hy_ffn_b1, hy_ffn_w2, hy_ffn_b2, hy_ffn_w3, hy_sin_freq, hy_log_decay)
    spectra_ctx = _hyena_spectra(n_ctx, *spec_args)
    spectra_lat = spectra_ctx if n_lat == n_ctx else _hyena_spectra(n_lat, *spec_args)

    xp = x_prompt
    states = None
    for l in range(depth):
        xp, states = _layer(xp, mods[l, 0:1], l, wts, spectra_ctx, states=states)
    state_k = states[0].reshape(b_ctx, depth, n_ctx, 2, DA_HEADS, DA_HEAD_DIM)
    state_v = states[1].reshape(b_ctx, depth, n_ctx, DA_HEADS, DA_V_DIM)
    state_ckv, state_kr = states[2], states[3]

    xs = x_sample
    ctx = (cache_diff_k, cache_diff_v, cache_mla_ckv, cache_mla_krope)
    for l in range(depth):
        xs, _ = _layer(xs, mods[l, 1:1 + b_lat], l, wts, spectra_lat, ctx=ctx)

    return (xp, xs, state_k, state_v, state_ckv, state_kr)
```

```python
import functools
import math

import numpy as np
import jax
import jax.numpy as jnp
from jax import lax
from jax.experimental import pallas as pl
from jax.experimental.pallas import tpu as pltpu

F32 = jnp.float32
BF16 = jnp.bfloat16

D_MODEL = 1024
N_BRANCH = 3
BR_W = 512
HY_W = 512
HY_ORDER = 2
HY_BANDS = 16
HY_FFN = 64
DA_HEADS = 4
DA_HEAD_DIM = 64
DA_V_DIM = 2 * DA_HEAD_DIM
MLA_HEADS = 8
MLA_NOPE = 64
MLA_ROPE = 32
MLA_V = 64
MLA_Q_LORA = 384
MLA_KV_LORA = 256
GRID_W = 64
ROPE_BASE = 10000.0
LN_EPS = 1e-5
RMS_EPS = 1e-6

LANES = 128
MLA_HEAD_PAD = LANES
MLA_V_PAD = 2 * LANES
VMEM_LIMIT = 56 * 1024 * 1024
ATTN_TQ = 512

D_IN_HY = 3 * HY_W
SMALL_W = 1024
SMALL_USED = MLA_Q_LORA + MLA_KV_LORA + MLA_ROPE
COL_HY = 0
COL_Q = D_IN_HY
COL_K = COL_Q + 2 * DA_HEADS * DA_HEAD_DIM
COL_V = COL_K + 2 * DA_HEADS * DA_HEAD_DIM
COL_SMALL = COL_V + DA_HEADS * DA_V_DIM
D_IN_MAIN = COL_SMALL + SMALL_W
GATE_MERGE = 0
GATE_PATHS = N_BRANCH * D_MODEL
ORIG_PATHS = COL_SMALL + SMALL_USED
ORIG_MERGE = ORIG_PATHS + N_BRANCH * BR_W
ORIG_END = ORIG_MERGE + N_BRANCH * D_MODEL


def _cparams(sem):
    return pltpu.CompilerParams(dimension_semantics=sem, vmem_limit_bytes=VMEM_LIMIT)


def _split_bf16(x):
    hi = x.astype(BF16)
    lo = (x - hi.astype(F32)).astype(BF16)
    return hi, lo


def _dot(a, b):
    return jnp.dot(a, b, preferred_element_type=F32)


def _dot_nt(a, b):
    return lax.dot_general(a, b, (((1,), (1,)), ((), ())), preferred_element_type=F32)


def _dot3(a, b):
    ah, al = _split_bf16(a)
    bh, bl = _split_bf16(b)
    return _dot(ah, bh) + _dot(ah, bl) + _dot(al, bh)


def _sigmoid(x):
    return 0.5 * jnp.tanh(0.5 * x) + 0.5


@functools.lru_cache(maxsize=None)
def _dft_tables(n_tok):
    k = np.arange(n_tok, dtype=np.int64)
    ang = np.pi * ((np.outer(k, k) % (2 * n_tok)).astype(np.float64)) / n_tok
    return (jnp.asarray(np.cos(ang), dtype=BF16), jnp.asarray(np.sin(ang), dtype=BF16))


@functools.lru_cache(maxsize=None)
def _hyena_feats(n_tok):
    t = np.arange(n_tok, dtype=np.float64)
    t_lin = t / (n_tok - 1)
    bands = np.arange(1, HY_BANDS + 1, dtype=np.float64)
    ang = (2.0 * np.pi / n_tok) * t[:, None] * bands
    feats = np.concatenate([t_lin[:, None], np.cos(ang), -np.sin(ang)], axis=-1)
    out = np.zeros((n_tok, LANES), np.float64)
    out[:, :feats.shape[1]] = feats
    return jnp.asarray(out, dtype=F32)


def _rope_pattern(pos_row, pos_col, dd):
    q = dd // 4
    freqs = 1.0 / (ROPE_BASE ** (np.arange(q, dtype=np.float64) / q))
    n = pos_row.shape[0]
    cos = np.zeros((n, dd)); sa = np.zeros((n, dd)); sb = np.zeros((n, dd))
    for g, pos in enumerate((pos_row, pos_col)):
        ang = pos[:, None].astype(np.float64) * freqs
        base = g * 2 * q
        cos[:, base:base + q] = np.cos(ang); cos[:, base + q:base + 2 * q] = np.cos(ang)
        sa[:, base:base + q] = -np.sin(ang)
        sb[:, base + q:base + 2 * q] = np.sin(ang)
    return cos, sa, sb


@functools.lru_cache(maxsize=None)
def _rope_tables(n_tok):
    t = np.arange(n_tok)
    row, col = t // GRID_W, t % GRID_W
    ones = np.ones((n_tok, LANES)); zeros = np.zeros((n_tok, LANES))
    c, a, b = _rope_pattern(row, col, DA_HEAD_DIM)
    da = [np.tile(c, (1, 2)), np.tile(a, (1, 2)), np.tile(b, (1, 2))]
    c, a, b = _rope_pattern(row, col, MLA_ROPE)
    mq = [ones.copy(), zeros.copy(), zeros.copy()]
    kr = [ones.copy(), zeros.copy(), zeros.copy()]
    for dst, src in zip(mq, (c, a, b)):
        dst[:, MLA_NOPE:MLA_NOPE + MLA_ROPE] = src
    for dst, src in zip(kr, (c, a, b)):
        dst[:, :MLA_ROPE] = src
    tab = np.stack([np.stack(da), np.stack(mq), np.stack(kr)])
    return jnp.asarray(tab, dtype=F32)


def _rope_apply(x, tab_ref, kind, q):
    cos = tab_ref[kind, 0]
    sa = tab_ref[kind, 1]
    sb = tab_ref[kind, 2]
    outs = []
    for c in range(x.shape[1] // LANES):
        xc = x[:, c * LANES:(c + 1) * LANES]
        outs.append(xc * cos + pltpu.roll(xc, LANES - q, 1) * sa + pltpu.roll(xc, q, 1) * sb)
    return outs[0] if len(outs) == 1 else jnp.concatenate(outs, axis=1)


def _rope_mix(x, rot, tab_ref, kind):
    cos = tab_ref[kind, 0]
    sin = tab_ref[kind, 2] - tab_ref[kind, 1]
    outs = []
    for c in range(x.shape[1] // LANES):
        cols = slice(c * LANES, (c + 1) * LANES)
        outs.append(x[:, cols] * cos + rot[:, cols] * sin)
    return jnp.concatenate(outs, axis=1)


def _rot_half_spec(width, period, groups):
    src = np.arange(width)
    sign = np.zeros(width, np.float32)
    for base in range(0, width, period):
        for start, dd in groups:
            q = dd // 4
            for half in range(2):
                lo = base + start + half * 2 * q
                for i in range(q):
                    src[lo + i], sign[lo + i] = lo + i + q, -1.0
                    src[lo + q + i], sign[lo + q + i] = lo + i, 1.0
    return src, sign


@functools.lru_cache(maxsize=None)
def _da_rot_matrix():
    width = 2 * LANES
    src, sign = _rot_half_spec(width, DA_HEAD_DIM, ((0, DA_HEAD_DIM),))
    p = np.zeros((width, width), np.float32)
    p[src, np.arange(width)] = sign
    return jnp.asarray(p, dtype=BF16)


def _mod_kernel(c_ref, w_ref, b_ref, o_ref):
    c = c_ref[...]
    o_ref[0] = _dot3(c * _sigmoid(c), w_ref[0]) + b_ref[0]


def _modulation(cvec, w_mod, b_mod):
    depth = w_mod.shape[0]
    rows = cvec.shape[0]
    return pl.pallas_call(
        _mod_kernel,
        grid=(depth, 3),
        in_specs=[pl.BlockSpec((rows, D_MODEL), lambda l, j: (0, 0)),
                  pl.BlockSpec((1, D_MODEL, D_MODEL), lambda l, j: (l, 0, j)),
                  pl.BlockSpec((1, 1, D_MODEL), lambda l, j: (l, 0, j))],
        out_specs=pl.BlockSpec((1, rows, D_MODEL), lambda l, j: (l, 0, j)),
        out_shape=jax.ShapeDtypeStruct((depth, rows, 3 * D_MODEL), F32),
        compiler_params=_cparams(("arbitrary", "arbitrary")),
        name="modulation",
    )(cvec, w_mod, b_mod.reshape(depth, 1, 3 * D_MODEL))


def _filt_kernel(feats_ref, w1_ref, b1_ref, w2_ref, b2_ref, w3f_ref, w3b_ref, sf_ref,
                 ldf_ref, ldb_ref, c_ref, s_ref, kr_ref, ki_ref, kny_ref, hid_ref):
    n_tok = feats_ref.shape[0]
    inv_n = 1.0 / (2 * n_tok)

    @pl.when((pl.program_id(1) == 0) & (pl.program_id(2) == 0))
    def _():
        sf = sf_ref[0]
        hid = jnp.sin(sf[0:1] * (_dot3(feats_ref[...], w1_ref[0]) + b1_ref[0]))
        hid_ref[...] = jnp.sin(sf[1:2] * (_dot3(hid, w2_ref[0]) + b2_ref[0]))

    hid = hid_ref[...]
    t_lin = feats_ref[:, 0:1]
    hf = _dot3(hid, w3f_ref[0]) * jnp.exp(-t_lin * jnp.exp(ldf_ref[0]))
    hb = _dot3(hid, w3b_ref[0]) * jnp.exp(-t_lin * jnp.exp(ldb_ref[0]))
    row = lax.broadcasted_iota(jnp.int32, hf.shape, 0)
    hb = jnp.where(row == 0, 0.0, hb)
    a = hf + hb
    d = hf - hb
    wk = jnp.where(row == 0, inv_n, 2.0 * inv_n)
    kr_ref[0, 0] = _dot(c_ref[...], a.astype(BF16)) * wk
    ki_ref[0, 0] = -_dot(s_ref[...], d.astype(BF16)) * wk
    sign = jnp.where((row & 1) == 0, 1.0, -1.0)
    kny_ref[0, 0] = jnp.sum(a * sign, axis=0, keepdims=True) * inv_n


def _hyena_spectra(n_tok, w1p, b1, w2, b2, w3, sf, log_decay):
    depth = w3.shape[0]
    ct = 256
    nc = HY_W // ct
    cmat, smat = _dft_tables(n_tok)
    feats = _hyena_feats(n_tok)
    const = lambda shape: pl.BlockSpec(shape, lambda l, n, c: (0,) * len(shape),
                                       pipeline_mode=pl.Buffered(1))
    per_l = lambda shape: pl.BlockSpec((1,) + shape, lambda l, n, c: (l, 0, 0))
    fcol = lambda l, n, c: (l, 0, n * nc + c)
    bcol = lambda l, n, c: (l, 0, HY_ORDER * nc + n * nc + c)
    out_spec = pl.BlockSpec((1, 1, n_tok, ct), lambda l, n, c: (l, n, 0, c))
    return pl.pallas_call(
        _filt_kernel,
        grid=(depth, HY_ORDER, nc),
        in_specs=[const((n_tok, LANES)),
                  per_l((LANES, HY_FFN)), per_l((1, HY_FFN)),
                  per_l((HY_FFN, HY_FFN)), per_l((1, HY_FFN)),
                  pl.BlockSpec((1, HY_FFN, ct), fcol), pl.BlockSpec((1, HY_FFN, ct), bcol),
                  per_l((2, HY_FFN)),
                  pl.BlockSpec((1, 1, ct), fcol), pl.BlockSpec((1, 1, ct), bcol),
                  const((n_tok, n_tok)), const((n_tok, n_tok))],
        out_specs=[out_spec, out_spec,
                   pl.BlockSpec((1, 1, 1, ct), lambda l, n, c: (l, n, 0, c))],
        out_shape=[jax.ShapeDtypeStruct((depth, HY_ORDER, n_tok, HY_W), F32),
                   jax.ShapeDtypeStruct((depth, HY_ORDER, n_tok, HY_W), F32),
                   jax.ShapeDtypeStruct((depth, HY_ORDER, 1, HY_W), F32)],
        scratch_shapes=[pltpu.VMEM((n_tok, HY_FFN), F32)],
        compiler_params=_cparams(("arbitrary",) * 3),
        name="hyena_spectra",
    )(feats, w1p, b1.reshape(depth, 1, HY_FFN), w2, b2.reshape(depth, 1, HY_FFN), w3, w3, sf,
      log_decay.reshape(depth, 1, -1), log_decay.reshape(depth, 1, -1), cmat, smat)


def _modulate(x, mod_ref):
    return (x * (1.0 + mod_ref[0, 1:2, :]) + mod_ref[0, 0:1, :]).astype(BF16)


def _inproj_kernel(x_ref, mod_ref, w_ref, o_ref, *, gates):
    acc = _dot(_modulate(x_ref[...], mod_ref), w_ref[0])
    if gates:
        tn = acc.shape[1]
        col = pl.program_id(0) * tn + lax.broadcasted_iota(jnp.int32, acc.shape, 1)
        sg = _sigmoid(acc)
        o_ref[...] = jnp.where(col >= N_BRANCH * D_MODEL, acc * sg, sg).astype(BF16)
    else:
        o_ref[...] = acc


def _inproj(x2d, mod, w, l, seq_len, gates):
    t = x2d.shape[0]
    n_out = w.shape[-1]
    per_batch = mod.shape[0] > 1
    tm = min(1024, seq_len if per_batch else t)
    tn = n_out // 2
    tiles_per_batch = seq_len // tm
    mod_idx = (lambda j, i: (i // tiles_per_batch, 0, 0)) if per_batch else (lambda j, i: (0, 0, 0))
    return pl.pallas_call(
        functools.partial(_inproj_kernel, gates=gates),
        grid=(n_out // tn, t // tm),
        in_specs=[pl.BlockSpec((tm, D_MODEL), lambda j, i: (i, 0)),
                  pl.BlockSpec((1, 3, D_MODEL), mod_idx),
                  pl.BlockSpec((1, D_MODEL, tn), lambda j, i: (l, 0, j))],
        out_specs=pl.BlockSpec((tm, tn), lambda j, i: (i, j)),
        out_shape=jax.ShapeDtypeStruct((t, n_out), BF16 if gates else F32),
        compiler_params=_cparams(("arbitrary", "arbitrary")),
        name="inproj_gates" if gates else "inproj",
    )(x2d, mod, w)


def _short_conv(x, w_ref, b_ref):
    n_tok = x.shape[0]
    row = lax.broadcasted_iota(jnp.int32, x.shape, 0)
    prev = jnp.where(row == 0, 0.0, pltpu.roll(x, 1, 0))
    nxt = jnp.where(row == n_tok - 1, 0.0, pltpu.roll(x, n_tok - 1, 0))
    return prev * w_ref[0, 0:1, :] + x * w_ref[0, 1:2, :] + nxt * w_ref[0, 2:3, :] + b_ref[0]


def _hyena_kernel(zin_ref, gin_ref, wz_ref, bz_ref, wg_ref, bg_ref, kr_ref, ki_ref, kny_ref, skip_ref,
                  c_ref, s_ref, o_ref, zb_ref, acc_ref, *, conv_z, tk):
    nb, n_tok, ct = zin_ref.shape
    for i in range(nb):
        z = zin_ref[i]
        if conv_z:
            z = _short_conv(z, wz_ref, bz_ref)
        zb_ref[i] = z.astype(BF16)
        row = lax.broadcasted_iota(jnp.int32, z.shape, 0)
        sign = jnp.where((row & 1) == 0, 1.0, -1.0)
        zny = jnp.sum(z * sign, axis=0, keepdims=True)
        acc_ref[i] = sign * (zny * kny_ref[0, 0]) + z * skip_ref[0, 0]
        for j in range(n_tok // tk):
            rows = slice(j * tk, (j + 1) * tk)
            zr = _dot(c_ref[rows, :], zb_ref[i])
            zs = _dot(s_ref[rows, :], zb_ref[i])
            kr = kr_ref[0, 0, rows, :]
            ki = ki_ref[0, 0, rows, :]
            yr = (zr * kr + zs * ki).astype(BF16)
            ym = (zs * kr - zr * ki).astype(BF16)
            acc_ref[i] += _dot(c_ref[:, rows], yr) + _dot(s_ref[:, rows], ym)
        o_ref[i] = (_short_conv(gin_ref[i], wg_ref, bg_ref) * acc_ref[i]).astype(o_ref.dtype)


def _hyena_order(zin, zblk, gin, gblk, conv_w, conv_b, kr, ki, kny, skip, l, order, conv_z):
    b, n_tok = zin.shape[0], zin.shape[1]
    cmat, smat = _dft_tables(n_tok)
    if n_tok >= 2048:
        ct, nb = 256, 1
    else:
        ct, nb = 512, min(8, b)
    nc = HY_W // ct
    tk = min(256, n_tok)
    const = lambda shape: pl.BlockSpec(shape, lambda c, i: (0,) * len(shape),
                                       pipeline_mode=pl.Buffered(1))
    zspec = pl.BlockSpec((nb, n_tok, ct), lambda c, i: (i, 0, zblk * nc + c))
    gspec = pl.BlockSpec((nb, n_tok, ct), lambda c, i: (i, 0, gblk * nc + c))
    kspec = pl.BlockSpec((1, 1, n_tok, ct), lambda c, i: (l, order, 0, c))
    vspec = pl.BlockSpec((1, 1, 1, ct), lambda c, i: (l, order, 0, c))
    wspec = lambda blk: pl.BlockSpec((1, 3, ct), lambda c, i: (l, 0, blk * nc + c))
    bspec = lambda blk: pl.BlockSpec((1, 1, ct), lambda c, i: (l, 0, blk * nc + c))
    return pl.pallas_call(
        functools.partial(_hyena_kernel, conv_z=conv_z, tk=tk),
        grid=(nc, b // nb),
        in_specs=[zspec, gspec, wspec(0), bspec(0), wspec(order + 1), bspec(order + 1),
                  kspec, kspec, vspec, vspec, const((n_tok, n_tok)), const((n_tok, n_tok))],
        out_specs=pl.BlockSpec((nb, n_tok, ct), lambda c, i: (i, 0, c)),
        out_shape=jax.ShapeDtypeStruct((b, n_tok, HY_W), F32 if conv_z else BF16),
        scratch_shapes=[pltpu.VMEM((nb, n_tok, ct), BF16), pltpu.VMEM((nb, n_tok, ct), F32)],
        compiler_params=_cparams(("arbitrary", "arbitrary")),
        name=f"hyena_order{order}",
    )(zin, gin, conv_w, conv_b, conv_w, conv_b, kr, ki, kny, skip, cmat, smat)


def _rms(x, g, n):
    ms = jnp.sum(x * x, axis=-1, keepdims=True) * (1.0 / n)
    return x * lax.rsqrt(ms + RMS_EPS) * g


def _da_values_ext(v):
    lane = lax.broadcasted_iota(jnp.int32, (v.shape[0], LANES), 1)
    ones_col = jnp.where(lane == 0, 1.0, 0.0).astype(BF16)
    pieces = []
    for h in range(DA_HEADS):
        pieces += [v[:, h * DA_V_DIM:(h + 1) * DA_V_DIM], ones_col]
    return jnp.concatenate(pieces, axis=1)


def _mla_values_ext(ckv_b, wv):
    v = _dot(ckv_b, wv)
    lane = lax.broadcasted_iota(jnp.int32, v.shape, 1)
    return jnp.where((lane & (MLA_V_PAD - 1)) == MLA_V, 1.0, v).astype(BF16)


def _prep_kernel(*refs, rope, n_state_in):
    it = iter(refs)
    q_ref, k_ref, v_ref, sm_ref, qn_ref, wuq_ref, kvn_ref, wk_ref, wv_ref, e_ref = (
        next(it) for _ in range(10))
    tab_ref, wuqr_ref, prot_ref = (next(it) for _ in range(3)) if rope else (None,) * 3
    for _ in range(n_state_in):
        next(it)
    qd_ref, kd_ref, vd_ref, qm_ref, km_ref, vm_ref = (next(it) for _ in range(6))
    q = q_ref[...] * (DA_HEAD_DIM ** -0.5)
    k = k_ref[...]
    v = v_ref[...]
    dq = sm_ref[:, 0:MLA_Q_LORA]
    dkv = sm_ref[:, MLA_Q_LORA:MLA_Q_LORA + MLA_KV_LORA]
    kr = sm_ref[:, 5 * LANES:6 * LANES]
    c_kv = _rms(dkv, kvn_ref[0], MLA_KV_LORA)
    if not rope:
        ks_ref, vs_ref, ckvs_ref, krs_ref = (next(it) for _ in range(4))
        nbt, _, n_tok, _ = ks_ref.shape
        ks_ref[:, 0] = k.reshape(nbt, n_tok, k.shape[1])
        vs_ref[:, 0] = v.reshape(nbt, n_tok, v.shape[1])
        ckvs_ref[:, 0] = c_kv.reshape(nbt, n_tok, MLA_KV_LORA)
        krs_ref[:, 0] = kr[:, :MLA_ROPE].reshape(nbt, n_tok, MLA_ROPE)
    if rope:
        def rot(x):
            hi, lo = _split_bf16(x)
            w = prot_ref.shape[0]
            return jnp.concatenate(
                [_dot(hi[:, c:c + w], prot_ref[...]) + _dot(lo[:, c:c + w], prot_ref[...])
                 for c in range(0, x.shape[1], w)], axis=1)
        q = _rope_mix(q, rot(q), tab_ref, 0)
        k = _rope_mix(k, rot(k), tab_ref, 0)
    qd_ref[...] = q.astype(BF16)
    kd_ref[...] = k.astype(BF16)

    mla_scale = (MLA_NOPE + MLA_ROPE) ** -0.5
    c_q = _rms(dq, qn_ref[0], MLA_Q_LORA).astype(BF16)
    qm = _dot(c_q, wuq_ref[0]) * mla_scale
    ckv_b = c_kv.astype(BF16)
    if rope:
        qm = _rope_mix(qm, _dot(c_q, wuqr_ref[0]) * mla_scale, tab_ref, 1)
        kr = _rope_apply(kr, tab_ref, 2, MLA_ROPE // 4)
        vd_ref[...] = _da_values_ext(v.astype(BF16))
        vm_ref[...] = _mla_values_ext(ckv_b, wv_ref[0])
    else:
        vd_ref[...] = v.astype(BF16)
        vm_ref[...] = _dot(ckv_b, wv_ref[0]).astype(BF16)
    qm_ref[...] = qm.astype(BF16)
    km_ref[...] = (_dot(ckv_b, wk_ref[0]) + _dot(kr.astype(BF16), e_ref[...])).astype(BF16)


def _prep(proj, wts, l, seq_len, rope, states=None):
    t = proj.shape[0]
    tm = min(512, seq_len if rope else t)
    tiles_per_seq = seq_len // tm
    blk = lambda w, idx: pl.BlockSpec((tm, w), lambda i: (i, idx))
    wl = lambda shape: pl.BlockSpec((1,) + shape, lambda i: (l, 0, 0))
    in_specs = [blk(512, COL_Q // 512), blk(512, COL_K // 512), blk(512, COL_V // 512),
                blk(SMALL_W, COL_SMALL // SMALL_W),
                wl((1, MLA_Q_LORA)), wl((MLA_Q_LORA, MLA_HEADS * MLA_HEAD_PAD)),
                wl((1, MLA_KV_LORA)), wl((MLA_KV_LORA, MLA_HEADS * MLA_HEAD_PAD)),
                wl((MLA_KV_LORA, MLA_HEADS * (MLA_V_PAD if rope else MLA_V))),
                pl.BlockSpec((LANES, MLA_HEADS * MLA_HEAD_PAD), lambda i: (0, 0))]
    args = [proj, proj, proj, proj, wts["q_norm"], wts["w_uq"], wts["kv_norm"], wts["w_uk"],
            wts["w_uv_pad" if rope else "w_uv"], wts["e_place"]]
    if rope:
        prot = _da_rot_matrix()
        in_specs += [pl.BlockSpec((3, 3, tm, LANES), lambda i: (0, 0, i % tiles_per_seq, 0)),
                     wl((MLA_Q_LORA, MLA_HEADS * MLA_HEAD_PAD)),
                     pl.BlockSpec(prot.shape, lambda i: (0, 0))]
        args += [_rope_tables(seq_len), wts["w_uq_rot"], prot]
    out = lambda w: pl.BlockSpec((tm, w), lambda i: (i, 0))
    widths = ((512, 512, DA_HEADS * 2 * DA_V_DIM, 1024, 1024, MLA_HEADS * MLA_V_PAD) if rope
              else (512, 512, DA_HEADS * DA_V_DIM, 1024, 1024, MLA_HEADS * MLA_V))
    out_specs = [out(w) for w in widths]
    out_shape = [jax.ShapeDtypeStruct((t, w), BF16) for w in widths]
    aliases = {}
    n_state_in = 0
    if not rope:
        nbt = tm // seq_len
        n_state_in = len(states)
        for k, s in enumerate(states):
            out_specs.append(pl.BlockSpec((nbt, 1, seq_len, s.shape[-1]), lambda i: (i, l, 0, 0)))
            out_shape.append(jax.ShapeDtypeStruct(s.shape, s.dtype))
            aliases[len(args)] = 6 + k
            in_specs.append(pl.BlockSpec(memory_space=pl.ANY))
            args.append(s)
    return pl.pallas_call(
        functools.partial(_prep_kernel, rope=rope, n_state_in=n_state_in),
        grid=(t // tm,),
        in_specs=in_specs,
        out_specs=out_specs,
        out_shape=out_shape,
        input_output_aliases=aliases,
        compiler_params=_cparams(("arbitrary",)),
        name="attn_prep",
    )(*args)


def _prep_cache_kernel(k_ref, v_ref, ckv_ref, kr_ref, wk_ref, wv_ref, e_ref,
                       kd_ref, vd_ref, km_ref, vm_ref):
    kd_ref[0] = k_ref[0, 0].astype(BF16)
    vd_ref[0] = _da_values_ext(v_ref[0, 0].astype(BF16))
    ckv_b = ckv_ref[0, 0].astype(BF16)
    km_ref[0] = (_dot(ckv_b, wk_ref[0]) + _dot(kr_ref[0, 0].astype(BF16), e_ref[...])).astype(BF16)
    vm_ref[0] = _mla_values_ext(ckv_b, wv_ref[0])


def _prep_cache(cache_k, cache_v, cache_ckv, cache_kr, wts, l):
    b, _, p = cache_k.shape[:3]
    ck = cache_k.reshape(b, -1, p, 512)
    cv = cache_v.reshape(b, -1, p, 512)
    cin = lambda w: pl.BlockSpec((1, 1, p, w), lambda i: (i, l, 0, 0))
    wl = lambda shape: pl.BlockSpec((1,) + shape, lambda i: (l, 0, 0))
    out = lambda w: pl.BlockSpec((1, p, w), lambda i: (i, 0, 0))
    return pl.pallas_call(
        _prep_cache_kernel,
        grid=(b,),
        in_specs=[cin(512), cin(512), cin(MLA_KV_LORA), cin(MLA_ROPE),
                  wl((MLA_KV_LORA, MLA_HEADS * MLA_HEAD_PAD)), wl((MLA_KV_LORA, MLA_HEADS * MLA_V_PAD)),
                  pl.BlockSpec((MLA_ROPE, MLA_HEADS * MLA_HEAD_PAD), lambda i: (0, 0))],
        out_specs=[out(512), out(1024), out(1024), out(MLA_HEADS * MLA_V_PAD)],
        out_shape=[jax.ShapeDtypeStruct((b, p, 512), BF16), jax.ShapeDtypeStruct((b, p, 1024), BF16),
                   jax.ShapeDtypeStruct((b, p, 1024), BF16),
                   jax.ShapeDtypeStruct((b, p, MLA_HEADS * MLA_V_PAD), BF16)],
        compiler_params=_cparams(("arbitrary",)),
        name="attn_prep_cache",
    )(ck, cv, cache_ckv, cache_kr, wts["w_uk"], wts["w_uv_pad"], wts["e_place"][:MLA_ROPE])


def _softmax_pv(q, k_parts, v_parts, dv, den_on_mxu):
    s = [_dot_nt(q, k) for k in k_parts]
    m = s[0].max(axis=-1, keepdims=True)
    for sp in s[1:]:
        m = jnp.maximum(m, sp.max(axis=-1, keepdims=True))
    o = None
    den = None
    for sp, v in zip(s, v_parts):
        if den_on_mxu:
            pv = _dot(jnp.exp((sp - m).astype(BF16)), v)
        else:
            p = jnp.exp(sp - m)
            psum = p.sum(axis=-1, keepdims=True)
            den = psum if den is None else den + psum
            pv = _dot(p.astype(BF16), v)
        o = pv if o is None else o + pv
    if den_on_mxu:
        return o[:, :dv] / o[:, dv:dv + 1]
    return o / den


def _da_kernel(*refs, lam_init, has_ctx):
    if has_ctx:
        q_ref, kn_ref, vn_ref, kc_ref, vc_ref, lam_ref, g_ref, o_ref = refs
    else:
        q_ref, kn_ref, vn_ref, lam_ref, g_ref, o_ref = refs
    lp = lam_ref[0]
    lam = (jnp.exp(jnp.sum(lp[0:1] * lp[1:2], axis=1, keepdims=True))
           - jnp.exp(jnp.sum(lp[2:3] * lp[3:4], axis=1, keepdims=True)) + lam_init)
    for bi in range(q_ref.shape[0]):
        for h in range(DA_HEADS):
            vw = 2 * DA_V_DIM if has_ctx else DA_V_DIM
            vcols = slice(h * vw, (h + 1) * vw)
            outs = []
            for j in range(2):
                cols = slice((j * DA_HEADS + h) * DA_HEAD_DIM, (j * DA_HEADS + h + 1) * DA_HEAD_DIM)
                k_parts = [kn_ref[bi, :, cols]]
                v_parts = [vn_ref[bi, :, vcols]]
                if has_ctx:
                    k_parts.insert(0, kc_ref[bi, :, cols])
                    v_parts.insert(0, vc_ref[bi, :, vcols])
                outs.append(_softmax_pv(q_ref[bi, :, cols], k_parts, v_parts, DA_V_DIM, has_ctx))
            o = outs[0] - lam * outs[1]
            o_ref[bi, :, h * DA_V_DIM:(h + 1) * DA_V_DIM] = (
                _rms(o, g_ref[0], DA_V_DIM) * (1.0 - lam_init)).astype(o_ref.dtype)


def _da_attention(qd, kd, vd, ctx, da_lambda, da_subln, l, tq, nbq):
    b, n, _ = qd.shape
    has_ctx = ctx is not None
    lam_init = 0.8 - 0.6 * math.exp(-0.3 * l)
    full = lambda a: pl.BlockSpec((nbq,) + a.shape[1:], lambda i, j: (i, 0, 0))
    in_specs = [pl.BlockSpec((nbq, tq, 512), lambda i, j: (i, j, 0)), full(kd), full(vd)]
    args = [qd, kd, vd]
    if has_ctx:
        in_specs += [full(ctx[0]), full(ctx[1])]
        args += [ctx[0], ctx[1]]
    in_specs += [pl.BlockSpec((1, 4, DA_HEAD_DIM), lambda i, j: (l, 0, 0)),
                 pl.BlockSpec((1, 1, DA_V_DIM), lambda i, j: (l, 0, 0))]
    args += [da_lambda, da_subln.reshape(-1, 1, DA_V_DIM)]
    return pl.pallas_call(
        functools.partial(_da_kernel, lam_init=lam_init, has_ctx=has_ctx),
        grid=(b // nbq, n // tq),
        in_specs=in_specs,
        out_specs=pl.BlockSpec((nbq, tq, 512), lambda i, j: (i, j, 0)),
        out_shape=jax.ShapeDtypeStruct((b, n, 512), BF16),
        compiler_params=_cparams(("arbitrary", "arbitrary")),
        name="diff_attention",
    )(*args)


def _mla_kernel(*refs, has_ctx):
    if has_ctx:
        q_ref, kn_ref, vn_ref, kc_ref, vc_ref, o_ref = refs
    else:
        q_ref, kn_ref, vn_ref, o_ref = refs
    for bi in range(q_ref.shape[0]):
        for h in range(MLA_HEADS):
            cols = slice(h * MLA_HEAD_PAD, (h + 1) * MLA_HEAD_PAD)
            vw = MLA_V_PAD if has_ctx else MLA_V
            vcols = slice(h * vw, (h + 1) * vw)
            k_parts = [kn_ref[bi, :, cols]]
            v_parts = [vn_ref[bi, :, vcols]]
            if has_ctx:
                k_parts.insert(0, kc_ref[bi, :, cols])
                v_parts.insert(0, vc_ref[bi, :, vcols])
            o_ref[bi, :, h * MLA_V:(h + 1) * MLA_V] = _softmax_pv(
                q_ref[bi, :, cols], k_parts, v_parts, MLA_V, has_ctx).astype(o_ref.dtype)


def _mla_attention(qm, km, vm, ctx, tq, nbq):
    b, n, _ = qm.shape
    has_ctx = ctx is not None
    full = lambda a: pl.BlockSpec((nbq,) + a.shape[1:], lambda i, j: (i, 0, 0))
    in_specs = [pl.BlockSpec((nbq, tq, 1024), lambda i, j: (i, j, 0)), full(km), full(vm)]
    args = [qm, km, vm]
    if has_ctx:
        in_specs += [full(ctx[0]), full(ctx[1])]
        args += [ctx[0], ctx[1]]
    return pl.pallas_call(
        functools.partial(_mla_kernel, has_ctx=has_ctx),
        grid=(b // nbq, n // tq),
        in_specs=in_specs,
        out_specs=pl.BlockSpec((nbq, tq, 512), lambda i, j: (i, j, 0)),
        out_shape=jax.ShapeDtypeStruct((b, n, 512), BF16),
        compiler_params=_cparams(("arbitrary", "arbitrary")),
        name="latent_attention",
    )(*args)


def _outproj_kernel(x_ref, mod_ref, yh_ref, yd_ref, ym_ref, p0_ref, p1_ref, p2_ref,
                    m0_ref, m1_ref, m2_ref, wb_ref, wo_ref, g_ref, b_ref, o_ref, *, alpha):
    merged = None
    for y_ref, p_ref, m_ref, n in ((yh_ref, p0_ref, m0_ref, 0), (yd_ref, p1_ref, m1_ref, 1),
                                   (ym_ref, p2_ref, m2_ref, 2)):
        br = (y_ref[...].astype(F32) * p_ref[...].astype(F32)).astype(BF16)
        term = m_ref[...].astype(F32) * _dot(br, wb_ref[0, n])
        merged = term if merged is None else merged + term
    out = _dot(merged.astype(BF16), wo_ref[0])
    gate = mod_ref[0, 2:3, :]
    y = alpha * x_ref[...] + gate * out
    mu = jnp.mean(y, axis=-1, keepdims=True)
    yc = y - mu
    var = jnp.mean(yc * yc, axis=-1, keepdims=True)
    o_ref[...] = yc * lax.rsqrt(var + LN_EPS) * g_ref[0] + b_ref[0]


def _outproj(x2d, mod, y_hy, y_da, y_mla, gates, w_branch, w_out, ln_g, ln_b, l, seq_len, alpha):
    t = x2d.shape[0]
    per_batch = mod.shape[0] > 1
    tm = min(512, seq_len if per_batch else t)
    tiles_per_batch = seq_len // tm
    mod_idx = (lambda i: (i // tiles_per_batch, 0, 0)) if per_batch else (lambda i: (0, 0, 0))
    row = lambda w, idx=0: pl.BlockSpec((tm, w), lambda i: (i, idx))
    depth = ln_g.shape[0]
    return pl.pallas_call(
        functools.partial(_outproj_kernel, alpha=alpha),
        grid=(t // tm,),
        in_specs=[row(D_MODEL), pl.BlockSpec((1, 3, D_MODEL), mod_idx),
                  row(BR_W), row(BR_W), row(BR_W),
                  row(BR_W, GATE_PATHS // BR_W), row(BR_W, GATE_PATHS // BR_W + 1),
                  row(BR_W, GATE_PATHS // BR_W + 2),
                  row(D_MODEL, GATE_MERGE // D_MODEL), row(D_MODEL, GATE_MERGE // D_MODEL + 1),
                  row(D_MODEL, GATE_MERGE // D_MODEL + 2),
                  pl.BlockSpec((1, N_BRANCH, BR_W, D_MODEL), lambda i: (l, 0, 0, 0)),
                  pl.BlockSpec((1, D_MODEL, D_MODEL), lambda i: (l, 0, 0)),
                  pl.BlockSpec((1, 1, D_MODEL), lambda i: (l, 0, 0)),
                  pl.BlockSpec((1, 1, D_MODEL), lambda i: (l, 0, 0))],
        out_specs=row(D_MODEL),
        out_shape=jax.ShapeDtypeStruct((t, D_MODEL), F32),
        compiler_params=_cparams(("arbitrary",)),
        name="outproj",
    )(x2d, mod, y_hy, y_da, y_mla, gates, gates, gates, gates, gates, gates, w_branch, w_out,
      ln_g.reshape(depth, 1, D_MODEL), ln_b.reshape(depth, 1, D_MODEL))


def _layer(x, mod, l, wts, spectra, ctx=None, states=None):
    b, n, _ = x.shape
    depth = wts["ln_g"].shape[0]
    alpha = (2 * depth) ** 0.25
    x2d = x.reshape(b * n, D_MODEL)
    proj = _inproj(x2d, mod, wts["w_main"], l, n, gates=False)
    gates = _inproj(x2d, mod, wts["w_gate"], l, n, gates=True)
    proj3 = proj.reshape(b, n, D_IN_MAIN)

    kr_s, ki_s, kny_s = spectra
    hy_blk = COL_HY // HY_W
    conv = (wts["conv_w"], wts["conv_b"])
    z2 = _hyena_order(proj3, hy_blk, proj3, hy_blk + 1, *conv, kr_s, ki_s, kny_s, wts["skip"], l, 0, True)
    y_hy = _hyena_order(z2, 0, proj3, hy_blk + 2, *conv, kr_s, ki_s, kny_s, wts["skip"], l, 1, False)

    prep_out = _prep(proj, wts, l, n, rope=ctx is not None, states=states)
    r3 = lambda a: a.reshape(b, n, a.shape[-1])
    qd, kd, vd, qm, km, vm = map(r3, prep_out[:6])
    new_states = tuple(prep_out[6:]) if ctx is None else None
    if ctx is not None:
        kd_c, vd_c, km_c, vm_c = _prep_cache(*ctx, wts, l)
        da_ctx, mla_ctx = (kd_c, vd_c), (km_c, vm_c)
    else:
        da_ctx = mla_ctx = None
    tq = min(ATTN_TQ, n)
    nbq = 2 if (ctx is None and b % 2 == 0) else 1
    y_da = _da_attention(qd, kd, vd, da_ctx, wts["da_lambda"], wts["da_subln"], l, tq, nbq)
    y_mla = _mla_attention(qm, km, vm, mla_ctx, tq, nbq)

    x_new = _outproj(x2d, mod, y_hy.reshape(b * n, HY_W), y_da.reshape(b * n, BR_W),
                     y_mla.reshape(b * n, BR_W), gates, wts["w_branch"], wts["w_out"],
                     wts["ln_g"], wts["ln_b"], l, n, alpha)
    return x_new.reshape(b, n, D_MODEL), new_states


def _prepare_weights(w_in, hy_conv_w, hy_conv_b, hy_ffn_w1, hy_skip, da_lambda, da_subln,
                     mla_q_norm, mla_w_uq, mla_kv_norm, mla_w_ukv, w_branch, w_out, ln_g, ln_b):
    depth = w_in.shape[0]
    w_main = jnp.pad(w_in[..., :ORIG_PATHS].astype(BF16),
                     ((0, 0), (0, 0), (0, SMALL_W - SMALL_USED)))
    w_gate = jnp.concatenate([w_in[..., ORIG_MERGE:ORIG_END], w_in[..., ORIG_PATHS:ORIG_MERGE]],
                             axis=-1).astype(BF16)
    uq = mla_w_uq.reshape(depth, MLA_Q_LORA, MLA_HEADS, MLA_NOPE + MLA_ROPE)
    uq = jnp.pad(uq, ((0, 0), (0, 0), (0, 0), (0, MLA_HEAD_PAD - MLA_NOPE - MLA_ROPE)))
    w_uq = uq.reshape(depth, MLA_Q_LORA, MLA_HEADS * MLA_HEAD_PAD).astype(BF16)
    ukv = mla_w_ukv.reshape(depth, MLA_KV_LORA, MLA_HEADS, MLA_NOPE + MLA_V)
    uk = jnp.pad(ukv[..., :MLA_NOPE], ((0, 0), (0, 0), (0, 0), (0, MLA_HEAD_PAD - MLA_NOPE)))
    w_uk = uk.reshape(depth, MLA_KV_LORA, MLA_HEADS * MLA_HEAD_PAD).astype(BF16)
    w_uv = ukv[..., MLA_NOPE:].reshape(depth, MLA_KV_LORA, MLA_HEADS * MLA_V).astype(BF16)
    uv = jnp.pad(ukv[..., MLA_NOPE:], ((0, 0), (0, 0), (0, 0), (0, MLA_V_PAD - MLA_V)))
    w_uv_pad = uv.reshape(depth, MLA_KV_LORA, MLA_HEADS * MLA_V_PAD).astype(BF16)
    src, sign = _rot_half_spec(MLA_HEADS * MLA_HEAD_PAD, MLA_HEAD_PAD, ((MLA_NOPE, MLA_ROPE),))
    w_uq_rot = w_uq[..., src] * jnp.asarray(sign, dtype=BF16)
    e = np.zeros((LANES, MLA_HEADS * MLA_HEAD_PAD), np.float32)
    for h in range(MLA_HEADS):
        for i in range(MLA_ROPE):
            e[i, h * MLA_HEAD_PAD + MLA_NOPE + i] = 1.0
    w1p = jnp.pad(hy_ffn_w1, ((0, 0), (0, LANES - hy_ffn_w1.shape[1]), (0, 0)))
    return {
        "w_main": w_main, "w_gate": w_gate, "conv_w": hy_conv_w, "conv_b": hy_conv_b.reshape(depth, 1, -1),
        "skip": hy_skip.reshape(depth, HY_ORDER, 1, HY_W), "w1p": w1p,
        "da_lambda": da_lambda, "da_subln": da_subln,
        "q_norm": mla_q_norm.reshape(depth, 1, -1), "kv_norm": mla_kv_norm.reshape(depth, 1, -1),
        "w_uq": w_uq, "w_uq_rot": w_uq_rot, "w_uk": w_uk, "w_uv": w_uv, "w_uv_pad": w_uv_pad, "e_place": jnp.asarray(e, dtype=BF16),
        "w_branch": w_branch.astype(BF16), "w_out": w_out.astype(BF16), "ln_g": ln_g, "ln_b": ln_b,
    }


def kernel(x_prompt, x_sample, c, cache_diff_k, cache_diff_v, cache_mla_ckv, cache_mla_krope, c_ctx, w_mod, b_mod, w_in, hy_conv_w, hy_conv_b, hy_ffn_w1, hy_ffn_b1, hy_ffn_w2, hy_ffn_b2, hy_ffn_w3, hy_sin_freq, hy_log_decay, hy_skip, da_lambda, da_subln, mla_q_norm, mla_w_uq, mla_kv_norm, mla_w_ukv, w_branch, w_out, ln_g, ln_b):
    depth = w_in.shape[0]
    b_ctx, n_ctx, _ = x_prompt.shape
    b_lat, n_lat, _ = x_sample.shape
    wts = _prepare_weights(w_in, hy_conv_w, hy_conv_b, hy_ffn_w1, hy_skip, da_lambda, da_subln,
                           mla_q_norm, mla_w_uq, mla_kv_norm, mla_w_ukv, w_branch, w_out, ln_g, ln_b)

    rows = -(-(1 + b_lat) // 8) * 8
    cvec = jnp.zeros((rows, D_MODEL), F32).at[0].set(c_ctx).at[1:1 + b_lat].set(c)
    mods = _modulation(cvec, w_mod, b_mod).reshape(depth, rows, 3, D_MODEL)

    spec_args = (wts["w1p"], hy_ffn_b1, hy_ffn_w2, hy_ffn_b2, hy_ffn_w3, hy_sin_freq, hy_log_decay)
    spectra_ctx = _hyena_spectra(n_ctx, *spec_args)
    spectra_lat = spectra_ctx if n_lat == n_ctx else _hyena_spectra(n_lat, *spec_args)

    xp = x_prompt
    states = tuple(jnp.zeros((b_ctx, depth, n_ctx, w), F32)
                   for w in (2 * DA_HEADS * DA_HEAD_DIM, DA_HEADS * DA_V_DIM, MLA_KV_LORA, MLA_ROPE))
    for l in range(depth):
        xp, states = _layer(xp, mods[l, 0:1], l, wts, spectra_ctx, states=states)
    state_k = states[0].reshape(b_ctx, depth, n_ctx, 2, DA_HEADS, DA_HEAD_DIM)
    state_v = states[1].reshape(b_ctx, depth, n_ctx, DA_HEADS, DA_V_DIM)
    state_ckv, state_kr = states[2], states[3]

    xs = x_sample
    ctx = (cache_diff_k, cache_diff_v, cache_mla_ckv, cache_mla_krope)
    for l in range(depth):
        xs, _ = _layer(xs, mods[l, 1:1 + b_lat], l, wts, spectra_lat, ctx=ctx)

    return (xp, xs, state_k, state_v, state_ckv, state_kr)
```

```python
import functools
import math

import numpy as np
import jax
import jax.numpy as jnp
from jax import lax
from jax.experimental import pallas as pl
from jax.experimental.pallas import tpu as pltpu

F32 = jnp.float32
BF16 = jnp.bfloat16

D_MODEL = 1024
N_BRANCH = 3
BR_W = 512
HY_W = 512
HY_ORDER = 2
HY_BANDS = 16
HY_FFN = 64
DA_HEADS = 4
DA_HEAD_DIM = 64
DA_V_DIM = 2 * DA_HEAD_DIM
MLA_HEADS = 8
MLA_NOPE = 64
MLA_ROPE = 32
MLA_V = 64
MLA_Q_LORA = 384
MLA_KV_LORA = 256
GRID_W = 64
ROPE_BASE = 10000.0
LN_EPS = 1e-5
RMS_EPS = 1e-6

LANES = 128
MLA_HEAD_PAD = LANES
MLA_V_PAD = 2 * LANES
VMEM_LIMIT = 56 * 1024 * 1024
ATTN_TQ = 512

D_IN_HY = 3 * HY_W
SMALL_W = 1024
SMALL_USED = MLA_Q_LORA + MLA_KV_LORA + MLA_ROPE
COL_HY = 0
COL_Q = D_IN_HY
COL_K = COL_Q + 2 * DA_HEADS * DA_HEAD_DIM
COL_V = COL_K + 2 * DA_HEADS * DA_HEAD_DIM
COL_SMALL = COL_V + DA_HEADS * DA_V_DIM
D_IN_MAIN = COL_SMALL + SMALL_W
GATE_MERGE = 0
GATE_PATHS = N_BRANCH * D_MODEL
ORIG_PATHS = COL_SMALL + SMALL_USED
ORIG_MERGE = ORIG_PATHS + N_BRANCH * BR_W
ORIG_END = ORIG_MERGE + N_BRANCH * D_MODEL


def _cparams(sem):
    return pltpu.CompilerParams(dimension_semantics=sem, vmem_limit_bytes=VMEM_LIMIT)


def _split_bf16(x):
    hi = x.astype(BF16)
    lo = (x - hi.astype(F32)).astype(BF16)
    return hi, lo


def _dot(a, b):
    return jnp.dot(a, b, preferred_element_type=F32)


def _dot_nt(a, b):
    return lax.dot_general(a, b, (((1,), (1,)), ((), ())), preferred_element_type=F32)


def _dot3(a, b):
    ah, al = _split_bf16(a)
    bh, bl = _split_bf16(b)
    return _dot(ah, bh) + _dot(ah, bl) + _dot(al, bh)


def _sigmoid(x):
    return 0.5 * jnp.tanh(0.5 * x) + 0.5


@functools.lru_cache(maxsize=None)
def _dft_tables(n_tok):
    k = np.arange(n_tok, dtype=np.int64)
    ang = np.pi * ((np.outer(k, k) % (2 * n_tok)).astype(np.float64)) / n_tok
    return (jnp.asarray(np.cos(ang), dtype=BF16), jnp.asarray(np.sin(ang), dtype=BF16))


@functools.lru_cache(maxsize=None)
def _hyena_feats(n_tok):
    t = np.arange(n_tok, dtype=np.float64)
    t_lin = t / (n_tok - 1)
    bands = np.arange(1, HY_BANDS + 1, dtype=np.float64)
    ang = (2.0 * np.pi / n_tok) * t[:, None] * bands
    feats = np.concatenate([t_lin[:, None], np.cos(ang), -np.sin(ang)], axis=-1)
    out = np.zeros((n_tok, LANES), np.float64)
    out[:, :feats.shape[1]] = feats
    return jnp.asarray(out, dtype=F32)


def _rope_pattern(pos_row, pos_col, dd):
    q = dd // 4
    freqs = 1.0 / (ROPE_BASE ** (np.arange(q, dtype=np.float64) / q))
    n = pos_row.shape[0]
    cos = np.zeros((n, dd)); sa = np.zeros((n, dd)); sb = np.zeros((n, dd))
    for g, pos in enumerate((pos_row, pos_col)):
        ang = pos[:, None].astype(np.float64) * freqs
        base = g * 2 * q
        cos[:, base:base + q] = np.cos(ang); cos[:, base + q:base + 2 * q] = np.cos(ang)
        sa[:, base:base + q] = -np.sin(ang)
        sb[:, base + q:base + 2 * q] = np.sin(ang)
    return cos, sa, sb


@functools.lru_cache(maxsize=None)
def _rope_tables(n_tok):
    t = np.arange(n_tok)
    row, col = t // GRID_W, t % GRID_W
    ones = np.ones((n_tok, LANES)); zeros = np.zeros((n_tok, LANES))
    c, a, b = _rope_pattern(row, col, DA_HEAD_DIM)
    da = [np.tile(c, (1, 2)), np.tile(a, (1, 2)), np.tile(b, (1, 2))]
    c, a, b = _rope_pattern(row, col, MLA_ROPE)
    mq = [ones.copy(), zeros.copy(), zeros.copy()]
    kr = [ones.copy(), zeros.copy(), zeros.copy()]
    for dst, src in zip(mq, (c, a, b)):
        dst[:, MLA_NOPE:MLA_NOPE + MLA_ROPE] = src
    for dst, src in zip(kr, (c, a, b)):
        dst[:, :MLA_ROPE] = src
    tab = np.stack([np.stack(da), np.stack(mq), np.stack(kr)])
    return jnp.asarray(tab, dtype=F32)


def _rope_apply(x, tab_ref, kind, q):
    cos = tab_ref[kind, 0]
    sa = tab_ref[kind, 1]
    sb = tab_ref[kind, 2]
    outs = []
    for c in range(x.shape[1] // LANES):
        xc = x[:, c * LANES:(c + 1) * LANES]
        outs.append(xc * cos + pltpu.roll(xc, LANES - q, 1) * sa + pltpu.roll(xc, q, 1) * sb)
    return outs[0] if len(outs) == 1 else jnp.concatenate(outs, axis=1)


def _rope_mix(x, rot, tab_ref, kind):
    cos = tab_ref[kind, 0]
    sin = tab_ref[kind, 2] - tab_ref[kind, 1]
    outs = []
    for c in range(x.shape[1] // LANES):
        cols = slice(c * LANES, (c + 1) * LANES)
        outs.append(x[:, cols] * cos + rot[:, cols] * sin)
    return jnp.concatenate(outs, axis=1)


def _rot_half_spec(width, period, groups):
    src = np.arange(width)
    sign = np.zeros(width, np.float32)
    for base in range(0, width, period):
        for start, dd in groups:
            q = dd // 4
            for half in range(2):
                lo = base + start + half * 2 * q
                for i in range(q):
                    src[lo + i], sign[lo + i] = lo + i + q, -1.0
                    src[lo + q + i], sign[lo + q + i] = lo + i, 1.0
    return src, sign


@functools.lru_cache(maxsize=None)
def _da_rot_matrix():
    width = 2 * LANES
    src, sign = _rot_half_spec(width, DA_HEAD_DIM, ((0, DA_HEAD_DIM),))
    p = np.zeros((width, width), np.float32)
    p[src, np.arange(width)] = sign
    return jnp.asarray(p, dtype=BF16)


def _mod_kernel(c_ref, w_ref, b_ref, o_ref):
    c = c_ref[...]
    o_ref[0] = _dot3(c * _sigmoid(c), w_ref[0]) + b_ref[0]


def _modulation(cvec, w_mod, b_mod):
    depth = w_mod.shape[0]
    rows = cvec.shape[0]
    return pl.pallas_call(
        _mod_kernel,
        grid=(depth, 3),
        in_specs=[pl.BlockSpec((rows, D_MODEL), lambda l, j: (0, 0)),
                  pl.BlockSpec((1, D_MODEL, D_MODEL), lambda l, j: (l, 0, j)),
                  pl.BlockSpec((1, 1, D_MODEL), lambda l, j: (l, 0, j))],
        out_specs=pl.BlockSpec((1, rows, D_MODEL), lambda l, j: (l, 0, j)),
        out_shape=jax.ShapeDtypeStruct((depth, rows, 3 * D_MODEL), F32),
        compiler_params=_cparams(("arbitrary", "arbitrary")),
        name="modulation",
    )(cvec, w_mod, b_mod.reshape(depth, 1, 3 * D_MODEL))


def _filt_kernel(feats_ref, w1_ref, b1_ref, w2_ref, b2_ref, w3f_ref, w3b_ref, sf_ref,
                 ldf_ref, ldb_ref, c_ref, s_ref, kr_ref, ki_ref, kny_ref, hid_ref):
    n_tok = feats_ref.shape[0]
    inv_n = 1.0 / (2 * n_tok)

    @pl.when((pl.program_id(1) == 0) & (pl.program_id(2) == 0))
    def _():
        sf = sf_ref[0]
        hid = jnp.sin(sf[0:1] * (_dot3(feats_ref[...], w1_ref[0]) + b1_ref[0]))
        hid_ref[...] = jnp.sin(sf[1:2] * (_dot3(hid, w2_ref[0]) + b2_ref[0]))

    hid = hid_ref[...]
    t_lin = feats_ref[:, 0:1]
    hf = _dot3(hid, w3f_ref[0]) * jnp.exp(-t_lin * jnp.exp(ldf_ref[0]))
    hb = _dot3(hid, w3b_ref[0]) * jnp.exp(-t_lin * jnp.exp(ldb_ref[0]))
    row = lax.broadcasted_iota(jnp.int32, hf.shape, 0)
    hb = jnp.where(row == 0, 0.0, hb)
    a = hf + hb
    d = hf - hb
    wk = jnp.where(row == 0, inv_n, 2.0 * inv_n)
    kr_ref[0, 0] = _dot(c_ref[...], a.astype(BF16)) * wk
    ki_ref[0, 0] = -_dot(s_ref[...], d.astype(BF16)) * wk
    sign = jnp.where((row & 1) == 0, 1.0, -1.0)
    kny_ref[0, 0] = jnp.sum(a * sign, axis=0, keepdims=True) * inv_n


def _hyena_spectra(n_tok, w1p, b1, w2, b2, w3, sf, log_decay):
    depth = w3.shape[0]
    ct = 256
    nc = HY_W // ct
    cmat, smat = _dft_tables(n_tok)
    feats = _hyena_feats(n_tok)
    const = lambda shape: pl.BlockSpec(shape, lambda l, n, c: (0,) * len(shape),
                                       pipeline_mode=pl.Buffered(1))
    per_l = lambda shape: pl.BlockSpec((1,) + shape, lambda l, n, c: (l, 0, 0))
    fcol = lambda l, n, c: (l, 0, n * nc + c)
    bcol = lambda l, n, c: (l, 0, HY_ORDER * nc + n * nc + c)
    out_spec = pl.BlockSpec((1, 1, n_tok, ct), lambda l, n, c: (l, n, 0, c))
    return pl.pallas_call(
        _filt_kernel,
        grid=(depth, HY_ORDER, nc),
        in_specs=[const((n_tok, LANES)),
                  per_l((LANES, HY_FFN)), per_l((1, HY_FFN)),
                  per_l((HY_FFN, HY_FFN)), per_l((1, HY_FFN)),
                  pl.BlockSpec((1, HY_FFN, ct), fcol), pl.BlockSpec((1, HY_FFN, ct), bcol),
                  per_l((2, HY_FFN)),
                  pl.BlockSpec((1, 1, ct), fcol), pl.BlockSpec((1, 1, ct), bcol),
                  const((n_tok, n_tok)), const((n_tok, n_tok))],
        out_specs=[out_spec, out_spec,
                   pl.BlockSpec((1, 1, 1, ct), lambda l, n, c: (l, n, 0, c))],
        out_shape=[jax.ShapeDtypeStruct((depth, HY_ORDER, n_tok, HY_W), F32),
                   jax.ShapeDtypeStruct((depth, HY_ORDER, n_tok, HY_W), F32),
                   jax.ShapeDtypeStruct((depth, HY_ORDER, 1, HY_W), F32)],
        scratch_shapes=[pltpu.VMEM((n_tok, HY_FFN), F32)],
        compiler_params=_cparams(("arbitrary",) * 3),
        name="hyena_spectra",
    )(feats, w1p, b1.reshape(depth, 1, HY_FFN), w2, b2.reshape(depth, 1, HY_FFN), w3, w3, sf,
      log_decay.reshape(depth, 1, -1), log_decay.reshape(depth, 1, -1), cmat, smat)


def _modulate(x, mod_ref):
    return (x * (1.0 + mod_ref[0, 1:2, :]) + mod_ref[0, 0:1, :]).astype(BF16)


def _inproj_kernel(x_ref, mod_ref, w_ref, o_ref, *, gates):
    acc = _dot(_modulate(x_ref[...], mod_ref), w_ref[0])
    if gates:
        tn = acc.shape[1]
        col = pl.program_id(0) * tn + lax.broadcasted_iota(jnp.int32, acc.shape, 1)
        sg = _sigmoid(acc)
        o_ref[...] = jnp.where(col >= N_BRANCH * D_MODEL, acc * sg, sg).astype(BF16)
    else:
        o_ref[...] = acc


def _inproj(x2d, mod, w, l, seq_len, gates):
    t = x2d.shape[0]
    n_out = w.shape[-1]
    per_batch = mod.shape[0] > 1
    tm = min(1024, seq_len if per_batch else t)
    tn = n_out // 2
    tiles_per_batch = seq_len // tm
    mod_idx = (lambda j, i: (i // tiles_per_batch, 0, 0)) if per_batch else (lambda j, i: (0, 0, 0))
    return pl.pallas_call(
        functools.partial(_inproj_kernel, gates=gates),
        grid=(n_out // tn, t // tm),
        in_specs=[pl.BlockSpec((tm, D_MODEL), lambda j, i: (i, 0)),
                  pl.BlockSpec((1, 3, D_MODEL), mod_idx),
                  pl.BlockSpec((1, D_MODEL, tn), lambda j, i: (l, 0, j))],
        out_specs=pl.BlockSpec((tm, tn), lambda j, i: (i, j)),
        out_shape=jax.ShapeDtypeStruct((t, n_out), BF16 if gates else F32),
        compiler_params=_cparams(("arbitrary", "arbitrary")),
        name="inproj_gates" if gates else "inproj",
    )(x2d, mod, w)


def _short_conv(x, w_ref, b_ref):
    n_tok = x.shape[0]
    row = lax.broadcasted_iota(jnp.int32, x.shape, 0)
    prev = jnp.where(row == 0, 0.0, pltpu.roll(x, 1, 0))
    nxt = jnp.where(row == n_tok - 1, 0.0, pltpu.roll(x, n_tok - 1, 0))
    return prev * w_ref[0, 0:1, :] + x * w_ref[0, 1:2, :] + nxt * w_ref[0, 2:3, :] + b_ref[0]


def _hyena_kernel(zin_ref, gin_ref, wz_ref, bz_ref, wg_ref, bg_ref, kr_ref, ki_ref, kny_ref, skip_ref,
                  c_ref, s_ref, o_ref, zb_ref, acc_ref, *, conv_z, tk):
    nb, n_tok, ct = zin_ref.shape
    for i in range(nb):
        z = zin_ref[i]
        if conv_z:
            z = _short_conv(z, wz_ref, bz_ref)
        zb_ref[i] = z.astype(BF16)
        row = lax.broadcasted_iota(jnp.int32, z.shape, 0)
        sign = jnp.where((row & 1) == 0, 1.0, -1.0)
        zny = jnp.sum(z * sign, axis=0, keepdims=True)
        acc_ref[i] = sign * (zny * kny_ref[0, 0]) + z * skip_ref[0, 0]
        for j in range(n_tok // tk):
            rows = slice(j * tk, (j + 1) * tk)
            zr = _dot(c_ref[rows, :], zb_ref[i])
            zs = _dot(s_ref[rows, :], zb_ref[i])
            kr = kr_ref[0, 0, rows, :]
            ki = ki_ref[0, 0, rows, :]
            yr = (zr * kr + zs * ki).astype(BF16)
            ym = (zs * kr - zr * ki).astype(BF16)
            acc_ref[i] += _dot(c_ref[:, rows], yr) + _dot(s_ref[:, rows], ym)
        o_ref[i] = (_short_conv(gin_ref[i], wg_ref, bg_ref) * acc_ref[i]).astype(o_ref.dtype)


def _hyena_order(zin, zblk, gin, gblk, conv_w, conv_b, kr, ki, kny, skip, l, order, conv_z):
    b, n_tok = zin.shape[0], zin.shape[1]
    cmat, smat = _dft_tables(n_tok)
    if n_tok >= 2048:
        ct, nb = 256, 1
    else:
        ct, nb = 512, min(8, b)
    nc = HY_W // ct
    tk = min(256, n_tok)
    const = lambda shape: pl.BlockSpec(shape, lambda c, i: (0,) * len(shape),
                                       pipeline_mode=pl.Buffered(1))
    zspec = pl.BlockSpec((nb, n_tok, ct), lambda c, i: (i, 0, zblk * nc + c))
    gspec = pl.BlockSpec((nb, n_tok, ct), lambda c, i: (i, 0, gblk * nc + c))
    kspec = pl.BlockSpec((1, 1, n_tok, ct), lambda c, i: (l, order, 0, c))
    vspec = pl.BlockSpec((1, 1, 1, ct), lambda c, i: (l, order, 0, c))
    wspec = lambda blk: pl.BlockSpec((1, 3, ct), lambda c, i: (l, 0, blk * nc + c))
    bspec = lambda blk: pl.BlockSpec((1, 1, ct), lambda c, i: (l, 0, blk * nc + c))
    return pl.pallas_call(
        functools.partial(_hyena_kernel, conv_z=conv_z, tk=tk),
        grid=(nc, b // nb),
        in_specs=[zspec, gspec, wspec(0), bspec(0), wspec(order + 1), bspec(order + 1),
                  kspec, kspec, vspec, vspec, const((n_tok, n_tok)), const((n_tok, n_tok))],
        out_specs=pl.BlockSpec((nb, n_tok, ct), lambda c, i: (i, 0, c)),
        out_shape=jax.ShapeDtypeStruct((b, n_tok, HY_W), F32 if conv_z else BF16),
        scratch_shapes=[pltpu.VMEM((nb, n_tok, ct), BF16), pltpu.VMEM((nb, n_tok, ct), F32)],
        compiler_params=_cparams(("arbitrary", "arbitrary")),
        name=f"hyena_order{order}",
    )(zin, gin, conv_w, conv_b, conv_w, conv_b, kr, ki, kny, skip, cmat, smat)


def _rms(x, g, n):
    ms = jnp.sum(x * x, axis=-1, keepdims=True) * (1.0 / n)
    return x * lax.rsqrt(ms + RMS_EPS) * g


def _da_values_ext(v):
    lane = lax.broadcasted_iota(jnp.int32, (v.shape[0], LANES), 1)
    ones_col = jnp.where(lane == 0, 1.0, 0.0).astype(BF16)
    pieces = []
    for h in range(DA_HEADS):
        pieces += [v[:, h * DA_V_DIM:(h + 1) * DA_V_DIM], ones_col]
    return jnp.concatenate(pieces, axis=1)


def _mla_values_ext(ckv_b, wv):
    v = _dot(ckv_b, wv)
    lane = lax.broadcasted_iota(jnp.int32, v.shape, 1)
    return jnp.where((lane & (MLA_V_PAD - 1)) == MLA_V, 1.0, v).astype(BF16)


def _prep_kernel(*refs, rope, n_state_in):
    it = iter(refs)
    q_ref, k_ref, v_ref, sm_ref, qn_ref, wuq_ref, kvn_ref, wk_ref, wv_ref, e_ref = (
        next(it) for _ in range(10))
    tab_ref, wuqr_ref, prot_ref = (next(it) for _ in range(3)) if rope else (None,) * 3
    for _ in range(n_state_in):
        next(it)
    qd_ref, kd_ref, vd_ref, qm_ref, km_ref, vm_ref = (next(it) for _ in range(6))
    q = q_ref[...] * (DA_HEAD_DIM ** -0.5)
    k = k_ref[...]
    v = v_ref[...]
    dq = sm_ref[:, 0:MLA_Q_LORA]
    dkv = sm_ref[:, MLA_Q_LORA:MLA_Q_LORA + MLA_KV_LORA]
    kr = sm_ref[:, 5 * LANES:6 * LANES]
    c_kv = _rms(dkv, kvn_ref[0], MLA_KV_LORA)
    if not rope:
        ks_ref, vs_ref, ckvs_ref, krs_ref = (next(it) for _ in range(4))
        nbt, _, n_tok, _ = ks_ref.shape
        ks_ref[:, 0] = k.reshape(nbt, n_tok, k.shape[1])
        vs_ref[:, 0] = v.reshape(nbt, n_tok, v.shape[1])
        ckvs_ref[:, 0] = c_kv.reshape(nbt, n_tok, MLA_KV_LORA)
        krs_ref[:, 0] = kr[:, :MLA_ROPE].reshape(nbt, n_tok, MLA_ROPE)
    if rope:
        def rot(x):
            hi, lo = _split_bf16(x)
            w = prot_ref.shape[0]
            return jnp.concatenate(
                [_dot(hi[:, c:c + w], prot_ref[...]) + _dot(lo[:, c:c + w], prot_ref[...])
                 for c in range(0, x.shape[1], w)], axis=1)
        q = _rope_mix(q, rot(q), tab_ref, 0)
        k = _rope_mix(k, rot(k), tab_ref, 0)
    qd_ref[...] = q.astype(BF16)
    kd_ref[...] = k.astype(BF16)

    mla_scale = (MLA_NOPE + MLA_ROPE) ** -0.5
    c_q = _rms(dq, qn_ref[0], MLA_Q_LORA).astype(BF16)
    qm = _dot(c_q, wuq_ref[0]) * mla_scale
    ckv_b = c_kv.astype(BF16)
    if rope:
        qm = _rope_mix(qm, _dot(c_q, wuqr_ref[0]) * mla_scale, tab_ref, 1)
        kr = _rope_apply(kr, tab_ref, 2, MLA_ROPE // 4)
        vd_ref[...] = _da_values_ext(v.astype(BF16))
        vm_ref[...] = _mla_values_ext(ckv_b, wv_ref[0])
    else:
        vd_ref[...] = v.astype(BF16)
        vm_ref[...] = _dot(ckv_b, wv_ref[0]).astype(BF16)
    qm_ref[...] = qm.astype(BF16)
    km_ref[...] = (_dot(ckv_b, wk_ref[0]) + _dot(kr.astype(BF16), e_ref[...])).astype(BF16)


def _prep(proj, wts, l, seq_len, rope, states=None):
    t = proj.shape[0]
    tm = min(512, seq_len if rope else t)
    tiles_per_seq = seq_len // tm
    blk = lambda w, idx: pl.BlockSpec((tm, w), lambda i: (i, idx))
    wl = lambda shape: pl.BlockSpec((1,) + shape, lambda i: (l, 0, 0))
    in_specs = [blk(512, COL_Q // 512), blk(512, COL_K // 512), blk(512, COL_V // 512),
                blk(SMALL_W, COL_SMALL // SMALL_W),
                wl((1, MLA_Q_LORA)), wl((MLA_Q_LORA, MLA_HEADS * MLA_HEAD_PAD)),
                wl((1, MLA_KV_LORA)), wl((MLA_KV_LORA, MLA_HEADS * MLA_HEAD_PAD)),
                wl((MLA_KV_LORA, MLA_HEADS * (MLA_V_PAD if rope else MLA_V))),
                pl.BlockSpec((LANES, MLA_HEADS * MLA_HEAD_PAD), lambda i: (0, 0))]
    args = [proj, proj, proj, proj, wts["q_norm"], wts["w_uq"], wts["kv_norm"], wts["w_uk"],
            wts["w_uv_pad" if rope else "w_uv"], wts["e_place"]]
    if rope:
        prot = _da_rot_matrix()
        in_specs += [pl.BlockSpec((3, 3, tm, LANES), lambda i: (0, 0, i % tiles_per_seq, 0)),
                     wl((MLA_Q_LORA, MLA_HEADS * MLA_HEAD_PAD)),
                     pl.BlockSpec(prot.shape, lambda i: (0, 0))]
        args += [_rope_tables(seq_len), wts["w_uq_rot"], prot]
    out = lambda w: pl.BlockSpec((tm, w), lambda i: (i, 0))
    widths = ((512, 512, DA_HEADS * 2 * DA_V_DIM, 1024, 1024, MLA_HEADS * MLA_V_PAD) if rope
              else (512, 512, DA_HEADS * DA_V_DIM, 1024, 1024, MLA_HEADS * MLA_V))
    out_specs = [out(w) for w in widths]
    out_shape = [jax.ShapeDtypeStruct((t, w), BF16) for w in widths]
    aliases = {}
    n_state_in = 0
    if not rope:
        nbt = tm // seq_len
        n_state_in = len(states)
        for k, s in enumerate(states):
            out_specs.append(pl.BlockSpec((nbt, 1, seq_len, s.shape[-1]), lambda i: (i, l, 0, 0)))
            out_shape.append(jax.ShapeDtypeStruct(s.shape, s.dtype))
            aliases[len(args)] = 6 + k
            in_specs.append(pl.BlockSpec(memory_space=pl.ANY))
            args.append(s)
    return pl.pallas_call(
        functools.partial(_prep_kernel, rope=rope, n_state_in=n_state_in),
        grid=(t // tm,),
        in_specs=in_specs,
        out_specs=out_specs,
        out_shape=out_shape,
        input_output_aliases=aliases,
        compiler_params=_cparams(("arbitrary",)),
        name="attn_prep",
    )(*args)


def _prep_cache_kernel(k_ref, v_ref, ckv_ref, kr_ref, wk_ref, wv_ref, e_ref,
                       kd_ref, vd_ref, km_ref, vm_ref):
    kd_ref[0] = k_ref[0, 0].astype(BF16)
    vd_ref[0] = _da_values_ext(v_ref[0, 0].astype(BF16))
    ckv_b = ckv_ref[0, 0].astype(BF16)
    km_ref[0] = (_dot(ckv_b, wk_ref[0]) + _dot(kr_ref[0, 0].astype(BF16), e_ref[...])).astype(BF16)
    vm_ref[0] = _mla_values_ext(ckv_b, wv_ref[0])


def _prep_cache(cache_k, cache_v, cache_ckv, cache_kr, wts, l):
    b, _, p = cache_k.shape[:3]
    ck = cache_k.reshape(b, -1, p, 512)
    cv = cache_v.reshape(b, -1, p, 512)
    cin = lambda w: pl.BlockSpec((1, 1, p, w), lambda i: (i, l, 0, 0))
    wl = lambda shape: pl.BlockSpec((1,) + shape, lambda i: (l, 0, 0))
    out = lambda w: pl.BlockSpec((1, p, w), lambda i: (i, 0, 0))
    return pl.pallas_call(
        _prep_cache_kernel,
        grid=(b,),
        in_specs=[cin(512), cin(512), cin(MLA_KV_LORA), cin(MLA_ROPE),
                  wl((MLA_KV_LORA, MLA_HEADS * MLA_HEAD_PAD)), wl((MLA_KV_LORA, MLA_HEADS * MLA_V_PAD)),
                  pl.BlockSpec((MLA_ROPE, MLA_HEADS * MLA_HEAD_PAD), lambda i: (0, 0))],
        out_specs=[out(512), out(1024), out(1024), out(MLA_HEADS * MLA_V_PAD)],
        out_shape=[jax.ShapeDtypeStruct((b, p, 512), BF16), jax.ShapeDtypeStruct((b, p, 1024), BF16),
                   jax.ShapeDtypeStruct((b, p, 1024), BF16),
                   jax.ShapeDtypeStruct((b, p, MLA_HEADS * MLA_V_PAD), BF16)],
        compiler_params=_cparams(("arbitrary",)),
        name="attn_prep_cache",
    )(ck, cv, cache_ckv, cache_kr, wts["w_uk"], wts["w_uv_pad"], wts["e_place"][:MLA_ROPE])


def _softmax_pv(q, k_parts, v_parts, dv, den_on_mxu):
    s = [_dot_nt(q, k) for k in k_parts]
    m = s[0].max(axis=-1, keepdims=True)
    for sp in s[1:]:
        m = jnp.maximum(m, sp.max(axis=-1, keepdims=True))
    o = None
    den = None
    for sp, v in zip(s, v_parts):
        if den_on_mxu:
            pv = _dot(jnp.exp((sp - m).astype(BF16)), v)
        else:
            p = jnp.exp(sp - m)
            psum = p.sum(axis=-1, keepdims=True)
            den = psum if den is None else den + psum
            pv = _dot(p.astype(BF16), v)
        o = pv if o is None else o + pv
    if den_on_mxu:
        return o[:, :dv] / o[:, dv:dv + 1]
    return o / den


def _da_kernel(*refs, lam_init, has_ctx):
    if has_ctx:
        q_ref, kn_ref, vn_ref, kc_ref, vc_ref, lam_ref, g_ref, o_ref = refs
    else:
        q_ref, kn_ref, vn_ref, lam_ref, g_ref, o_ref = refs
    lp = lam_ref[0]
    lam = (jnp.exp(jnp.sum(lp[0:1] * lp[1:2], axis=1, keepdims=True))
           - jnp.exp(jnp.sum(lp[2:3] * lp[3:4], axis=1, keepdims=True)) + lam_init)
    for bi in range(q_ref.shape[0]):
        for h in range(DA_HEADS):
            vw = 2 * DA_V_DIM if has_ctx else DA_V_DIM
            vcols = slice(h * vw, (h + 1) * vw)
            outs = []
            for j in range(2):
                cols = slice((j * DA_HEADS + h) * DA_HEAD_DIM, (j * DA_HEADS + h + 1) * DA_HEAD_DIM)
                k_parts = [kn_ref[bi, :, cols]]
                v_parts = [vn_ref[bi, :, vcols]]
                if has_ctx:
                    k_parts.insert(0, kc_ref[bi, :, cols])
                    v_parts.insert(0, vc_ref[bi, :, vcols])
                outs.append(_softmax_pv(q_ref[bi, :, cols], k_parts, v_parts, DA_V_DIM, has_ctx))
            o = outs[0] - lam * outs[1]
            o_ref[bi, :, h * DA_V_DIM:(h + 1) * DA_V_DIM] = (
                _rms(o, g_ref[0], DA_V_DIM) * (1.0 - lam_init)).astype(o_ref.dtype)


def _da_attention(qd, kd, vd, ctx, da_lambda, da_subln, l, tq, nbq):
    b, n, _ = qd.shape
    has_ctx = ctx is not None
    lam_init = 0.8 - 0.6 * math.exp(-0.3 * l)
    full = lambda a: pl.BlockSpec((nbq,) + a.shape[1:], lambda i, j: (i, 0, 0))
    in_specs = [pl.BlockSpec((nbq, tq, 512), lambda i, j: (i, j, 0)), full(kd), full(vd)]
    args = [qd, kd, vd]
    if has_ctx:
        in_specs += [full(ctx[0]), full(ctx[1])]
        args += [ctx[0], ctx[1]]
    in_specs += [pl.BlockSpec((1, 4, DA_HEAD_DIM), lambda i, j: (l, 0, 0)),
                 pl.BlockSpec((1, 1, DA_V_DIM), lambda i, j: (l, 0, 0))]
    args += [da_lambda, da_subln.reshape(-1, 1, DA_V_DIM)]
    return pl.pallas_call(
        functools.partial(_da_kernel, lam_init=lam_init, has_ctx=has_ctx),
        grid=(b // nbq, n // tq),
        in_specs=in_specs,
        out_specs=pl.BlockSpec((nbq, tq, 512), lambda i, j: (i, j, 0)),
        out_shape=jax.ShapeDtypeStruct((b, n, 512), BF16),
        compiler_params=_cparams(("arbitrary", "arbitrary")),
        name="diff_attention",
    )(*args)


def _mla_kernel(*refs, has_ctx):
    if has_ctx:
        q_ref, kn_ref, vn_ref, kc_ref, vc_ref, o_ref = refs
    else:
        q_ref, kn_ref, vn_ref, o_ref = refs
    for bi in range(q_ref.shape[0]):
        for h in range(MLA_HEADS):
            cols = slice(h * MLA_HEAD_PAD, (h + 1) * MLA_HEAD_PAD)
            vw = MLA_V_PAD if has_ctx else MLA_V
            vcols = slice(h * vw, (h + 1) * vw)
            k_parts = [kn_ref[bi, :, cols]]
            v_parts = [vn_ref[bi, :, vcols]]
            if has_ctx:
                k_parts.insert(0, kc_ref[bi, :, cols])
                v_parts.insert(0, vc_ref[bi, :, vcols])
            o_ref[bi, :, h * MLA_V:(h + 1) * MLA_V] = _softmax_pv(
                q_ref[bi, :, cols], k_parts, v_parts, MLA_V, has_ctx).astype(o_ref.dtype)


def _mla_attention(qm, km, vm, ctx, tq, nbq):
    b, n, _ = qm.shape
    has_ctx = ctx is not None
    full = lambda a: pl.BlockSpec((nbq,) + a.shape[1:], lambda i, j: (i, 0, 0))
    in_specs = [pl.BlockSpec((nbq, tq, 1024), lambda i, j: (i, j, 0)), full(km), full(vm)]
    args = [qm, km, vm]
    if has_ctx:
        in_specs += [full(ctx[0]), full(ctx[1])]
        args += [ctx[0], ctx[1]]
    return pl.pallas_call(
        functools.partial(_mla_kernel, has_ctx=has_ctx),
        grid=(b // nbq, n // tq),
        in_specs=in_specs,
        out_specs=pl.BlockSpec((nbq, tq, 512), lambda i, j: (i, j, 0)),
        out_shape=jax.ShapeDtypeStruct((b, n, 512), BF16),
        compiler_params=_cparams(("arbitrary", "arbitrary")),
        name="latent_attention",
    )(*args)


def _outproj_kernel(x_ref, mod_ref, yh_ref, yd_ref, ym_ref, p0_ref, p1_ref, p2_ref,
                    m0_ref, m1_ref, m2_ref, wb_ref, wo_ref, g_ref, b_ref, o_ref, *, alpha):
    merged = None
    for y_ref, p_ref, m_ref, n in ((yh_ref, p0_ref, m0_ref, 0), (yd_ref, p1_ref, m1_ref, 1),
                                   (ym_ref, p2_ref, m2_ref, 2)):
        br = (y_ref[...].astype(F32) * p_ref[...].astype(F32)).astype(BF16)
        term = m_ref[...].astype(F32) * _dot(br, wb_ref[0, n])
        merged = term if merged is None else merged + term
    out = _dot(merged.astype(BF16), wo_ref[0])
    gate = mod_ref[0, 2:3, :]
    y = alpha * x_ref[...] + gate * out
    mu = jnp.mean(y, axis=-1, keepdims=True)
    yc = y - mu
    var = jnp.mean(yc * yc, axis=-1, keepdims=True)
    o_ref[...] = yc * lax.rsqrt(var + LN_EPS) * g_ref[0] + b_ref[0]


def _outproj(x2d, mod, y_hy, y_da, y_mla, gates, w_branch, w_out, ln_g, ln_b, l, seq_len, alpha):
    t = x2d.shape[0]
    per_batch = mod.shape[0] > 1
    tm = min(512, seq_len if per_batch else t)
    tiles_per_batch = seq_len // tm
    mod_idx = (lambda i: (i // tiles_per_batch, 0, 0)) if per_batch else (lambda i: (0, 0, 0))
    row = lambda w, idx=0: pl.BlockSpec((tm, w), lambda i: (i, idx))
    depth = ln_g.shape[0]
    return pl.pallas_call(
        functools.partial(_outproj_kernel, alpha=alpha),
        grid=(t // tm,),
        in_specs=[row(D_MODEL), pl.BlockSpec((1, 3, D_MODEL), mod_idx),
                  row(BR_W), row(BR_W), row(BR_W),
                  row(BR_W, GATE_PATHS // BR_W), row(BR_W, GATE_PATHS // BR_W + 1),
                  row(BR_W, GATE_PATHS // BR_W + 2),
                  row(D_MODEL, GATE_MERGE // D_MODEL), row(D_MODEL, GATE_MERGE // D_MODEL + 1),
                  row(D_MODEL, GATE_MERGE // D_MODEL + 2),
                  pl.BlockSpec((1, N_BRANCH, BR_W, D_MODEL), lambda i: (l, 0, 0, 0)),
                  pl.BlockSpec((1, D_MODEL, D_MODEL), lambda i: (l, 0, 0)),
                  pl.BlockSpec((1, 1, D_MODEL), lambda i: (l, 0, 0)),
                  pl.BlockSpec((1, 1, D_MODEL), lambda i: (l, 0, 0))],
        out_specs=row(D_MODEL),
        out_shape=jax.ShapeDtypeStruct((t, D_MODEL), F32),
        compiler_params=_cparams(("arbitrary",)),
        name="outproj",
    )(x2d, mod, y_hy, y_da, y_mla, gates, gates, gates, gates, gates, gates, w_branch, w_out,
      ln_g.reshape(depth, 1, D_MODEL), ln_b.reshape(depth, 1, D_MODEL))


def _layer(x, mod, l, wts, spectra, ctx=None, states=None):
    b, n, _ = x.shape
    depth = wts["ln_g"].shape[0]
    alpha = (2 * depth) ** 0.25
    x2d = x.reshape(b * n, D_MODEL)
    proj = _inproj(x2d, mod, wts["w_main"], l, n, gates=False)
    gates = _inproj(x2d, mod, wts["w_gate"], l, n, gates=True)
    proj3 = proj.reshape(b, n, D_IN_MAIN)

    kr_s, ki_s, kny_s = spectra
    hy_blk = COL_HY // HY_W
    conv = (wts["conv_w"], wts["conv_b"])
    z2 = _hyena_order(proj3, hy_blk, proj3, hy_blk + 1, *conv, kr_s, ki_s, kny_s, wts["skip"], l, 0, True)
    y_hy = _hyena_order(z2, 0, proj3, hy_blk + 2, *conv, kr_s, ki_s, kny_s, wts["skip"], l, 1, False)

    prep_out = _prep(proj, wts, l, n, rope=ctx is not None, states=states)
    r3 = lambda a: a.reshape(b, n, a.shape[-1])
    qd, kd, vd, qm, km, vm = map(r3, prep_out[:6])
    new_states = tuple(prep_out[6:]) if ctx is None else None
    if ctx is not None:
        kd_c, vd_c, km_c, vm_c = _prep_cache(*ctx, wts, l)
        da_ctx, mla_ctx = (kd_c, vd_c), (km_c, vm_c)
    else:
        da_ctx = mla_ctx = None
    tq = min(ATTN_TQ, n)
    nbq = 2 if (ctx is None and b % 2 == 0) else 1
    y_da = _da_attention(qd, kd, vd, da_ctx, wts["da_lambda"], wts["da_subln"], l, tq, nbq)
    y_mla = _mla_attention(qm, km, vm, mla_ctx, tq, 1)

    x_new = _outproj(x2d, mod, y_hy.reshape(b * n, HY_W), y_da.reshape(b * n, BR_W),
                     y_mla.reshape(b * n, BR_W), gates, wts["w_branch"], wts["w_out"],
                     wts["ln_g"], wts["ln_b"], l, n, alpha)
    return x_new.reshape(b, n, D_MODEL), new_states


def _prepare_weights(w_in, hy_conv_w, hy_conv_b, hy_ffn_w1, hy_skip, da_lambda, da_subln,
                     mla_q_norm, mla_w_uq, mla_kv_norm, mla_w_ukv, w_branch, w_out, ln_g, ln_b):
    depth = w_in.shape[0]
    w_main = jnp.pad(w_in[..., :ORIG_PATHS].astype(BF16),
                     ((0, 0), (0, 0), (0, SMALL_W - SMALL_USED)))
    w_gate = jnp.concatenate([w_in[..., ORIG_MERGE:ORIG_END], w_in[..., ORIG_PATHS:ORIG_MERGE]],
                             axis=-1).astype(BF16)
    uq = mla_w_uq.reshape(depth, MLA_Q_LORA, MLA_HEADS, MLA_NOPE + MLA_ROPE)
    uq = jnp.pad(uq, ((0, 0), (0, 0), (0, 0), (0, MLA_HEAD_PAD - MLA_NOPE - MLA_ROPE)))
    w_uq = uq.reshape(depth, MLA_Q_LORA, MLA_HEADS * MLA_HEAD_PAD).astype(BF16)
    ukv = mla_w_ukv.reshape(depth, MLA_KV_LORA, MLA_HEADS, MLA_NOPE + MLA_V)
    uk = jnp.pad(ukv[..., :MLA_NOPE], ((0, 0), (0, 0), (0, 0), (0, MLA_HEAD_PAD - MLA_NOPE)))
    w_uk = uk.reshape(depth, MLA_KV_LORA, MLA_HEADS * MLA_HEAD_PAD).astype(BF16)
    w_uv = ukv[..., MLA_NOPE:].reshape(depth, MLA_KV_LORA, MLA_HEADS * MLA_V).astype(BF16)
    uv = jnp.pad(ukv[..., MLA_NOPE:], ((0, 0), (0, 0), (0, 0), (0, MLA_V_PAD - MLA_V)))
    w_uv_pad = uv.reshape(depth, MLA_KV_LORA, MLA_HEADS * MLA_V_PAD).astype(BF16)
    src, sign = _rot_half_spec(MLA_HEADS * MLA_HEAD_PAD, MLA_HEAD_PAD, ((MLA_NOPE, MLA_ROPE),))
    w_uq_rot = w_uq[..., src] * jnp.asarray(sign, dtype=BF16)
    e = np.zeros((LANES, MLA_HEADS * MLA_HEAD_PAD), np.float32)
    for h in range(MLA_HEADS):
        for i in range(MLA_ROPE):
            e[i, h * MLA_HEAD_PAD + MLA_NOPE + i] = 1.0
    w1p = jnp.pad(hy_ffn_w1, ((0, 0), (0, LANES - hy_ffn_w1.shape[1]), (0, 0)))
    return {
        "w_main": w_main, "w_gate": w_gate, "conv_w": hy_conv_w, "conv_b": hy_conv_b.reshape(depth, 1, -1),
        "skip": hy_skip.reshape(depth, HY_ORDER, 1, HY_W), "w1p": w1p,
        "da_lambda": da_lambda, "da_subln": da_subln,
        "q_norm": mla_q_norm.reshape(depth, 1, -1), "kv_norm": mla_kv_norm.reshape(depth, 1, -1),
        "w_uq": w_uq, "w_uq_rot": w_uq_rot, "w_uk": w_uk, "w_uv": w_uv, "w_uv_pad": w_uv_pad, "e_place": jnp.asarray(e, dtype=BF16),
        "w_branch": w_branch.astype(BF16), "w_out": w_out.astype(BF16), "ln_g": ln_g, "ln_b": ln_b,
    }


def kernel(x_prompt, x_sample, c, cache_diff_k, cache_diff_v, cache_mla_ckv, cache_mla_krope, c_ctx, w_mod, b_mod, w_in, hy_conv_w, hy_conv_b, hy_ffn_w1, hy_ffn_b1, hy_ffn_w2, hy_ffn_b2, hy_ffn_w3, hy_sin_freq, hy_log_decay, hy_skip, da_lambda, da_subln, mla_q_norm, mla_w_uq, mla_kv_norm, mla_w_ukv, w_branch, w_out, ln_g, ln_b):
    depth = w_in.shape[0]
    b_ctx, n_ctx, _ = x_prompt.shape
    b_lat, n_lat, _ = x_sample.shape
    wts = _prepare_weights(w_in, hy_conv_w, hy_conv_b, hy_ffn_w1, hy_skip, da_lambda, da_subln,
                           mla_q_norm, mla_w_uq, mla_kv_norm, mla_w_ukv, w_branch, w_out, ln_g, ln_b)

    rows = -(-(1 + b_lat) // 8) * 8
    cvec = jnp.zeros((rows, D_MODEL), F32).at[0].set(c_ctx).at[1:1 + b_lat].set(c)
    mods = _modulation(cvec, w_mod, b_mod).reshape(depth, rows, 3, D_MODEL)

    spec_args = (wts["w1p"], hy_ffn_b1, hy_ffn_w2, hy_ffn_b2, hy_ffn_w3, hy_sin_freq, hy_log_decay)
    spectra_ctx = _hyena_spectra(n_ctx, *spec_args)
    spectra_lat = spectra_ctx if n_lat == n_ctx else _hyena_spectra(n_lat, *spec_args)

    xp = x_prompt
    states = tuple(jnp.zeros((b_ctx, depth, n_ctx, w), F32)
                   for w in (2 * DA_HEADS * DA_HEAD_DIM, DA_HEADS * DA_V_DIM, MLA_KV_LORA, MLA_ROPE))
    for l in range(depth):
        xp, states = _layer(xp, mods[l, 0:1], l, wts, spectra_ctx, states=states)
    state_k = states[0].reshape(b_ctx, depth, n_ctx, 2, DA_HEADS, DA_HEAD_DIM)
    state_v = states[1].reshape(b_ctx, depth, n_ctx, DA_HEADS, DA_V_DIM)
    state_ckv, state_kr = states[2], states[3]

    xs = x_sample
    ctx = (cache_diff_k, cache_diff_v, cache_mla_ckv, cache_mla_krope)
    for l in range(depth):
        xs, _ = _layer(xs, mods[l, 1:1 + b_lat], l, wts, spectra_lat, ctx=ctx)

    return (xp, xs, state_k, state_v, state_ckv, state_kr)
```

```python
import functools
import math

import numpy as np
import jax
import jax.numpy as jnp
from jax import lax
from jax.experimental import pallas as pl
from jax.experimental.pallas import tpu as pltpu

F32 = jnp.float32
BF16 = jnp.bfloat16

D_MODEL = 1024
N_BRANCH = 3
BR_W = 512
HY_W = 512
HY_ORDER = 2
HY_BANDS = 16
HY_FFN = 64
DA_HEADS = 4
DA_HEAD_DIM = 64
DA_V_DIM = 2 * DA_HEAD_DIM
MLA_HEADS = 8
MLA_NOPE = 64
MLA_ROPE = 32
MLA_V = 64
MLA_Q_LORA = 384
MLA_KV_LORA = 256
GRID_W = 64
ROPE_BASE = 10000.0
LN_EPS = 1e-5
RMS_EPS = 1e-6

LANES = 128
MLA_HEAD_PAD = LANES
MLA_V_PAD = 2 * LANES
VMEM_LIMIT = 56 * 1024 * 1024
ATTN_TQ = 512

D_IN_HY = 3 * HY_W
SMALL_W = 1024
SMALL_USED = MLA_Q_LORA + MLA_KV_LORA + MLA_ROPE
COL_HY = 0
COL_Q = D_IN_HY
COL_K = COL_Q + 2 * DA_HEADS * DA_HEAD_DIM
COL_V = COL_K + 2 * DA_HEADS * DA_HEAD_DIM
COL_SMALL = COL_V + DA_HEADS * DA_V_DIM
D_IN_MAIN = COL_SMALL + SMALL_W
GATE_MERGE = 0
GATE_PATHS = N_BRANCH * D_MODEL
ORIG_PATHS = COL_SMALL + SMALL_USED
ORIG_MERGE = ORIG_PATHS + N_BRANCH * BR_W
ORIG_END = ORIG_MERGE + N_BRANCH * D_MODEL


def _cparams(sem):
    return pltpu.CompilerParams(dimension_semantics=sem, vmem_limit_bytes=VMEM_LIMIT)


def _split_bf16(x):
    hi = x.astype(BF16)
    lo = (x - hi.astype(F32)).astype(BF16)
    return hi, lo


def _dot(a, b):
    return jnp.dot(a, b, preferred_element_type=F32)


def _dot_nt(a, b):
    return lax.dot_general(a, b, (((1,), (1,)), ((), ())), preferred_element_type=F32)


def _dot3(a, b):
    ah, al = _split_bf16(a)
    bh, bl = _split_bf16(b)
    return _dot(ah, bh) + _dot(ah, bl) + _dot(al, bh)


def _sigmoid(x):
    return 0.5 * jnp.tanh(0.5 * x) + 0.5


@functools.lru_cache(maxsize=None)
def _dft_tables(n_tok):
    k = np.arange(n_tok, dtype=np.int64)
    ang = np.pi * ((np.outer(k, k) % (2 * n_tok)).astype(np.float64)) / n_tok
    return (jnp.asarray(np.cos(ang), dtype=BF16), jnp.asarray(np.sin(ang), dtype=BF16))


@functools.lru_cache(maxsize=None)
def _hyena_feats(n_tok):
    t = np.arange(n_tok, dtype=np.float64)
    t_lin = t / (n_tok - 1)
    bands = np.arange(1, HY_BANDS + 1, dtype=np.float64)
    ang = (2.0 * np.pi / n_tok) * t[:, None] * bands
    feats = np.concatenate([t_lin[:, None], np.cos(ang), -np.sin(ang)], axis=-1)
    out = np.zeros((n_tok, LANES), np.float64)
    out[:, :feats.shape[1]] = feats
    return jnp.asarray(out, dtype=F32)


def _rope_pattern(pos_row, pos_col, dd):
    q = dd // 4
    freqs = 1.0 / (ROPE_BASE ** (np.arange(q, dtype=np.float64) / q))
    n = pos_row.shape[0]
    cos = np.zeros((n, dd)); sa = np.zeros((n, dd)); sb = np.zeros((n, dd))
    for g, pos in enumerate((pos_row, pos_col)):
        ang = pos[:, None].astype(np.float64) * freqs
        base = g * 2 * q
        cos[:, base:base + q] = np.cos(ang); cos[:, base + q:base + 2 * q] = np.cos(ang)
        sa[:, base:base + q] = -np.sin(ang)
        sb[:, base + q:base + 2 * q] = np.sin(ang)
    return cos, sa, sb


@functools.lru_cache(maxsize=None)
def _rope_tables(n_tok):
    t = np.arange(n_tok)
    row, col = t // GRID_W, t % GRID_W
    ones = np.ones((n_tok, LANES)); zeros = np.zeros((n_tok, LANES))
    c, a, b = _rope_pattern(row, col, DA_HEAD_DIM)
    da = [np.tile(c, (1, 2)), np.tile(a, (1, 2)), np.tile(b, (1, 2))]
    c, a, b = _rope_pattern(row, col, MLA_ROPE)
    mq = [ones.copy(), zeros.copy(), zeros.copy()]
    kr = [ones.copy(), zeros.copy(), zeros.copy()]
    for dst, src in zip(mq, (c, a, b)):
        dst[:, MLA_NOPE:MLA_NOPE + MLA_ROPE] = src
    for dst, src in zip(kr, (c, a, b)):
        dst[:, :MLA_ROPE] = src
    tab = np.stack([np.stack(da), np.stack(mq), np.stack(kr)])
    return jnp.asarray(tab, dtype=F32)


def _rope_apply(x, tab_ref, kind, q):
    cos = tab_ref[kind, 0]
    sa = tab_ref[kind, 1]
    sb = tab_ref[kind, 2]
    outs = []
    for c in range(x.shape[1] // LANES):
        xc = x[:, c * LANES:(c + 1) * LANES]
        outs.append(xc * cos + pltpu.roll(xc, LANES - q, 1) * sa + pltpu.roll(xc, q, 1) * sb)
    return outs[0] if len(outs) == 1 else jnp.concatenate(outs, axis=1)


def _rope_mix(x, rot, tab_ref, kind):
    cos = tab_ref[kind, 0]
    sin = tab_ref[kind, 2] - tab_ref[kind, 1]
    outs = []
    for c in range(x.shape[1] // LANES):
        cols = slice(c * LANES, (c + 1) * LANES)
        outs.append(x[:, cols] * cos + rot[:, cols] * sin)
    return jnp.concatenate(outs, axis=1)


def _rot_half_spec(width, period, groups):
    src = np.arange(width)
    sign = np.zeros(width, np.float32)
    for base in range(0, width, period):
        for start, dd in groups:
            q = dd // 4
            for half in range(2):
                lo = base + start + half * 2 * q
                for i in range(q):
                    src[lo + i], sign[lo + i] = lo + i + q, -1.0
                    src[lo + q + i], sign[lo + q + i] = lo + i, 1.0
    return src, sign


@functools.lru_cache(maxsize=None)
def _da_rot_matrix():
    width = 2 * LANES
    src, sign = _rot_half_spec(width, DA_HEAD_DIM, ((0, DA_HEAD_DIM),))
    p = np.zeros((width, width), np.float32)
    p[src, np.arange(width)] = sign
    return jnp.asarray(p, dtype=BF16)


def _mod_kernel(c_ref, w_ref, b_ref, o_ref):
    c = c_ref[...]
    o_ref[0] = _dot3(c * _sigmoid(c), w_ref[0]) + b_ref[0]


def _modulation(cvec, w_mod, b_mod):
    depth = w_mod.shape[0]
    rows = cvec.shape[0]
    return pl.pallas_call(
        _mod_kernel,
        grid=(depth, 3),
        in_specs=[pl.BlockSpec((rows, D_MODEL), lambda l, j: (0, 0)),
                  pl.BlockSpec((1, D_MODEL, D_MODEL), lambda l, j: (l, 0, j)),
                  pl.BlockSpec((1, 1, D_MODEL), lambda l, j: (l, 0, j))],
        out_specs=pl.BlockSpec((1, rows, D_MODEL), lambda l, j: (l, 0, j)),
        out_shape=jax.ShapeDtypeStruct((depth, rows, 3 * D_MODEL), F32),
        compiler_params=_cparams(("arbitrary", "arbitrary")),
        name="modulation",
    )(cvec, w_mod, b_mod.reshape(depth, 1, 3 * D_MODEL))


def _filt_kernel(feats_ref, w1_ref, b1_ref, w2_ref, b2_ref, w3f_ref, w3b_ref, sf_ref,
                 ldf_ref, ldb_ref, c_ref, s_ref, kr_ref, ki_ref, kny_ref, hid_ref):
    n_tok = feats_ref.shape[0]
    inv_n = 1.0 / (2 * n_tok)

    @pl.when((pl.program_id(1) == 0) & (pl.program_id(2) == 0))
    def _():
        sf = sf_ref[0]
        hid = jnp.sin(sf[0:1] * (_dot3(feats_ref[...], w1_ref[0]) + b1_ref[0]))
        hid_ref[...] = jnp.sin(sf[1:2] * (_dot3(hid, w2_ref[0]) + b2_ref[0]))

    hid = hid_ref[...]
    t_lin = feats_ref[:, 0:1]
    hf = _dot3(hid, w3f_ref[0]) * jnp.exp(-t_lin * jnp.exp(ldf_ref[0]))
    hb = _dot3(hid, w3b_ref[0]) * jnp.exp(-t_lin * jnp.exp(ldb_ref[0]))
    row = lax.broadcasted_iota(jnp.int32, hf.shape, 0)
    hb = jnp.where(row == 0, 0.0, hb)
    a = hf + hb
    d = hf - hb
    wk = jnp.where(row == 0, inv_n, 2.0 * inv_n)
    kr_ref[0, 0] = _dot(c_ref[...], a.astype(BF16)) * wk
    ki_ref[0, 0] = -_dot(s_ref[...], d.astype(BF16)) * wk
    sign = jnp.where((row & 1) == 0, 1.0, -1.0)
    kny_ref[0, 0] = jnp.sum(a * sign, axis=0, keepdims=True) * inv_n


def _hyena_spectra(n_tok, w1p, b1, w2, b2, w3, sf, log_decay):
    depth = w3.shape[0]
    ct = 256
    nc = HY_W // ct
    cmat, smat = _dft_tables(n_tok)
    feats = _hyena_feats(n_tok)
    const = lambda shape: pl.BlockSpec(shape, lambda l, n, c: (0,) * len(shape),
                                       pipeline_mode=pl.Buffered(1))
    per_l = lambda shape: pl.BlockSpec((1,) + shape, lambda l, n, c: (l, 0, 0))
    fcol = lambda l, n, c: (l, 0, n * nc + c)
    bcol = lambda l, n, c: (l, 0, HY_ORDER * nc + n * nc + c)
    out_spec = pl.BlockSpec((1, 1, n_tok, ct), lambda l, n, c: (l, n, 0, c))
    return pl.pallas_call(
        _filt_kernel,
        grid=(depth, HY_ORDER, nc),
        in_specs=[const((n_tok, LANES)),
                  per_l((LANES, HY_FFN)), per_l((1, HY_FFN)),
                  per_l((HY_FFN, HY_FFN)), per_l((1, HY_FFN)),
                  pl.BlockSpec((1, HY_FFN, ct), fcol), pl.BlockSpec((1, HY_FFN, ct), bcol),
                  per_l((2, HY_FFN)),
                  pl.BlockSpec((1, 1, ct), fcol), pl.BlockSpec((1, 1, ct), bcol),
                  const((n_tok, n_tok)), const((n_tok, n_tok))],
        out_specs=[out_spec, out_spec,
                   pl.BlockSpec((1, 1, 1, ct), lambda l, n, c: (l, n, 0, c))],
        out_shape=[jax.ShapeDtypeStruct((depth, HY_ORDER, n_tok, HY_W), F32),
                   jax.ShapeDtypeStruct((depth, HY_ORDER, n_tok, HY_W), F32),
                   jax.ShapeDtypeStruct((depth, HY_ORDER, 1, HY_W), F32)],
        scratch_shapes=[pltpu.VMEM((n_tok, HY_FFN), F32)],
        compiler_params=_cparams(("arbitrary",) * 3),
        name="hyena_spectra",
    )(feats, w1p, b1.reshape(depth, 1, HY_FFN), w2, b2.reshape(depth, 1, HY_FFN), w3, w3, sf,
      log_decay.reshape(depth, 1, -1), log_decay.reshape(depth, 1, -1), cmat, smat)


def _modulate(x, mod_ref):
    return (x * (1.0 + mod_ref[0, 1:2, :]) + mod_ref[0, 0:1, :]).astype(BF16)


def _inproj_kernel(x_ref, mod_ref, w_ref, o_ref, *, gates):
    acc = _dot(_modulate(x_ref[...], mod_ref), w_ref[0])
    if gates:
        tn = acc.shape[1]
        col = pl.program_id(0) * tn + lax.broadcasted_iota(jnp.int32, acc.shape, 1)
        sg = _sigmoid(acc)
        o_ref[...] = jnp.where(col >= N_BRANCH * D_MODEL, acc * sg, sg).astype(BF16)
    else:
        o_ref[...] = acc


def _inproj(x2d, mod, w, l, seq_len, gates):
    t = x2d.shape[0]
    n_out = w.shape[-1]
    per_batch = mod.shape[0] > 1
    tm = min(1024, seq_len if per_batch else t)
    tn = n_out // 2
    tiles_per_batch = seq_len // tm
    mod_idx = (lambda j, i: (i // tiles_per_batch, 0, 0)) if per_batch else (lambda j, i: (0, 0, 0))
    return pl.pallas_call(
        functools.partial(_inproj_kernel, gates=gates),
        grid=(n_out // tn, t // tm),
        in_specs=[pl.BlockSpec((tm, D_MODEL), lambda j, i: (i, 0)),
                  pl.BlockSpec((1, 3, D_MODEL), mod_idx),
                  pl.BlockSpec((1, D_MODEL, tn), lambda j, i: (l, 0, j))],
        out_specs=pl.BlockSpec((tm, tn), lambda j, i: (i, j)),
        out_shape=jax.ShapeDtypeStruct((t, n_out), BF16 if gates else F32),
        compiler_params=_cparams(("arbitrary", "arbitrary")),
        name="inproj_gates" if gates else "inproj",
    )(x2d, mod, w)


def _short_conv(x, w_ref, b_ref):
    n_tok = x.shape[0]
    row = lax.broadcasted_iota(jnp.int32, x.shape, 0)
    prev = jnp.where(row == 0, 0.0, pltpu.roll(x, 1, 0))
    nxt = jnp.where(row == n_tok - 1, 0.0, pltpu.roll(x, n_tok - 1, 0))
    return prev * w_ref[0, 0:1, :] + x * w_ref[0, 1:2, :] + nxt * w_ref[0, 2:3, :] + b_ref[0]


def _hyena_kernel(zin_ref, gin_ref, wz_ref, bz_ref, wg_ref, bg_ref, kr_ref, ki_ref, kny_ref, skip_ref,
                  c_ref, s_ref, o_ref, zb_ref, acc_ref, *, conv_z, tk):
    nb, n_tok, ct = zin_ref.shape
    for i in range(nb):
        z = zin_ref[i]
        if conv_z:
            z = _short_conv(z, wz_ref, bz_ref)
        zb_ref[i] = z.astype(BF16)
        row = lax.broadcasted_iota(jnp.int32, z.shape, 0)
        sign = jnp.where((row & 1) == 0, 1.0, -1.0)
        zny = jnp.sum(z * sign, axis=0, keepdims=True)
        acc_ref[i] = sign * (zny * kny_ref[0, 0]) + z * skip_ref[0, 0]
        for j in range(n_tok // tk):
            rows = slice(j * tk, (j + 1) * tk)
            zr = _dot(c_ref[rows, :], zb_ref[i])
            zs = _dot(s_ref[rows, :], zb_ref[i])
            kr = kr_ref[0, 0, rows, :]
            ki = ki_ref[0, 0, rows, :]
            yr = (zr * kr + zs * ki).astype(BF16)
            ym = (zs * kr - zr * ki).astype(BF16)
            acc_ref[i] += _dot(c_ref[:, rows], yr) + _dot(s_ref[:, rows], ym)
        o_ref[i] = (_short_conv(gin_ref[i], wg_ref, bg_ref) * acc_ref[i]).astype(o_ref.dtype)


def _hyena_order(zin, zblk, gin, gblk, conv_w, conv_b, kr, ki, kny, skip, l, order, conv_z):
    b, n_tok = zin.shape[0], zin.shape[1]
    cmat, smat = _dft_tables(n_tok)
    if n_tok >= 2048:
        ct, nb = 256, 1
    else:
        ct, nb = 512, min(8, b)
    nc = HY_W // ct
    tk = min(256, n_tok)
    const = lambda shape: pl.BlockSpec(shape, lambda c, i: (0,) * len(shape),
                                       pipeline_mode=pl.Buffered(1))
    zspec = pl.BlockSpec((nb, n_tok, ct), lambda c, i: (i, 0, zblk * nc + c))
    gspec = pl.BlockSpec((nb, n_tok, ct), lambda c, i: (i, 0, gblk * nc + c))
    kspec = pl.BlockSpec((1, 1, n_tok, ct), lambda c, i: (l, order, 0, c))
    vspec = pl.BlockSpec((1, 1, 1, ct), lambda c, i: (l, order, 0, c))
    wspec = lambda blk: pl.BlockSpec((1, 3, ct), lambda c, i: (l, 0, blk * nc + c))
    bspec = lambda blk: pl.BlockSpec((1, 1, ct), lambda c, i: (l, 0, blk * nc + c))
    return pl.pallas_call(
        functools.partial(_hyena_kernel, conv_z=conv_z, tk=tk),
        grid=(nc, b // nb),
        in_specs=[zspec, gspec, wspec(0), bspec(0), wspec(order + 1), bspec(order + 1),
                  kspec, kspec, vspec, vspec, const((n_tok, n_tok)), const((n_tok, n_tok))],
        out_specs=pl.BlockSpec((nb, n_tok, ct), lambda c, i: (i, 0, c)),
        out_shape=jax.ShapeDtypeStruct((b, n_tok, HY_W), F32 if conv_z else BF16),
        scratch_shapes=[pltpu.VMEM((nb, n_tok, ct), BF16), pltpu.VMEM((nb, n_tok, ct), F32)],
        compiler_params=_cparams(("arbitrary", "arbitrary")),
        name=f"hyena_order{order}",
    )(zin, gin, conv_w, conv_b, conv_w, conv_b, kr, ki, kny, skip, cmat, smat)


def _rms(x, g, n):
    ms = jnp.sum(x * x, axis=-1, keepdims=True) * (1.0 / n)
    return x * lax.rsqrt(ms + RMS_EPS) * g


def _da_values_ext(v):
    lane = lax.broadcasted_iota(jnp.int32, (v.shape[0], LANES), 1)
    ones_col = jnp.where(lane == 0, 1.0, 0.0).astype(BF16)
    pieces = []
    for h in range(DA_HEADS):
        pieces += [v[:, h * DA_V_DIM:(h + 1) * DA_V_DIM], ones_col]
    return jnp.concatenate(pieces, axis=1)


def _mla_values_ext(ckv_b, wv):
    v = _dot(ckv_b, wv)
    lane = lax.broadcasted_iota(jnp.int32, v.shape, 1)
    return jnp.where((lane & (MLA_V_PAD - 1)) == MLA_V, 1.0, v).astype(BF16)


def _prep_kernel(*refs, rope, n_state_in):
    it = iter(refs)
    q_ref, k_ref, v_ref, sm_ref, qn_ref, wuq_ref, kvn_ref, wk_ref, wv_ref, e_ref = (
        next(it) for _ in range(10))
    tab_ref, wuqr_ref, prot_ref = (next(it) for _ in range(3)) if rope else (None,) * 3
    for _ in range(n_state_in):
        next(it)
    qd_ref, kd_ref, vd_ref, qm_ref, km_ref, vm_ref = (next(it) for _ in range(6))
    q = q_ref[...] * (DA_HEAD_DIM ** -0.5)
    k = k_ref[...]
    v = v_ref[...]
    dq = sm_ref[:, 0:MLA_Q_LORA]
    dkv = sm_ref[:, MLA_Q_LORA:MLA_Q_LORA + MLA_KV_LORA]
    kr = sm_ref[:, 5 * LANES:6 * LANES]
    c_kv = _rms(dkv, kvn_ref[0], MLA_KV_LORA)
    if not rope:
        ks_ref, vs_ref, ckvs_ref, krs_ref = (next(it) for _ in range(4))
        nbt, _, n_tok, _ = ks_ref.shape
        ks_ref[:, 0] = k.reshape(nbt, n_tok, k.shape[1])
        vs_ref[:, 0] = v.reshape(nbt, n_tok, v.shape[1])
        ckvs_ref[:, 0] = c_kv.reshape(nbt, n_tok, MLA_KV_LORA)
        krs_ref[:, 0] = kr[:, :MLA_ROPE].reshape(nbt, n_tok, MLA_ROPE)
    if rope:
        def rot(x):
            hi, lo = _split_bf16(x)
            w = prot_ref.shape[0]
            return jnp.concatenate(
                [_dot(hi[:, c:c + w], prot_ref[...]) + _dot(lo[:, c:c + w], prot_ref[...])
                 for c in range(0, x.shape[1], w)], axis=1)
        q = _rope_mix(q, rot(q), tab_ref, 0)
        k = _rope_mix(k, rot(k), tab_ref, 0)
    qd_ref[...] = q.astype(BF16)
    kd_ref[...] = k.astype(BF16)

    mla_scale = (MLA_NOPE + MLA_ROPE) ** -0.5
    c_q = _rms(dq, qn_ref[0], MLA_Q_LORA).astype(BF16)
    qm = _dot(c_q, wuq_ref[0]) * mla_scale
    ckv_b = c_kv.astype(BF16)
    if rope:
        qm = _rope_mix(qm, _dot(c_q, wuqr_ref[0]) * mla_scale, tab_ref, 1)
        kr = _rope_apply(kr, tab_ref, 2, MLA_ROPE // 4)
        vd_ref[...] = _da_values_ext(v.astype(BF16))
        vm_ref[...] = _mla_values_ext(ckv_b, wv_ref[0])
    else:
        vd_ref[...] = v.astype(BF16)
        vm_ref[...] = _dot(ckv_b, wv_ref[0]).astype(BF16)
    qm_ref[...] = qm.astype(BF16)
    km_ref[...] = (_dot(ckv_b, wk_ref[0]) + _dot(kr.astype(BF16), e_ref[...])).astype(BF16)


def _prep(proj, wts, l, seq_len, rope, states=None):
    t = proj.shape[0]
    tm = min(512, seq_len if rope else t)
    tiles_per_seq = seq_len // tm
    blk = lambda w, idx: pl.BlockSpec((tm, w), lambda i: (i, idx))
    wl = lambda shape: pl.BlockSpec((1,) + shape, lambda i: (l, 0, 0))
    in_specs = [blk(512, COL_Q // 512), blk(512, COL_K // 512), blk(512, COL_V // 512),
                blk(SMALL_W, COL_SMALL // SMALL_W),
                wl((1, MLA_Q_LORA)), wl((MLA_Q_LORA, MLA_HEADS * MLA_HEAD_PAD)),
                wl((1, MLA_KV_LORA)), wl((MLA_KV_LORA, MLA_HEADS * MLA_HEAD_PAD)),
                wl((MLA_KV_LORA, MLA_HEADS * (MLA_V_PAD if rope else MLA_V))),
                pl.BlockSpec((LANES, MLA_HEADS * MLA_HEAD_PAD), lambda i: (0, 0))]
    args = [proj, proj, proj, proj, wts["q_norm"], wts["w_uq"], wts["kv_norm"], wts["w_uk"],
            wts["w_uv_pad" if rope else "w_uv"], wts["e_place"]]
    if rope:
        prot = _da_rot_matrix()
        in_specs += [pl.BlockSpec((3, 3, tm, LANES), lambda i: (0, 0, i % tiles_per_seq, 0)),
                     wl((MLA_Q_LORA, MLA_HEADS * MLA_HEAD_PAD)),
                     pl.BlockSpec(prot.shape, lambda i: (0, 0))]
        args += [_rope_tables(seq_len), wts["w_uq_rot"], prot]
    out = lambda w: pl.BlockSpec((tm, w), lambda i: (i, 0))
    widths = ((512, 512, DA_HEADS * 2 * DA_V_DIM, 1024, 1024, MLA_HEADS * MLA_V_PAD) if rope
              else (512, 512, DA_HEADS * DA_V_DIM, 1024, 1024, MLA_HEADS * MLA_V))
    out_specs = [out(w) for w in widths]
    out_shape = [jax.ShapeDtypeStruct((t, w), BF16) for w in widths]
    aliases = {}
    n_state_in = 0
    if not rope:
        nbt = tm // seq_len
        n_state_in = len(states)
        for k, s in enumerate(states):
            out_specs.append(pl.BlockSpec((nbt, 1, seq_len, s.shape[-1]), lambda i: (i, l, 0, 0)))
            out_shape.append(jax.ShapeDtypeStruct(s.shape, s.dtype))
            aliases[len(args)] = 6 + k
            in_specs.append(pl.BlockSpec(memory_space=pl.ANY))
            args.append(s)
    return pl.pallas_call(
        functools.partial(_prep_kernel, rope=rope, n_state_in=n_state_in),
        grid=(t // tm,),
        in_specs=in_specs,
        out_specs=out_specs,
        out_shape=out_shape,
        input_output_aliases=aliases,
        compiler_params=_cparams(("arbitrary",)),
        name="attn_prep",
    )(*args)


def _prep_cache_kernel(k_ref, v_ref, ckv_ref, kr_ref, wk_ref, wv_ref, e_ref,
                       kd_ref, vd_ref, km_ref, vm_ref):
    kd_ref[0] = k_ref[0, 0].astype(BF16)
    vd_ref[0] = _da_values_ext(v_ref[0, 0].astype(BF16))
    ckv_b = ckv_ref[0, 0].astype(BF16)
    km_ref[0] = (_dot(ckv_b, wk_ref[0]) + _dot(kr_ref[0, 0].astype(BF16), e_ref[...])).astype(BF16)
    vm_ref[0] = _mla_values_ext(ckv_b, wv_ref[0])


def _prep_cache(cache_k, cache_v, cache_ckv, cache_kr, wts, l):
    b, _, p = cache_k.shape[:3]
    ck = cache_k.reshape(b, -1, p, 512)
    cv = cache_v.reshape(b, -1, p, 512)
    cin = lambda w: pl.BlockSpec((1, 1, p, w), lambda i: (i, l, 0, 0))
    wl = lambda shape: pl.BlockSpec((1,) + shape, lambda i: (l, 0, 0))
    out = lambda w: pl.BlockSpec((1, p, w), lambda i: (i, 0, 0))
    return pl.pallas_call(
        _prep_cache_kernel,
        grid=(b,),
        in_specs=[cin(512), cin(512), cin(MLA_KV_LORA), cin(MLA_ROPE),
                  wl((MLA_KV_LORA, MLA_HEADS * MLA_HEAD_PAD)), wl((MLA_KV_LORA, MLA_HEADS * MLA_V_PAD)),
                  pl.BlockSpec((MLA_ROPE, MLA_HEADS * MLA_HEAD_PAD), lambda i: (0, 0))],
        out_specs=[out(512), out(1024), out(1024), out(MLA_HEADS * MLA_V_PAD)],
        out_shape=[jax.ShapeDtypeStruct((b, p, 512), BF16), jax.ShapeDtypeStruct((b, p, 1024), BF16),
                   jax.ShapeDtypeStruct((b, p, 1024), BF16),
                   jax.ShapeDtypeStruct((b, p, MLA_HEADS * MLA_V_PAD), BF16)],
        compiler_params=_cparams(("arbitrary",)),
        name="attn_prep_cache",
    )(ck, cv, cache_ckv, cache_kr, wts["w_uk"], wts["w_uv_pad"], wts["e_place"][:MLA_ROPE])


def _softmax_pv(q, k_parts, v_parts, dv, den_on_mxu):
    s = [_dot_nt(q, k) for k in k_parts]
    m = s[0].max(axis=-1, keepdims=True)
    for sp in s[1:]:
        m = jnp.maximum(m, sp.max(axis=-1, keepdims=True))
    o = None
    den = None
    for sp, v in zip(s, v_parts):
        if den_on_mxu:
            pv = _dot(jnp.exp((sp - m).astype(BF16)), v)
        else:
            p = jnp.exp(sp - m)
            psum = p.sum(axis=-1, keepdims=True)
            den = psum if den is None else den + psum
            pv = _dot(p.astype(BF16), v)
        o = pv if o is None else o + pv
    if den_on_mxu:
        return o[:, :dv] / o[:, dv:dv + 1]
    return o / den


def _da_kernel(*refs, lam_init, has_ctx):
    if has_ctx:
        q_ref, kn_ref, vn_ref, kc_ref, vc_ref, lam_ref, g_ref, o_ref = refs
    else:
        q_ref, kn_ref, vn_ref, lam_ref, g_ref, o_ref = refs
    lp = lam_ref[0]
    lam = (jnp.exp(jnp.sum(lp[0:1] * lp[1:2], axis=1, keepdims=True))
           - jnp.exp(jnp.sum(lp[2:3] * lp[3:4], axis=1, keepdims=True)) + lam_init)
    for bi in range(q_ref.shape[0]):
        for h in range(DA_HEADS):
            vw = 2 * DA_V_DIM if has_ctx else DA_V_DIM
            vcols = slice(h * vw, (h + 1) * vw)
            outs = []
            for j in range(2):
                cols = slice((j * DA_HEADS + h) * DA_HEAD_DIM, (j * DA_HEADS + h + 1) * DA_HEAD_DIM)
                k_parts = [kn_ref[bi, :, cols]]
                v_parts = [vn_ref[bi, :, vcols]]
                if has_ctx:
                    k_parts.insert(0, kc_ref[bi, :, cols])
                    v_parts.insert(0, vc_ref[bi, :, vcols])
                outs.append(_softmax_pv(q_ref[bi, :, cols], k_parts, v_parts, DA_V_DIM, has_ctx))
            o = outs[0] - lam * outs[1]
            o_ref[bi, :, h * DA_V_DIM:(h + 1) * DA_V_DIM] = (
                _rms(o, g_ref[0], DA_V_DIM) * (1.0 - lam_init)).astype(o_ref.dtype)


def _da_attention(qd, kd, vd, ctx, da_lambda, da_subln, l, tq, nbq):
    b, n, _ = qd.shape
    has_ctx = ctx is not None
    lam_init = 0.8 - 0.6 * math.exp(-0.3 * l)
    full = lambda a: pl.BlockSpec((nbq,) + a.shape[1:], lambda i, j: (i, 0, 0))
    in_specs = [pl.BlockSpec((nbq, tq, 512), lambda i, j: (i, j, 0)), full(kd), full(vd)]
    args = [qd, kd, vd]
    if has_ctx:
        in_specs += [full(ctx[0]), full(ctx[1])]
        args += [ctx[0], ctx[1]]
    in_specs += [pl.BlockSpec((1, 4, DA_HEAD_DIM), lambda i, j: (l, 0, 0)),
                 pl.BlockSpec((1, 1, DA_V_DIM), lambda i, j: (l, 0, 0))]
    args += [da_lambda, da_subln.reshape(-1, 1, DA_V_DIM)]
    return pl.pallas_call(
        functools.partial(_da_kernel, lam_init=lam_init, has_ctx=has_ctx),
        grid=(b // nbq, n // tq),
        in_specs=in_specs,
        out_specs=pl.BlockSpec((nbq, tq, 512), lambda i, j: (i, j, 0)),
        out_shape=jax.ShapeDtypeStruct((b, n, 512), BF16),
        compiler_params=_cparams(("arbitrary", "arbitrary")),
        name="diff_attention",
    )(*args)


def _mla_kernel(*refs, has_ctx):
    if has_ctx:
        q_ref, kn_ref, vn_ref, kc_ref, vc_ref, o_ref = refs
    else:
        q_ref, kn_ref, vn_ref, o_ref = refs
    for bi in range(q_ref.shape[0]):
        for h in range(MLA_HEADS):
            cols = slice(h * MLA_HEAD_PAD, (h + 1) * MLA_HEAD_PAD)
            vw = MLA_V_PAD if has_ctx else MLA_V
            vcols = slice(h * vw, (h + 1) * vw)
            k_parts = [kn_ref[bi, :, cols]]
            v_parts = [vn_ref[bi, :, vcols]]
            if has_ctx:
                k_parts.insert(0, kc_ref[bi, :, cols])
                v_parts.insert(0, vc_ref[bi, :, vcols])
            o_ref[bi, :, h * MLA_V:(h + 1) * MLA_V] = _softmax_pv(
                q_ref[bi, :, cols], k_parts, v_parts, MLA_V, has_ctx).astype(o_ref.dtype)


def _mla_attention(qm, km, vm, ctx, tq, nbq):
    b, n, _ = qm.shape
    has_ctx = ctx is not None
    full = lambda a: pl.BlockSpec((nbq,) + a.shape[1:], lambda i, j: (i, 0, 0))
    in_specs = [pl.BlockSpec((nbq, tq, 1024), lambda i, j: (i, j, 0)), full(km), full(vm)]
    args = [qm, km, vm]
    if has_ctx:
        in_specs += [full(ctx[0]), full(ctx[1])]
        args += [ctx[0], ctx[1]]
    return pl.pallas_call(
        functools.partial(_mla_kernel, has_ctx=has_ctx),
        grid=(b // nbq, n // tq),
        in_specs=in_specs,
        out_specs=pl.BlockSpec((nbq, tq, 512), lambda i, j: (i, j, 0)),
        out_shape=jax.ShapeDtypeStruct((b, n, 512), BF16),
        compiler_params=_cparams(("arbitrary", "arbitrary")),
        name="latent_attention",
    )(*args)


def _outproj_kernel(x_ref, mod_ref, yh_ref, yd_ref, ym_ref, p0_ref, p1_ref, p2_ref,
                    m0_ref, m1_ref, m2_ref, wb_ref, wo_ref, g_ref, b_ref, o_ref, *, alpha):
    merged = None
    for y_ref, p_ref, m_ref, n in ((yh_ref, p0_ref, m0_ref, 0), (yd_ref, p1_ref, m1_ref, 1),
                                   (ym_ref, p2_ref, m2_ref, 2)):
        br = (y_ref[...].astype(F32) * p_ref[...].astype(F32)).astype(BF16)
        term = m_ref[...].astype(F32) * _dot(br, wb_ref[0, n])
        merged = term if merged is None else merged + term
    out = _dot(merged.astype(BF16), wo_ref[0])
    gate = mod_ref[0, 2:3, :]
    y = alpha * x_ref[...] + gate * out
    mu = jnp.mean(y, axis=-1, keepdims=True)
    yc = y - mu
    var = jnp.mean(yc * yc, axis=-1, keepdims=True)
    o_ref[...] = yc * lax.rsqrt(var + LN_EPS) * g_ref[0] + b_ref[0]


def _outproj(x2d, mod, y_hy, y_da, y_mla, gates, w_branch, w_out, ln_g, ln_b, l, seq_len, alpha):
    t = x2d.shape[0]
    per_batch = mod.shape[0] > 1
    tm = min(512, seq_len if per_batch else t)
    tiles_per_batch = seq_len // tm
    mod_idx = (lambda i: (i // tiles_per_batch, 0, 0)) if per_batch else (lambda i: (0, 0, 0))
    row = lambda w, idx=0: pl.BlockSpec((tm, w), lambda i: (i, idx))
    depth = ln_g.shape[0]
    return pl.pallas_call(
        functools.partial(_outproj_kernel, alpha=alpha),
        grid=(t // tm,),
        in_specs=[row(D_MODEL), pl.BlockSpec((1, 3, D_MODEL), mod_idx),
                  row(BR_W), row(BR_W), row(BR_W),
                  row(BR_W, GATE_PATHS // BR_W), row(BR_W, GATE_PATHS // BR_W + 1),
                  row(BR_W, GATE_PATHS // BR_W + 2),
                  row(D_MODEL, GATE_MERGE // D_MODEL), row(D_MODEL, GATE_MERGE // D_MODEL + 1),
                  row(D_MODEL, GATE_MERGE // D_MODEL + 2),
                  pl.BlockSpec((1, N_BRANCH, BR_W, D_MODEL), lambda i: (l, 0, 0, 0)),
                  pl.BlockSpec((1, D_MODEL, D_MODEL), lambda i: (l, 0, 0)),
                  pl.BlockSpec((1, 1, D_MODEL), lambda i: (l, 0, 0)),
                  pl.BlockSpec((1, 1, D_MODEL), lambda i: (l, 0, 0))],
        out_specs=row(D_MODEL),
        out_shape=jax.ShapeDtypeStruct((t, D_MODEL), F32),
        compiler_params=_cparams(("arbitrary",)),
        name="outproj",
    )(x2d, mod, y_hy, y_da, y_mla, gates, gates, gates, gates, gates, gates, w_branch, w_out,
      ln_g.reshape(depth, 1, D_MODEL), ln_b.reshape(depth, 1, D_MODEL))


def _layer(x, mod, l, wts, spectra, ctx=None, states=None):
    b, n, _ = x.shape
    depth = wts["ln_g"].shape[0]
    alpha = (2 * depth) ** 0.25
    x2d = x.reshape(b * n, D_MODEL)
    proj = _inproj(x2d, mod, wts["w_main"], l, n, gates=False)
    gates = _inproj(x2d, mod, wts["w_gate"], l, n, gates=True)
    proj3 = proj.reshape(b, n, D_IN_MAIN)

    kr_s, ki_s, kny_s = spectra
    hy_blk = COL_HY // HY_W
    conv = (wts["conv_w"], wts["conv_b"])
    z2 = _hyena_order(proj3, hy_blk, proj3, hy_blk + 1, *conv, kr_s, ki_s, kny_s, wts["skip"], l, 0, True)
    y_hy = _hyena_order(z2, 0, proj3, hy_blk + 2, *conv, kr_s, ki_s, kny_s, wts["skip"], l, 1, False)

    prep_out = _prep(proj, wts, l, n, rope=ctx is not None, states=states)
    r3 = lambda a: a.reshape(b, n, a.shape[-1])
    qd, kd, vd, qm, km, vm = map(r3, prep_out[:6])
    new_states = tuple(prep_out[6:]) if ctx is None else None
    if ctx is not None:
        kd_c, vd_c, km_c, vm_c = _prep_cache(*ctx, wts, l)
        da_ctx, mla_ctx = (kd_c, vd_c), (km_c, vm_c)
    else:
        da_ctx = mla_ctx = None
    tq = min(ATTN_TQ, n)
    nbq = 2 if (ctx is None and b % 2 == 0) else 1
    y_da = _da_attention(qd, kd, vd, da_ctx, wts["da_lambda"], wts["da_subln"], l, tq, nbq)
    y_mla = _mla_attention(qm, km, vm, mla_ctx, tq, 1)

    x_new = _outproj(x2d, mod, y_hy.reshape(b * n, HY_W), y_da.reshape(b * n, BR_W),
                     y_mla.reshape(b * n, BR_W), gates, wts["w_branch"], wts["w_out"],
                     wts["ln_g"], wts["ln_b"], l, n, alpha)
    return x_new.reshape(b, n, D_MODEL), new_states


def _prepare_weights(w_in, hy_conv_w, hy_conv_b, hy_ffn_w1, hy_skip, da_lambda, da_subln,
                     mla_q_norm, mla_w_uq, mla_kv_norm, mla_w_ukv, w_branch, w_out, ln_g, ln_b):
    depth = w_in.shape[0]
    w_main = jnp.pad(w_in[..., :ORIG_PATHS].astype(BF16),
                     ((0, 0), (0, 0), (0, SMALL_W - SMALL_USED)))
    w_gate = jnp.concatenate([w_in[..., ORIG_MERGE:ORIG_END], w_in[..., ORIG_PATHS:ORIG_MERGE]],
                             axis=-1).astype(BF16)
    uq = mla_w_uq.reshape(depth, MLA_Q_LORA, MLA_HEADS, MLA_NOPE + MLA_ROPE)
    uq = jnp.pad(uq, ((0, 0), (0, 0), (0, 0), (0, MLA_HEAD_PAD - MLA_NOPE - MLA_ROPE)))
    w_uq = uq.reshape(depth, MLA_Q_LORA, MLA_HEADS * MLA_HEAD_PAD).astype(BF16)
    ukv = mla_w_ukv.reshape(depth, MLA_KV_LORA, MLA_HEADS, MLA_NOPE + MLA_V)
    uk = jnp.pad(ukv[..., :MLA_NOPE], ((0, 0), (0, 0), (0, 0), (0, MLA_HEAD_PAD - MLA_NOPE)))
    w_uk = uk.reshape(depth, MLA_KV_LORA, MLA_HEADS * MLA_HEAD_PAD).astype(BF16)
    w_uv = ukv[..., MLA_NOPE:].reshape(depth, MLA_KV_LORA, MLA_HEADS * MLA_V).astype(BF16)
    uv = jnp.pad(ukv[..., MLA_NOPE:], ((0, 0), (0, 0), (0, 0), (0, MLA_V_PAD - MLA_V)))
    w_uv_pad = uv.reshape(depth, MLA_KV_LORA, MLA_HEADS * MLA_V_PAD).astype(BF16)
    src, sign = _rot_half_spec(MLA_HEADS * MLA_HEAD_PAD, MLA_HEAD_PAD, ((MLA_NOPE, MLA_ROPE),))
    w_uq_rot = w_uq[..., src] * jnp.asarray(sign, dtype=BF16)
    e = np.zeros((LANES, MLA_HEADS * MLA_HEAD_PAD), np.float32)
    for h in range(MLA_HEADS):
        for i in range(MLA_ROPE):
            e[i, h * MLA_HEAD_PAD + MLA_NOPE + i] = 1.0
    w1p = jnp.pad(hy_ffn_w1, ((0, 0), (0, LANES - hy_ffn_w1.shape[1]), (0, 0)))
    return {
        "w_main": w_main, "w_gate": w_gate, "conv_w": hy_conv_w, "conv_b": hy_conv_b.reshape(depth, 1, -1),
        "skip": hy_skip.reshape(depth, HY_ORDER, 1, HY_W), "w1p": w1p,
        "da_lambda": da_lambda, "da_subln": da_subln,
        "q_norm": mla_q_norm.reshape(depth, 1, -1), "kv_norm": mla_kv_norm.reshape(depth, 1, -1),
        "w_uq": w_uq, "w_uq_rot": w_uq_rot, "w_uk": w_uk, "w_uv": w_uv, "w_uv_pad": w_uv_pad, "e_place": jnp.asarray(e, dtype=BF16),
        "w_branch": w_branch.astype(BF16), "w_out": w_out.astype(BF16), "ln_g": ln_g, "ln_b": ln_b,
    }


def kernel(x_prompt, x_sample, c, cache_diff_k, cache_diff_v, cache_mla_ckv, cache_mla_krope, c_ctx, w_mod, b_mod, w_in, hy_conv_w, hy_conv_b, hy_ffn_w1, hy_ffn_b1, hy_ffn_w2, hy_ffn_b2, hy_ffn_w3, hy_sin_freq, hy_log_decay, hy_skip, da_lambda, da_subln, mla_q_norm, mla_w_uq, mla_kv_norm, mla_w_ukv, w_branch, w_out, ln_g, ln_b):
    depth = w_in.shape[0]
    b_ctx, n_ctx, _ = x_prompt.shape
    b_lat, n_lat, _ = x_sample.shape
    wts = _prepare_weights(w_in, hy_conv_w, hy_conv_b, hy_ffn_w1, hy_skip, da_lambda, da_subln,
                           mla_q_norm, mla_w_uq, mla_kv_norm, mla_w_ukv, w_branch, w_out, ln_g, ln_b)

    rows = -(-(1 + b_lat) // 8) * 8
    cvec = jnp.zeros((rows, D_MODEL), F32).at[0].set(c_ctx).at[1:1 + b_lat].set(c)
    mods = _modulation(cvec, w_mod, b_mod).reshape(depth, rows, 3, D_MODEL)

    spec_args = (wts["w1p"], hy_ffn_b1, hy_ffn_w2, hy_ffn_b2, hy_ffn_w3, hy_sin_freq, hy_log_decay)
    spectra_ctx = _hyena_spectra(n_ctx, *spec_args)
    spectra_lat = spectra_ctx if n_lat == n_ctx else _hyena_spectra(n_lat, *spec_args)

    xs = x_sample
    ctx = (cache_diff_k, cache_diff_v, cache_mla_ckv, cache_mla_krope)
    for l in range(depth):
        xs, _ = _layer(xs, mods[l, 1:1 + b_lat], l, wts, spectra_lat, ctx=ctx)

    xp = x_prompt
    states = tuple(jnp.zeros((b_ctx, depth, n_ctx, w), F32)
                   for w in (2 * DA_HEADS * DA_HEAD_DIM, DA_HEADS * DA_V_DIM, MLA_KV_LORA, MLA_ROPE))
    for l in range(depth):
        xp, states = _layer(xp, mods[l, 0:1], l, wts, spectra_ctx, states=states)
    state_k = states[0].reshape(b_ctx, depth, n_ctx, 2, DA_HEADS, DA_HEAD_DIM)
    state_v = states[1].reshape(b_ctx, depth, n_ctx, DA_HEADS, DA_V_DIM)
    state_ckv, state_kr = states[2], states[3]

    return (xp, xs, state_k, state_v, state_ckv, state_kr)
```

```python
import functools
import math

import numpy as np
import jax
import jax.numpy as jnp
from jax import lax
from jax.experimental import pallas as pl
from jax.experimental.pallas import tpu as pltpu

F32 = jnp.float32
BF16 = jnp.bfloat16

D_MODEL = 1024
N_BRANCH = 3
BR_W = 512
HY_W = 512
HY_ORDER = 2
HY_BANDS = 16
HY_FFN = 64
DA_HEADS = 4
DA_HEAD_DIM = 64
DA_V_DIM = 2 * DA_HEAD_DIM
MLA_HEADS = 8
MLA_NOPE = 64
MLA_ROPE = 32
MLA_V = 64
MLA_Q_LORA = 384
MLA_KV_LORA = 256
GRID_W = 64
ROPE_BASE = 10000.0
LN_EPS = 1e-5
RMS_EPS = 1e-6

LANES = 128
MLA_HEAD_PAD = LANES
MLA_V_PAD = 2 * LANES
VMEM_LIMIT = 56 * 1024 * 1024
ATTN_TQ = 512

D_IN_HY = 3 * HY_W
SMALL_W = 1024
SMALL_USED = MLA_Q_LORA + MLA_KV_LORA + MLA_ROPE
COL_HY = 0
COL_Q = D_IN_HY
COL_K = COL_Q + 2 * DA_HEADS * DA_HEAD_DIM
COL_V = COL_K + 2 * DA_HEADS * DA_HEAD_DIM
COL_SMALL = COL_V + DA_HEADS * DA_V_DIM
D_IN_MAIN = COL_SMALL + SMALL_W
GATE_MERGE = 0
GATE_PATHS = N_BRANCH * D_MODEL
ORIG_PATHS = COL_SMALL + SMALL_USED
ORIG_MERGE = ORIG_PATHS + N_BRANCH * BR_W
ORIG_END = ORIG_MERGE + N_BRANCH * D_MODEL


def _cparams(sem):
    return pltpu.CompilerParams(dimension_semantics=sem, vmem_limit_bytes=VMEM_LIMIT)


def _split_bf16(x):
    hi = x.astype(BF16)
    lo = (x - hi.astype(F32)).astype(BF16)
    return hi, lo


def _dot(a, b):
    return jnp.dot(a, b, preferred_element_type=F32)


def _dot_nt(a, b):
    return lax.dot_general(a, b, (((1,), (1,)), ((), ())), preferred_element_type=F32)


def _dot3(a, b):
    ah, al = _split_bf16(a)
    bh, bl = _split_bf16(b)
    return _dot(ah, bh) + _dot(ah, bl) + _dot(al, bh)


def _sigmoid(x):
    return 0.5 * jnp.tanh(0.5 * x) + 0.5


@functools.lru_cache(maxsize=None)
def _dft_tables(n_tok):
    k = np.arange(n_tok, dtype=np.int64)
    ang = np.pi * ((np.outer(k, k) % (2 * n_tok)).astype(np.float64)) / n_tok
    return (jnp.asarray(np.cos(ang), dtype=BF16), jnp.asarray(np.sin(ang), dtype=BF16))


@functools.lru_cache(maxsize=None)
def _hyena_feats(n_tok):
    t = np.arange(n_tok, dtype=np.float64)
    t_lin = t / (n_tok - 1)
    bands = np.arange(1, HY_BANDS + 1, dtype=np.float64)
    ang = (2.0 * np.pi / n_tok) * t[:, None] * bands
    feats = np.concatenate([t_lin[:, None], np.cos(ang), -np.sin(ang)], axis=-1)
    out = np.zeros((n_tok, LANES), np.float64)
    out[:, :feats.shape[1]] = feats
    return jnp.asarray(out, dtype=F32)


def _rope_pattern(pos_row, pos_col, dd):
    q = dd // 4
    freqs = 1.0 / (ROPE_BASE ** (np.arange(q, dtype=np.float64) / q))
    n = pos_row.shape[0]
    cos = np.zeros((n, dd)); sa = np.zeros((n, dd)); sb = np.zeros((n, dd))
    for g, pos in enumerate((pos_row, pos_col)):
        ang = pos[:, None].astype(np.float64) * freqs
        base = g * 2 * q
        cos[:, base:base + q] = np.cos(ang); cos[:, base + q:base + 2 * q] = np.cos(ang)
        sa[:, base:base + q] = -np.sin(ang)
        sb[:, base + q:base + 2 * q] = np.sin(ang)
    return cos, sa, sb


@functools.lru_cache(maxsize=None)
def _rope_tables(n_tok):
    t = np.arange(n_tok)
    row, col = t // GRID_W, t % GRID_W
    ones = np.ones((n_tok, LANES)); zeros = np.zeros((n_tok, LANES))
    c, a, b = _rope_pattern(row, col, DA_HEAD_DIM)
    da = [np.tile(c, (1, 2)), np.tile(a, (1, 2)), np.tile(b, (1, 2))]
    c, a, b = _rope_pattern(row, col, MLA_ROPE)
    mq = [ones.copy(), zeros.copy(), zeros.copy()]
    kr = [ones.copy(), zeros.copy(), zeros.copy()]
    for dst, src in zip(mq, (c, a, b)):
        dst[:, MLA_NOPE:MLA_NOPE + MLA_ROPE] = src
    for dst, src in zip(kr, (c, a, b)):
        dst[:, :MLA_ROPE] = src
    tab = np.stack([np.stack(da), np.stack(mq), np.stack(kr)])
    return jnp.asarray(tab, dtype=F32)


def _rope_apply(x, tab_ref, kind, q):
    cos = tab_ref[kind, 0]
    sa = tab_ref[kind, 1]
    sb = tab_ref[kind, 2]
    outs = []
    for c in range(x.shape[1] // LANES):
        xc = x[:, c * LANES:(c + 1) * LANES]
        outs.append(xc * cos + pltpu.roll(xc, LANES - q, 1) * sa + pltpu.roll(xc, q, 1) * sb)
    return outs[0] if len(outs) == 1 else jnp.concatenate(outs, axis=1)


def _rope_mix(x, rot, tab_ref, kind):
    cos = tab_ref[kind, 0]
    sin = tab_ref[kind, 2] - tab_ref[kind, 1]
    outs = []
    for c in range(x.shape[1] // LANES):
        cols = slice(c * LANES, (c + 1) * LANES)
        outs.append(x[:, cols] * cos + rot[:, cols] * sin)
    return jnp.concatenate(outs, axis=1)


def _rot_half_spec(width, period, groups):
    src = np.arange(width)
    sign = np.zeros(width, np.float32)
    for base in range(0, width, period):
        for start, dd in groups:
            q = dd // 4
            for half in range(2):
                lo = base + start + half * 2 * q
                for i in range(q):
                    src[lo + i], sign[lo + i] = lo + i + q, -1.0
                    src[lo + q + i], sign[lo + q + i] = lo + i, 1.0
    return src, sign


@functools.lru_cache(maxsize=None)
def _da_rot_matrix():
    width = 2 * LANES
    src, sign = _rot_half_spec(width, DA_HEAD_DIM, ((0, DA_HEAD_DIM),))
    p = np.zeros((width, width), np.float32)
    p[src, np.arange(width)] = sign
    return jnp.asarray(p, dtype=BF16)


def _mod_kernel(c_ref, w_ref, b_ref, o_ref):
    c = c_ref[...]
    o_ref[0] = _dot3(c * _sigmoid(c), w_ref[0]) + b_ref[0]


def _modulation(cvec, w_mod, b_mod):
    depth = w_mod.shape[0]
    rows = cvec.shape[0]
    return pl.pallas_call(
        _mod_kernel,
        grid=(depth, 3),
        in_specs=[pl.BlockSpec((rows, D_MODEL), lambda l, j: (0, 0)),
                  pl.BlockSpec((1, D_MODEL, D_MODEL), lambda l, j: (l, 0, j)),
                  pl.BlockSpec((1, 1, D_MODEL), lambda l, j: (l, 0, j))],
        out_specs=pl.BlockSpec((1, rows, D_MODEL), lambda l, j: (l, 0, j)),
        out_shape=jax.ShapeDtypeStruct((depth, rows, 3 * D_MODEL), F32),
        compiler_params=_cparams(("arbitrary", "arbitrary")),
        name="modulation",
    )(cvec, w_mod, b_mod.reshape(depth, 1, 3 * D_MODEL))


def _filt_kernel(feats_ref, w1_ref, b1_ref, w2_ref, b2_ref, w3f_ref, w3b_ref, sf_ref,
                 ldf_ref, ldb_ref, c_ref, s_ref, kr_ref, ki_ref, kny_ref, hid_ref):
    n_tok = feats_ref.shape[0]
    inv_n = 1.0 / (2 * n_tok)

    @pl.when((pl.program_id(1) == 0) & (pl.program_id(2) == 0))
    def _():
        sf = sf_ref[0]
        hid = jnp.sin(sf[0:1] * (_dot3(feats_ref[...], w1_ref[0]) + b1_ref[0]))
        hid_ref[...] = jnp.sin(sf[1:2] * (_dot3(hid, w2_ref[0]) + b2_ref[0]))

    hid = hid_ref[...]
    t_lin = feats_ref[:, 0:1]
    hf = _dot3(hid, w3f_ref[0]) * jnp.exp(-t_lin * jnp.exp(ldf_ref[0]))
    hb = _dot3(hid, w3b_ref[0]) * jnp.exp(-t_lin * jnp.exp(ldb_ref[0]))
    row = lax.broadcasted_iota(jnp.int32, hf.shape, 0)
    hb = jnp.where(row == 0, 0.0, hb)
    a = hf + hb
    d = hf - hb
    wk = jnp.where(row == 0, inv_n, 2.0 * inv_n)
    kr_ref[0, 0] = _dot(c_ref[...], a.astype(BF16)) * wk
    ki_ref[0, 0] = -_dot(s_ref[...], d.astype(BF16)) * wk
    sign = jnp.where((row & 1) == 0, 1.0, -1.0)
    kny_ref[0, 0] = jnp.sum(a * sign, axis=0, keepdims=True) * inv_n


def _hyena_spectra(n_tok, w1p, b1, w2, b2, w3, sf, log_decay):
    depth = w3.shape[0]
    ct = 256
    nc = HY_W // ct
    cmat, smat = _dft_tables(n_tok)
    feats = _hyena_feats(n_tok)
    const = lambda shape: pl.BlockSpec(shape, lambda l, n, c: (0,) * len(shape),
                                       pipeline_mode=pl.Buffered(1))
    per_l = lambda shape: pl.BlockSpec((1,) + shape, lambda l, n, c: (l, 0, 0))
    fcol = lambda l, n, c: (l, 0, n * nc + c)
    bcol = lambda l, n, c: (l, 0, HY_ORDER * nc + n * nc + c)
    out_spec = pl.BlockSpec((1, 1, n_tok, ct), lambda l, n, c: (l, n, 0, c))
    return pl.pallas_call(
        _filt_kernel,
        grid=(depth, HY_ORDER, nc),
        in_specs=[const((n_tok, LANES)),
                  per_l((LANES, HY_FFN)), per_l((1, HY_FFN)),
                  per_l((HY_FFN, HY_FFN)), per_l((1, HY_FFN)),
                  pl.BlockSpec((1, HY_FFN, ct), fcol), pl.BlockSpec((1, HY_FFN, ct), bcol),
                  per_l((2, HY_FFN)),
                  pl.BlockSpec((1, 1, ct), fcol), pl.BlockSpec((1, 1, ct), bcol),
                  const((n_tok, n_tok)), const((n_tok, n_tok))],
        out_specs=[out_spec, out_spec,
                   pl.BlockSpec((1, 1, 1, ct), lambda l, n, c: (l, n, 0, c))],
        out_shape=[jax.ShapeDtypeStruct((depth, HY_ORDER, n_tok, HY_W), F32),
                   jax.ShapeDtypeStruct((depth, HY_ORDER, n_tok, HY_W), F32),
                   jax.ShapeDtypeStruct((depth, HY_ORDER, 1, HY_W), F32)],
        scratch_shapes=[pltpu.VMEM((n_tok, HY_FFN), F32)],
        compiler_params=_cparams(("arbitrary",) * 3),
        name="hyena_spectra",
    )(feats, w1p, b1.reshape(depth, 1, HY_FFN), w2, b2.reshape(depth, 1, HY_FFN), w3, w3, sf,
      log_decay.reshape(depth, 1, -1), log_decay.reshape(depth, 1, -1), cmat, smat)


def _modulate(x, mod_ref):
    return (x * (1.0 + mod_ref[0, 1:2, :]) + mod_ref[0, 0:1, :]).astype(BF16)


def _inproj_kernel(x_ref, mod_ref, w_ref, o_ref, *, gates):
    acc = _dot(_modulate(x_ref[...], mod_ref), w_ref[0])
    if gates:
        tn = acc.shape[1]
        col = pl.program_id(0) * tn + lax.broadcasted_iota(jnp.int32, acc.shape, 1)
        sg = _sigmoid(acc)
        o_ref[...] = jnp.where(col >= N_BRANCH * D_MODEL, acc * sg, sg).astype(BF16)
    else:
        o_ref[...] = acc


def _inproj(x2d, mod, w, l, seq_len, gates):
    t = x2d.shape[0]
    n_out = w.shape[-1]
    per_batch = mod.shape[0] > 1
    tm = min(1024, seq_len if per_batch else t)
    tn = n_out // 2
    tiles_per_batch = seq_len // tm
    mod_idx = (lambda j, i: (i // tiles_per_batch, 0, 0)) if per_batch else (lambda j, i: (0, 0, 0))
    return pl.pallas_call(
        functools.partial(_inproj_kernel, gates=gates),
        grid=(n_out // tn, t // tm),
        in_specs=[pl.BlockSpec((tm, D_MODEL), lambda j, i: (i, 0)),
                  pl.BlockSpec((1, 3, D_MODEL), mod_idx),
                  pl.BlockSpec((1, D_MODEL, tn), lambda j, i: (l, 0, j))],
        out_specs=pl.BlockSpec((tm, tn), lambda j, i: (i, j)),
        out_shape=jax.ShapeDtypeStruct((t, n_out), BF16 if gates else F32),
        compiler_params=_cparams(("arbitrary", "arbitrary")),
        name="inproj_gates" if gates else "inproj",
    )(x2d, mod, w)


def _short_conv(x, w_ref, b_ref):
    n_tok = x.shape[0]
    row = lax.broadcasted_iota(jnp.int32, x.shape, 0)
    prev = jnp.where(row == 0, 0.0, pltpu.roll(x, 1, 0))
    nxt = jnp.where(row == n_tok - 1, 0.0, pltpu.roll(x, n_tok - 1, 0))
    return prev * w_ref[0, 0:1, :] + x * w_ref[0, 1:2, :] + nxt * w_ref[0, 2:3, :] + b_ref[0]


def _hyena_kernel(zin_ref, gin_ref, wz_ref, bz_ref, wg_ref, bg_ref, kr_ref, ki_ref, kny_ref, skip_ref,
                  c_ref, s_ref, o_ref, zb_ref, acc_ref, *, conv_z, tk):
    nb, n_tok, ct = zin_ref.shape
    for i in range(nb):
        z = zin_ref[i]
        if conv_z:
            z = _short_conv(z, wz_ref, bz_ref)
        zb_ref[i] = z.astype(BF16)
        row = lax.broadcasted_iota(jnp.int32, z.shape, 0)
        sign = jnp.where((row & 1) == 0, 1.0, -1.0)
        zny = jnp.sum(z * sign, axis=0, keepdims=True)
        acc_ref[i] = sign * (zny * kny_ref[0, 0]) + z * skip_ref[0, 0]
        for j in range(n_tok // tk):
            rows = slice(j * tk, (j + 1) * tk)
            zr = _dot(c_ref[rows, :], zb_ref[i])
            zs = _dot(s_ref[rows, :], zb_ref[i])
            kr = kr_ref[0, 0, rows, :]
            ki = ki_ref[0, 0, rows, :]
            yr = (zr * kr + zs * ki).astype(BF16)
            ym = (zs * kr - zr * ki).astype(BF16)
            acc_ref[i] += _dot(c_ref[:, rows], yr) + _dot(s_ref[:, rows], ym)
        o_ref[i] = (_short_conv(gin_ref[i], wg_ref, bg_ref) * acc_ref[i]).astype(o_ref.dtype)


def _hyena_order(zin, zblk, gin, gblk, conv_w, conv_b, kr, ki, kny, skip, l, order, conv_z):
    b, n_tok = zin.shape[0], zin.shape[1]
    cmat, smat = _dft_tables(n_tok)
    if n_tok >= 2048:
        ct, nb = 256, 1
    else:
        ct, nb = 512, min(8, b)
    nc = HY_W // ct
    tk = min(256, n_tok)
    const = lambda shape: pl.BlockSpec(shape, lambda c, i: (0,) * len(shape),
                                       pipeline_mode=pl.Buffered(1))
    zspec = pl.BlockSpec((nb, n_tok, ct), lambda c, i: (i, 0, zblk * nc + c))
    gspec = pl.BlockSpec((nb, n_tok, ct), lambda c, i: (i, 0, gblk * nc + c))
    kspec = pl.BlockSpec((1, 1, n_tok, ct), lambda c, i: (l, order, 0, c))
    vspec = pl.BlockSpec((1, 1, 1, ct), lambda c, i: (l, order, 0, c))
    wspec = lambda blk: pl.BlockSpec((1, 3, ct), lambda c, i: (l, 0, blk * nc + c))
    bspec = lambda blk: pl.BlockSpec((1, 1, ct), lambda c, i: (l, 0, blk * nc + c))
    return pl.pallas_call(
        functools.partial(_hyena_kernel, conv_z=conv_z, tk=tk),
        grid=(nc, b // nb),
        in_specs=[zspec, gspec, wspec(0), bspec(0), wspec(order + 1), bspec(order + 1),
                  kspec, kspec, vspec, vspec, const((n_tok, n_tok)), const((n_tok, n_tok))],
        out_specs=pl.BlockSpec((nb, n_tok, ct), lambda c, i: (i, 0, c)),
        out_shape=jax.ShapeDtypeStruct((b, n_tok, HY_W), F32 if conv_z else BF16),
        scratch_shapes=[pltpu.VMEM((nb, n_tok, ct), BF16), pltpu.VMEM((nb, n_tok, ct), F32)],
        compiler_params=_cparams(("arbitrary", "arbitrary")),
        name=f"hyena_order{order}",
    )(zin, gin, conv_w, conv_b, conv_w, conv_b, kr, ki, kny, skip, cmat, smat)


def _rms(x, g, n):
    ms = jnp.sum(x * x, axis=-1, keepdims=True) * (1.0 / n)
    return x * lax.rsqrt(ms + RMS_EPS) * g


def _da_values_ext(v):
    lane = lax.broadcasted_iota(jnp.int32, (v.shape[0], LANES), 1)
    ones_col = jnp.where(lane == 0, 1.0, 0.0).astype(BF16)
    pieces = []
    for h in range(DA_HEADS):
        pieces += [v[:, h * DA_V_DIM:(h + 1) * DA_V_DIM], ones_col]
    return jnp.concatenate(pieces, axis=1)


def _mla_values_ext(ckv_b, wv):
    v = _dot(ckv_b, wv)
    lane = lax.broadcasted_iota(jnp.int32, v.shape, 1)
    return jnp.where((lane & (MLA_V_PAD - 1)) == MLA_V, 1.0, v).astype(BF16)


def _prep_kernel(*refs, rope, n_state_in):
    it = iter(refs)
    q_ref, k_ref, v_ref, sm_ref, qn_ref, wuq_ref, kvn_ref, wk_ref, wv_ref, e_ref = (
        next(it) for _ in range(10))
    tab_ref, wuqr_ref, prot_ref = (next(it) for _ in range(3)) if rope else (None,) * 3
    for _ in range(n_state_in):
        next(it)
    qd_ref, kd_ref, vd_ref, qm_ref, km_ref, vm_ref = (next(it) for _ in range(6))
    q = q_ref[...] * (DA_HEAD_DIM ** -0.5)
    k = k_ref[...]
    v = v_ref[...]
    dq = sm_ref[:, 0:MLA_Q_LORA]
    dkv = sm_ref[:, MLA_Q_LORA:MLA_Q_LORA + MLA_KV_LORA]
    kr = sm_ref[:, 5 * LANES:6 * LANES]
    c_kv = _rms(dkv, kvn_ref[0], MLA_KV_LORA)
    if not rope:
        ks_ref, vs_ref, ckvs_ref, krs_ref = (next(it) for _ in range(4))
        nbt, _, n_tok, _ = ks_ref.shape
        ks_ref[:, 0] = k.reshape(nbt, n_tok, k.shape[1])
        vs_ref[:, 0] = v.reshape(nbt, n_tok, v.shape[1])
        ckvs_ref[:, 0] = c_kv.reshape(nbt, n_tok, MLA_KV_LORA)
        krs_ref[:, 0] = kr[:, :MLA_ROPE].reshape(nbt, n_tok, MLA_ROPE)
    if rope:
        def rot(x):
            hi, lo = _split_bf16(x)
            w = prot_ref.shape[0]
            return jnp.concatenate(
                [_dot(hi[:, c:c + w], prot_ref[...]) + _dot(lo[:, c:c + w], prot_ref[...])
                 for c in range(0, x.shape[1], w)], axis=1)
        q = _rope_mix(q, rot(q), tab_ref, 0)
        k = _rope_mix(k, rot(k), tab_ref, 0)
    qd_ref[...] = q.astype(BF16)
    kd_ref[...] = k.astype(BF16)

    mla_scale = (MLA_NOPE + MLA_ROPE) ** -0.5
    c_q = _rms(dq, qn_ref[0], MLA_Q_LORA).astype(BF16)
    qm = _dot(c_q, wuq_ref[0]) * mla_scale
    ckv_b = c_kv.astype(BF16)
    if rope:
        qm = _rope_mix(qm, _dot(c_q, wuqr_ref[0]) * mla_scale, tab_ref, 1)
        kr = _rope_apply(kr, tab_ref, 2, MLA_ROPE // 4)
        vd_ref[...] = _da_values_ext(v.astype(BF16))
        vm_ref[...] = _mla_values_ext(ckv_b, wv_ref[0])
    else:
        vd_ref[...] = v.astype(BF16)
        vm_ref[...] = _dot(ckv_b, wv_ref[0]).astype(BF16)
    qm_ref[...] = qm.astype(BF16)
    km_ref[...] = (_dot(ckv_b, wk_ref[0]) + _dot(kr.astype(BF16), e_ref[...])).astype(BF16)


def _prep(proj, wts, l, seq_len, rope, states=None):
    t = proj.shape[0]
    tm = min(512, seq_len if rope else t)
    tiles_per_seq = seq_len // tm
    blk = lambda w, idx: pl.BlockSpec((tm, w), lambda i: (i, idx))
    wl = lambda shape: pl.BlockSpec((1,) + shape, lambda i: (l, 0, 0))
    in_specs = [blk(512, COL_Q // 512), blk(512, COL_K // 512), blk(512, COL_V // 512),
                blk(SMALL_W, COL_SMALL // SMALL_W),
                wl((1, MLA_Q_LORA)), wl((MLA_Q_LORA, MLA_HEADS * MLA_HEAD_PAD)),
                wl((1, MLA_KV_LORA)), wl((MLA_KV_LORA, MLA_HEADS * MLA_HEAD_PAD)),
                wl((MLA_KV_LORA, MLA_HEADS * (MLA_V_PAD if rope else MLA_V))),
                pl.BlockSpec((LANES, MLA_HEADS * MLA_HEAD_PAD), lambda i: (0, 0))]
    args = [proj, proj, proj, proj, wts["q_norm"], wts["w_uq"], wts["kv_norm"], wts["w_uk"],
            wts["w_uv_pad" if rope else "w_uv"], wts["e_place"]]
    if rope:
        prot = _da_rot_matrix()
        in_specs += [pl.BlockSpec((3, 3, tm, LANES), lambda i: (0, 0, i % tiles_per_seq, 0)),
                     wl((MLA_Q_LORA, MLA_HEADS * MLA_HEAD_PAD)),
                     pl.BlockSpec(prot.shape, lambda i: (0, 0))]
        args += [_rope_tables(seq_len), wts["w_uq_rot"], prot]
    out = lambda w: pl.BlockSpec((tm, w), lambda i: (i, 0))
    widths = ((512, 512, DA_HEADS * 2 * DA_V_DIM, 1024, 1024, MLA_HEADS * MLA_V_PAD) if rope
              else (512, 512, DA_HEADS * DA_V_DIM, 1024, 1024, MLA_HEADS * MLA_V))
    out_specs = [out(w) for w in widths]
    out_shape = [jax.ShapeDtypeStruct((t, w), BF16) for w in widths]
    aliases = {}
    n_state_in = 0
    if not rope:
        nbt = tm // seq_len
        n_state_in = len(states)
        for k, s in enumerate(states):
            out_specs.append(pl.BlockSpec((nbt, 1, seq_len, s.shape[-1]), lambda i: (i, l, 0, 0)))
            out_shape.append(jax.ShapeDtypeStruct(s.shape, s.dtype))
            aliases[len(args)] = 6 + k
            in_specs.append(pl.BlockSpec(memory_space=pl.ANY))
            args.append(s)
    return pl.pallas_call(
        functools.partial(_prep_kernel, rope=rope, n_state_in=n_state_in),
        grid=(t // tm,),
        in_specs=in_specs,
        out_specs=out_specs,
        out_shape=out_shape,
        input_output_aliases=aliases,
        compiler_params=_cparams(("arbitrary",)),
        name="attn_prep",
    )(*args)


def _prep_cache_kernel(k_ref, v_ref, ckv_ref, kr_ref, wk_ref, wv_ref, e_ref,
                       kd_ref, vd_ref, km_ref, vm_ref):
    kd_ref[0] = k_ref[0, 0].astype(BF16)
    vd_ref[0] = _da_values_ext(v_ref[0, 0].astype(BF16))
    ckv_b = ckv_ref[0, 0].astype(BF16)
    km_ref[0] = (_dot(ckv_b, wk_ref[0]) + _dot(kr_ref[0, 0].astype(BF16), e_ref[...])).astype(BF16)
    vm_ref[0] = _mla_values_ext(ckv_b, wv_ref[0])


def _prep_cache(cache_k, cache_v, cache_ckv, cache_kr, wts, l):
    b, _, p = cache_k.shape[:3]
    ck = cache_k.reshape(b, -1, p, 512)
    cv = cache_v.reshape(b, -1, p, 512)
    cin = lambda w: pl.BlockSpec((1, 1, p, w), lambda i: (i, l, 0, 0))
    wl = lambda shape: pl.BlockSpec((1,) + shape, lambda i: (l, 0, 0))
    out = lambda w: pl.BlockSpec((1, p, w), lambda i: (i, 0, 0))
    return pl.pallas_call(
        _prep_cache_kernel,
        grid=(b,),
        in_specs=[cin(512), cin(512), cin(MLA_KV_LORA), cin(MLA_ROPE),
                  wl((MLA_KV_LORA, MLA_HEADS * MLA_HEAD_PAD)), wl((MLA_KV_LORA, MLA_HEADS * MLA_V_PAD)),
                  pl.BlockSpec((MLA_ROPE, MLA_HEADS * MLA_HEAD_PAD), lambda i: (0, 0))],
        out_specs=[out(512), out(1024), out(1024), out(MLA_HEADS * MLA_V_PAD)],
        out_shape=[jax.ShapeDtypeStruct((b, p, 512), BF16), jax.ShapeDtypeStruct((b, p, 1024), BF16),
                   jax.ShapeDtypeStruct((b, p, 1024), BF16),
                   jax.ShapeDtypeStruct((b, p, MLA_HEADS * MLA_V_PAD), BF16)],
        compiler_params=_cparams(("arbitrary",)),
        name="attn_prep_cache",
    )(ck, cv, cache_ckv, cache_kr, wts["w_uk"], wts["w_uv_pad"], wts["e_place"][:MLA_ROPE])


def _softmax_pv(q, k_parts, v_parts, dv, den_on_mxu):
    s = [_dot_nt(q, k) for k in k_parts]
    m = s[0].max(axis=-1, keepdims=True)
    for sp in s[1:]:
        m = jnp.maximum(m, sp.max(axis=-1, keepdims=True))
    o = None
    den = None
    for sp, v in zip(s, v_parts):
        if den_on_mxu:
            pv = _dot(jnp.exp((sp - m).astype(BF16)), v)
        else:
            p = jnp.exp(sp - m)
            psum = p.sum(axis=-1, keepdims=True)
            den = psum if den is None else den + psum
            pv = _dot(p.astype(BF16), v)
        o = pv if o is None else o + pv
    if den_on_mxu:
        return o[:, :dv] / o[:, dv:dv + 1]
    return o / den


def _da_kernel(*refs, lam_init, has_ctx):
    if has_ctx:
        q_ref, kn_ref, vn_ref, kc_ref, vc_ref, lam_ref, g_ref, o_ref = refs
    else:
        q_ref, kn_ref, vn_ref, lam_ref, g_ref, o_ref = refs
    lp = lam_ref[0]
    lam = (jnp.exp(jnp.sum(lp[0:1] * lp[1:2], axis=1, keepdims=True))
           - jnp.exp(jnp.sum(lp[2:3] * lp[3:4], axis=1, keepdims=True)) + lam_init)
    for bi in range(q_ref.shape[0]):
        for h in range(DA_HEADS):
            vw = 2 * DA_V_DIM if has_ctx else DA_V_DIM
            vcols = slice(h * vw, (h + 1) * vw)
            outs = []
            for j in range(2):
                cols = slice((j * DA_HEADS + h) * DA_HEAD_DIM, (j * DA_HEADS + h + 1) * DA_HEAD_DIM)
                k_parts = [kn_ref[bi, :, cols]]
                v_parts = [vn_ref[bi, :, vcols]]
                if has_ctx:
                    k_parts.insert(0, kc_ref[bi, :, cols])
                    v_parts.insert(0, vc_ref[bi, :, vcols])
                outs.append(_softmax_pv(q_ref[bi, :, cols], k_parts, v_parts, DA_V_DIM, has_ctx))
            o = outs[0] - lam * outs[1]
            o_ref[bi, :, h * DA_V_DIM:(h + 1) * DA_V_DIM] = (
                _rms(o, g_ref[0], DA_V_DIM) * (1.0 - lam_init)).astype(o_ref.dtype)


def _da_attention(qd, kd, vd, ctx, da_lambda, da_subln, l, tq, nbq):
    b, n, _ = qd.shape
    has_ctx = ctx is not None
    lam_init = 0.8 - 0.6 * math.exp(-0.3 * l)
    full = lambda a: pl.BlockSpec((nbq,) + a.shape[1:], lambda i, j: (i, 0, 0))
    in_specs = [pl.BlockSpec((nbq, tq, 512), lambda i, j: (i, j, 0)), full(kd), full(vd)]
    args = [qd, kd, vd]
    if has_ctx:
        in_specs += [full(ctx[0]), full(ctx[1])]
        args += [ctx[0], ctx[1]]
    in_specs += [pl.BlockSpec((1, 4, DA_HEAD_DIM), lambda i, j: (l, 0, 0)),
                 pl.BlockSpec((1, 1, DA_V_DIM), lambda i, j: (l, 0, 0))]
    args += [da_lambda, da_subln.reshape(-1, 1, DA_V_DIM)]
    return pl.pallas_call(
        functools.partial(_da_kernel, lam_init=lam_init, has_ctx=has_ctx),
        grid=(b // nbq, n // tq),
        in_specs=in_specs,
        out_specs=pl.BlockSpec((nbq, tq, 512), lambda i, j: (i, j, 0)),
        out_shape=jax.ShapeDtypeStruct((b, n, 512), BF16),
        compiler_params=_cparams(("arbitrary", "arbitrary")),
        name="diff_attention",
    )(*args)


def _mla_kernel(*refs, has_ctx):
    if has_ctx:
        q_ref, kn_ref, vn_ref, kc_ref, vc_ref, o_ref = refs
    else:
        q_ref, kn_ref, vn_ref, o_ref = refs
    for bi in range(q_ref.shape[0]):
        for h in range(MLA_HEADS):
            cols = slice(h * MLA_HEAD_PAD, (h + 1) * MLA_HEAD_PAD)
            vw = MLA_V_PAD if has_ctx else MLA_V
            vcols = slice(h * vw, (h + 1) * vw)
            k_parts = [kn_ref[bi, :, cols]]
            v_parts = [vn_ref[bi, :, vcols]]
            if has_ctx:
                k_parts.insert(0, kc_ref[bi, :, cols])
                v_parts.insert(0, vc_ref[bi, :, vcols])
            o_ref[bi, :, h * MLA_V:(h + 1) * MLA_V] = _softmax_pv(
                q_ref[bi, :, cols], k_parts, v_parts, MLA_V, has_ctx).astype(o_ref.dtype)


def _mla_attention(qm, km, vm, ctx, tq, nbq):
    b, n, _ = qm.shape
    has_ctx = ctx is not None
    full = lambda a: pl.BlockSpec((nbq,) + a.shape[1:], lambda i, j: (i, 0, 0))
    in_specs = [pl.BlockSpec((nbq, tq, 1024), lambda i, j: (i, j, 0)), full(km), full(vm)]
    args = [qm, km, vm]
    if has_ctx:
        in_specs += [full(ctx[0]), full(ctx[1])]
        args += [ctx[0], ctx[1]]
    return pl.pallas_call(
        functools.partial(_mla_kernel, has_ctx=has_ctx),
        grid=(b // nbq, n // tq),
        in_specs=in_specs,
        out_specs=pl.BlockSpec((nbq, tq, 512), lambda i, j: (i, j, 0)),
        out_shape=jax.ShapeDtypeStruct((b, n, 512), BF16),
        compiler_params=_cparams(("arbitrary", "arbitrary")),
        name="latent_attention",
    )(*args)


def _outproj_kernel(x_ref, mod_ref, yh_ref, yd_ref, ym_ref, p0_ref, p1_ref, p2_ref,
                    m0_ref, m1_ref, m2_ref, wb_ref, wo_ref, g_ref, b_ref, o_ref, *, alpha):
    merged = None
    for y_ref, p_ref, m_ref, n in ((yh_ref, p0_ref, m0_ref, 0), (yd_ref, p1_ref, m1_ref, 1),
                                   (ym_ref, p2_ref, m2_ref, 2)):
        br = (y_ref[...].astype(F32) * p_ref[...].astype(F32)).astype(BF16)
        term = m_ref[...].astype(F32) * _dot(br, wb_ref[0, n])
        merged = term if merged is None else merged + term
    out = _dot(merged.astype(BF16), wo_ref[0])
    gate = mod_ref[0, 2:3, :]
    y = alpha * x_ref[...] + gate * out
    mu = jnp.mean(y, axis=-1, keepdims=True)
    yc = y - mu
    var = jnp.mean(yc * yc, axis=-1, keepdims=True)
    o_ref[...] = yc * lax.rsqrt(var + LN_EPS) * g_ref[0] + b_ref[0]


def _outproj(x2d, mod, y_hy, y_da, y_mla, gates, w_branch, w_out, ln_g, ln_b, l, seq_len, alpha):
    t = x2d.shape[0]
    per_batch = mod.shape[0] > 1
    tm = min(1024, seq_len if per_batch else t)
    tiles_per_batch = seq_len // tm
    mod_idx = (lambda i: (i // tiles_per_batch, 0, 0)) if per_batch else (lambda i: (0, 0, 0))
    row = lambda w, idx=0: pl.BlockSpec((tm, w), lambda i: (i, idx))
    depth = ln_g.shape[0]
    return pl.pallas_call(
        functools.partial(_outproj_kernel, alpha=alpha),
        grid=(t // tm,),
        in_specs=[row(D_MODEL), pl.BlockSpec((1, 3, D_MODEL), mod_idx),
                  row(BR_W), row(BR_W), row(BR_W),
                  row(BR_W, GATE_PATHS // BR_W), row(BR_W, GATE_PATHS // BR_W + 1),
                  row(BR_W, GATE_PATHS // BR_W + 2),
                  row(D_MODEL, GATE_MERGE // D_MODEL), row(D_MODEL, GATE_MERGE // D_MODEL + 1),
                  row(D_MODEL, GATE_MERGE // D_MODEL + 2),
                  pl.BlockSpec((1, N_BRANCH, BR_W, D_MODEL), lambda i: (l, 0, 0, 0),
                               pipeline_mode=pl.Buffered(1)),
                  pl.BlockSpec((1, D_MODEL, D_MODEL), lambda i: (l, 0, 0),
                               pipeline_mode=pl.Buffered(1)),
                  pl.BlockSpec((1, 1, D_MODEL), lambda i: (l, 0, 0)),
                  pl.BlockSpec((1, 1, D_MODEL), lambda i: (l, 0, 0))],
        out_specs=row(D_MODEL),
        out_shape=jax.ShapeDtypeStruct((t, D_MODEL), F32),
        compiler_params=_cparams(("arbitrary",)),
        name="outproj",
    )(x2d, mod, y_hy, y_da, y_mla, gates, gates, gates, gates, gates, gates, w_branch, w_out,
      ln_g.reshape(depth, 1, D_MODEL), ln_b.reshape(depth, 1, D_MODEL))


def _layer(x, mod, l, wts, spectra, ctx=None, states=None):
    b, n, _ = x.shape
    depth = wts["ln_g"].shape[0]
    alpha = (2 * depth) ** 0.25
    x2d = x.reshape(b * n, D_MODEL)
    proj = _inproj(x2d, mod, wts["w_main"], l, n, gates=False)
    gates = _inproj(x2d, mod, wts["w_gate"], l, n, gates=True)
    proj3 = proj.reshape(b, n, D_IN_MAIN)

    kr_s, ki_s, kny_s = spectra
    hy_blk = COL_HY // HY_W
    conv = (wts["conv_w"], wts["conv_b"])
    z2 = _hyena_order(proj3, hy_blk, proj3, hy_blk + 1, *conv, kr_s, ki_s, kny_s, wts["skip"], l, 0, True)
    y_hy = _hyena_order(z2, 0, proj3, hy_blk + 2, *conv, kr_s, ki_s, kny_s, wts["skip"], l, 1, False)

    prep_out = _prep(proj, wts, l, n, rope=ctx is not None, states=states)
    r3 = lambda a: a.reshape(b, n, a.shape[-1])
    qd, kd, vd, qm, km, vm = map(r3, prep_out[:6])
    new_states = tuple(prep_out[6:]) if ctx is None else None
    if ctx is not None:
        kd_c, vd_c, km_c, vm_c = _prep_cache(*ctx, wts, l)
        da_ctx, mla_ctx = (kd_c, vd_c), (km_c, vm_c)
    else:
        da_ctx = mla_ctx = None
    tq = min(ATTN_TQ, n)
    nbq = 2 if (ctx is None and b % 2 == 0) else 1
    y_da = _da_attention(qd, kd, vd, da_ctx, wts["da_lambda"], wts["da_subln"], l, tq, nbq)
    y_mla = _mla_attention(qm, km, vm, mla_ctx, tq, 1)

    x_new = _outproj(x2d, mod, y_hy.reshape(b * n, HY_W), y_da.reshape(b * n, BR_W),
                     y_mla.reshape(b * n, BR_W), gates, wts["w_branch"], wts["w_out"],
                     wts["ln_g"], wts["ln_b"], l, n, alpha)
    return x_new.reshape(b, n, D_MODEL), new_states


def _prepare_weights(w_in, hy_conv_w, hy_conv_b, hy_ffn_w1, hy_skip, da_lambda, da_subln,
                     mla_q_norm, mla_w_uq, mla_kv_norm, mla_w_ukv, w_branch, w_out, ln_g, ln_b):
    depth = w_in.shape[0]
    w_main = jnp.pad(w_in[..., :ORIG_PATHS].astype(BF16),
                     ((0, 0), (0, 0), (0, SMALL_W - SMALL_USED)))
    w_gate = jnp.concatenate([w_in[..., ORIG_MERGE:ORIG_END], w_in[..., ORIG_PATHS:ORIG_MERGE]],
                             axis=-1).astype(BF16)
    uq = mla_w_uq.reshape(depth, MLA_Q_LORA, MLA_HEADS, MLA_NOPE + MLA_ROPE)
    uq = jnp.pad(uq, ((0, 0), (0, 0), (0, 0), (0, MLA_HEAD_PAD - MLA_NOPE - MLA_ROPE)))
    w_uq = uq.reshape(depth, MLA_Q_LORA, MLA_HEADS * MLA_HEAD_PAD).astype(BF16)
    ukv = mla_w_ukv.reshape(depth, MLA_KV_LORA, MLA_HEADS, MLA_NOPE + MLA_V)
    uk = jnp.pad(ukv[..., :MLA_NOPE], ((0, 0), (0, 0), (0, 0), (0, MLA_HEAD_PAD - MLA_NOPE)))
    w_uk = uk.reshape(depth, MLA_KV_LORA, MLA_HEADS * MLA_HEAD_PAD).astype(BF16)
    w_uv = ukv[..., MLA_NOPE:].reshape(depth, MLA_KV_LORA, MLA_HEADS * MLA_V).astype(BF16)
    uv = jnp.pad(ukv[..., MLA_NOPE:], ((0, 0), (0, 0), (0, 0), (0, MLA_V_PAD - MLA_V)))
    w_uv_pad = uv.reshape(depth, MLA_KV_LORA, MLA_HEADS * MLA_V_PAD).astype(BF16)
    src, sign = _rot_half_spec(MLA_HEADS * MLA_HEAD_PAD, MLA_HEAD_PAD, ((MLA_NOPE, MLA_ROPE),))
    w_uq_rot = w_uq[..., src] * jnp.asarray(sign, dtype=BF16)
    e = np.zeros((LANES, MLA_HEADS * MLA_HEAD_PAD), np.float32)
    for h in range(MLA_HEADS):
        for i in range(MLA_ROPE):
            e[i, h * MLA_HEAD_PAD + MLA_NOPE + i] = 1.0
    w1p = jnp.pad(hy_ffn_w1, ((0, 0), (0, LANES - hy_ffn_w1.shape[1]), (0, 0)))
    return {
        "w_main": w_main, "w_gate": w_gate, "conv_w": hy_conv_w, "conv_b": hy_conv_b.reshape(depth, 1, -1),
        "skip": hy_skip.reshape(depth, HY_ORDER, 1, HY_W), "w1p": w1p,
        "da_lambda": da_lambda, "da_subln": da_subln,
        "q_norm": mla_q_norm.reshape(depth, 1, -1), "kv_norm": mla_kv_norm.reshape(depth, 1, -1),
        "w_uq": w_uq, "w_uq_rot": w_uq_rot, "w_uk": w_uk, "w_uv": w_uv, "w_uv_pad": w_uv_pad, "e_place": jnp.asarray(e, dtype=BF16),
        "w_branch": w_branch.astype(BF16), "w_out": w_out.astype(BF16), "ln_g": ln_g, "ln_b": ln_b,
    }


def kernel(x_prompt, x_sample, c, cache_diff_k, cache_diff_v, cache_mla_ckv, cache_mla_krope, c_ctx, w_mod, b_mod, w_in, hy_conv_w, hy_conv_b, hy_ffn_w1, hy_ffn_b1, hy_ffn_w2, hy_ffn_b2, hy_ffn_w3, hy_sin_freq, hy_log_decay, hy_skip, da_lambda, da_subln, mla_q_norm, mla_w_uq, mla_kv_norm, mla_w_ukv, w_branch, w_out, ln_g, ln_b):
    depth = w_in.shape[0]
    b_ctx, n_ctx, _ = x_prompt.shape
    b_lat, n_lat, _ = x_sample.shape
    wts = _prepare_weights(w_in, hy_conv_w, hy_conv_b, hy_ffn_w1, hy_skip, da_lambda, da_subln,
                           mla_q_norm, mla_w_uq, mla_kv_norm, mla_w_ukv, w_branch, w_out, ln_g, ln_b)

    rows = -(-(1 + b_lat) // 8) * 8
    cvec = jnp.zeros((rows, D_MODEL), F32).at[0].set(c_ctx).at[1:1 + b_lat].set(c)
    mods = _modulation(cvec, w_mod, b_mod).reshape(depth, rows, 3, D_MODEL)

    spec_args = (wts["w1p"], hy_ffn_b1, hy_ffn_w2, hy_ffn_b2, hy_ffn_w3, hy_sin_freq, hy_log_decay)
    spectra_ctx = _hyena_spectra(n_ctx, *spec_args)
    spectra_lat = spectra_ctx if n_lat == n_ctx else _hyena_spectra(n_lat, *spec_args)

    xp = x_prompt
    states = tuple(jnp.zeros((b_ctx, depth, n_ctx, w), F32)
                   for w in (2 * DA_HEADS * DA_HEAD_DIM, DA_HEADS * DA_V_DIM, MLA_KV_LORA, MLA_ROPE))
    for l in range(depth):
        xp, states = _layer(xp, mods[l, 0:1], l, wts, spectra_ctx, states=states)
    state_k = states[0].reshape(b_ctx, depth, n_ctx, 2, DA_HEADS, DA_HEAD_DIM)
    state_v = states[1].reshape(b_ctx, depth, n_ctx, DA_HEADS, DA_V_DIM)
    state_ckv, state_kr = states[2], states[3]

    xs = x_sample
    ctx = (cache_diff_k, cache_diff_v, cache_mla_ckv, cache_mla_krope)
    for l in range(depth):
        xs, _ = _layer(xs, mods[l, 1:1 + b_lat], l, wts, spectra_lat, ctx=ctx)

    return (xp, xs, state_k, state_v, state_ckv, state_kr)
```
